```python
import jax, jax.numpy as jnp
from jax import lax
import numpy as np

D_MODEL = 1024
BATCH = 8
SEQ = 2048
DEPTH = 1
DEC_BATCH = 128
DEC_SEQ = 8
PAST_LEN = 8192
PAGE_SIZE = 128

D_MIX = D_MODEL
D_ATTN = D_MIX // 2
D_SSM = D_MIX - D_ATTN
V_HEAD = 64
MLA_HEADS = D_ATTN // V_HEAD
QK_NOPE = 64
QK_ROPE = 32
KV_LORA = D_MODEL // 4
Q_LORA = (3 * D_MODEL) // 8
ROPE_THETA = 10000.0
SSM_HEADDIM = 64
SSM_HEADS = D_SSM // SSM_HEADDIM
SSM_GROUPS = 2
D_STATE = 128
CONV_K = 4
CHUNK = 256
CONV_DIM = D_SSM + 2 * SSM_GROUPS * D_STATE
D_FF = 4 * D_MODEL
N_IN = Q_LORA + KV_LORA + QK_ROPE + D_SSM + CONV_DIM + SSM_HEADS
EPS = 1e-6
QBLOCK = 128

kernel_name = 'hymba_mla_ssd_decode_step'


def rmsnorm(x, g):
    xf = x.astype(jnp.float32)
    xf = xf * lax.rsqrt(jnp.mean(xf * xf, axis=-1, keepdims=True) + EPS)
    return (xf * g.astype(jnp.float32)).astype(x.dtype)


def rope_tables(pos):
    inv = ROPE_THETA ** (-(jnp.arange(0, QK_ROPE, 2, dtype=jnp.float32) / QK_ROPE))
    ang = pos.astype(jnp.float32)[:, None] * inv[None, :]
    return jnp.cos(ang), jnp.sin(ang)


def apply_rope(x, cos, sin):
    half = QK_ROPE // 2
    xf = x.astype(jnp.float32)
    x1, x2 = xf[..., :half], xf[..., half:]
    return jnp.concatenate([x1 * cos - x2 * sin, x2 * cos + x1 * sin], axis=-1).astype(x.dtype)


def mla_attend(q_abs, q_pe, c_kv, k_pe, q_pos, k_pos):
    b, lq, h, r = q_abs.shape
    qb = QBLOCK if lq % QBLOCK == 0 else lq
    nb = lq // qb
    scale = (QK_NOPE + QK_ROPE) ** -0.5
    qa = jnp.moveaxis(q_abs.reshape(b, nb, qb, h, r), 1, 0)
    qp = jnp.moveaxis(q_pe.reshape(b, nb, qb, h, QK_ROPE), 1, 0)
    qpos = q_pos.reshape(nb, qb)

    def block(args):
        qa_b, qp_b, pos_b = args
        s = (jnp.einsum('bqhr,bkr->bhqk', qa_b, c_kv) + jnp.einsum('bqhd,bkd->bhqk', qp_b, k_pe)).astype(jnp.float32) * scale
        mask = k_pos[None, :] <= pos_b[:, None]
        s = jnp.where(mask[None, None], s, -jnp.inf)
        p = jax.nn.softmax(s, axis=-1).astype(c_kv.dtype)
        return jnp.einsum('bhqk,bkr->bqhr', p, c_kv)

    out = lax.map(block, (qa, qp, qpos))
    return jnp.moveaxis(out, 0, 1).reshape(b, lq, h, r)


def ssd_chunked(x, dt, A, Bm, Cm, h0):
    b, L, H, P = x.shape
    G, N = Bm.shape[2], Bm.shape[3]
    E = H // G
    Q = min(CHUNK, L)
    pad = (-L) % Q
    x = x.astype(jnp.float32)
    dt = dt.astype(jnp.float32)
    Bm = Bm.astype(jnp.float32)
    Cm = Cm.astype(jnp.float32)
    if pad:
        x = jnp.pad(x, ((0, 0), (0, pad), (0, 0), (0, 0)))
        dt = jnp.pad(dt, ((0, 0), (0, pad), (0, 0)))
        Bm = jnp.pad(Bm, ((0, 0), (0, pad), (0, 0), (0, 0)))
        Cm = jnp.pad(Cm, ((0, 0), (0, pad), (0, 0), (0, 0)))
    nc = (L + pad) // Q
    xc = x.reshape(b, nc, Q, G, E, P)
    dtc = dt.reshape(b, nc, Q, G, E)
    Bc = Bm.reshape(b, nc, Q, G, N)
    Cc = Cm.reshape(b, nc, Q, G, N)
    acs = jnp.cumsum(dtc * A.astype(jnp.float32).reshape(G, E), axis=2)
    causal = jnp.tril(jnp.ones((Q, Q), dtype=bool))
    seg = acs[:, :, :, None] - acs[:, :, None, :]
    lmat = jnp.exp(jnp.where(causal[:, :, None, None], seg, -jnp.inf))
    x_dt = xc * dtc[..., None]
    cb = jnp.einsum('bcqgn,bcsgn->bcqsg', Cc, Bc)
    y_diag = jnp.einsum('bcqsge,bcsgep->bcqgep', cb[..., None] * lmat, x_dt)
    decay_to_end = jnp.exp(acs[:, :, -1:] - acs)
    chunk_states = jnp.einsum('bcsgn,bcsgep->bcgepn', Bc, x_dt * decay_to_end[..., None])
    chunk_decay = jnp.exp(acs[:, :, -1])

    def step(h, inp):
        dec, st = inp
        return dec[..., None, None] * h + st, h

    h_init = h0.astype(jnp.float32).reshape(b, G, E, P, N)
    h_final, h_prev = lax.scan(step, h_init, (jnp.moveaxis(chunk_decay, 1, 0), jnp.moveaxis(chunk_states, 1, 0)))
    h_prev = jnp.moveaxis(h_prev, 0, 1)
    y_off = jnp.einsum('bcqgn,bcgepn->bcqgep', Cc, h_prev) * jnp.exp(acs)[..., None]
    y = (y_diag + y_off).reshape(b, nc * Q, H, P)[:, :L]
    return y, h_final.reshape(b, H, P, N)


def mixer_block(x, c_past, kr_past, h0, conv_buf, norm_attn, w_in, q_norm, kv_norm, w_uq, w_uk, w_uv,
                conv_w, conv_b, dt_bias, a_log, d_skip, ssm_norm, w_out):
    b, L, _ = x.shape
    past_len = c_past.shape[1]
    h = rmsnorm(x, norm_attn)
    proj = jnp.einsum('bld,dn->bln', h, w_in)
    o1 = Q_LORA
    o2 = o1 + KV_LORA
    o3 = o2 + QK_ROPE
    o4 = o3 + D_SSM
    o5 = o4 + CONV_DIM
    q_lat, kv_lat, k_pe, z, xbc, dt_raw = jnp.split(proj, [o1, o2, o3, o4, o5], axis=-1)
    q_pos = past_len + jnp.arange(L, dtype=jnp.int32)
    cos, sin = rope_tables(q_pos)
    q = jnp.einsum('blr,rn->bln', rmsnorm(q_lat, q_norm), w_uq).reshape(b, L, MLA_HEADS, QK_NOPE + QK_ROPE)
    q_nope, q_pe = q[..., :QK_NOPE], q[..., QK_NOPE:]
    q_pe = apply_rope(q_pe, cos[:, None, :], sin[:, None, :])
    c_new = rmsnorm(kv_lat, kv_norm)
    kr_new = apply_rope(k_pe, cos, sin)
    q_abs = jnp.einsum('blhd,hdr->blhr', q_nope, w_uk)
    c_all = jnp.concatenate([c_past.astype(c_new.dtype), c_new], axis=1)
    kr_all = jnp.concatenate([kr_past.astype(kr_new.dtype), kr_new], axis=1)
    k_pos = jnp.arange(past_len + L, dtype=jnp.int32)
    o_lat = mla_attend(q_abs, q_pe, c_all, kr_all, q_pos, k_pos)
    o_attn = jnp.einsum('blhr,hrv->blhv', o_lat, w_uv).reshape(b, L, D_ATTN)
    xpad = jnp.concatenate([conv_buf.astype(xbc.dtype), xbc], axis=1)
    conv = conv_b
    for k in range(CONV_K):
        conv = conv + xpad[:, k:k + L] * conv_w[k]
    conv = jax.nn.silu(conv)
    conv_new = xpad[:, L:]
    xs, Bm, Cm = jnp.split(conv, [D_SSM, D_SSM + SSM_GROUPS * D_STATE], axis=-1)
    xs = xs.reshape(b, L, SSM_HEADS, SSM_HEADDIM)
    dt = jax.nn.softplus(dt_raw.astype(jnp.float32) + dt_bias.astype(jnp.float32))
    A = -jnp.exp(a_log.astype(jnp.float32))
    y, h_new = ssd_chunked(xs, dt, A, Bm.reshape(b, L, SSM_GROUPS, D_STATE), Cm.reshape(b, L, SSM_GROUPS, D_STATE), h0)
    y = y + d_skip.astype(jnp.float32)[:, None] * xs.astype(jnp.float32)
    g = y.reshape(b, L, D_SSM) * jax.nn.silu(z.astype(jnp.float32))
    g = g.reshape(b, L, SSM_GROUPS, D_SSM // SSM_GROUPS)
    g = g * lax.rsqrt(jnp.mean(g * g, axis=-1, keepdims=True) + EPS)
    o_ssm = (g.reshape(b, L, D_SSM) * ssm_norm.astype(jnp.float32)).astype(x.dtype)
    out = jnp.einsum('blm,md->bld', jnp.concatenate([o_attn, o_ssm], axis=-1), w_out)
    return out, c_new, kr_new, h_new.astype(h0.dtype), conv_new


def sq_relu_mlp(x, g, w_up, w_down):
    u = jax.nn.relu(jnp.einsum('bld,df->blf', rmsnorm(x, g), w_up))
    return jnp.einsum('blf,fd->bld', u * u, w_down)


def setup_inputs(seed: int = 0) -> dict:
    key = jax.random.key(seed)
    ks = jax.random.split(key, 32)
    n_pages = PAST_LEN // PAGE_SIZE
    used = DEC_BATCH * n_pages
    n_pool = used + used // 4
    f32 = jnp.float32

    def nrm(k, shape, scale):
        return jax.random.normal(k, shape, f32) * scale

    dt0 = jnp.exp(jax.random.uniform(ks[14], (DEPTH, SSM_HEADS), f32, np.log(1e-3), np.log(1e-1)))
    return {
        'x_prompt': nrm(ks[0], (BATCH, SEQ, D_MODEL), 1.0),
        'x_sample': nrm(ks[1], (DEC_BATCH, DEC_SEQ, D_MODEL), 1.0),
        'cache_kv_latent': nrm(ks[2], (DEPTH, n_pool, PAGE_SIZE, KV_LORA), 1.0),
        'cache_k_rope': nrm(ks[3], (DEPTH, n_pool, PAGE_SIZE, QK_ROPE), 1.0),
        'state_ssm': nrm(ks[4], (DEPTH, DEC_BATCH, SSM_HEADS, SSM_HEADDIM, D_STATE), 0.1),
        'state_conv': nrm(ks[5], (DEPTH, DEC_BATCH, CONV_K - 1, CONV_DIM), 1.0),
        'page_table': jax.random.permutation(ks[6], n_pool)[:used].reshape(DEC_BATCH, n_pages).astype(jnp.int32),
        'norm_attn': 1.0 + nrm(ks[7], (DEPTH, D_MODEL), 0.02),
        'w_in': nrm(ks[8], (DEPTH, D_MODEL, N_IN), D_MODEL ** -0.5),
        'q_norm': 1.0 + nrm(ks[9], (DEPTH, Q_LORA), 0.02),
        'kv_norm': 1.0 + nrm(ks[10], (DEPTH, KV_LORA), 0.02),
        'w_uq': nrm(ks[11], (DEPTH, Q_LORA, MLA_HEADS * (QK_NOPE + QK_ROPE)), Q_LORA ** -0.5),
        'w_uk': nrm(ks[12], (DEPTH, MLA_HEADS, QK_NOPE, KV_LORA), KV_LORA ** -0.5),
        'w_uv': nrm(ks[13], (DEPTH, MLA_HEADS, KV_LORA, V_HEAD), KV_LORA ** -0.5),
        'conv_w': nrm(ks[15], (DEPTH, CONV_K, CONV_DIM), CONV_K ** -0.5),
        'conv_b': nrm(ks[16], (DEPTH, CONV_DIM), 0.02),
        'dt_bias': dt0 + jnp.log(-jnp.expm1(-dt0)),
        'a_log': jnp.log(jax.random.uniform(ks[17], (DEPTH, SSM_HEADS), f32, 1.0, 16.0)),
        'd_skip': 1.0 + nrm(ks[18], (DEPTH, SSM_HEADS), 0.1),
        'ssm_norm': 1.0 + nrm(ks[19], (DEPTH, D_SSM), 0.02),
        'w_out': nrm(ks[20], (DEPTH, D_MIX, D_MODEL), D_MIX ** -0.5),
        'norm_mlp': 1.0 + nrm(ks[21], (DEPTH, D_MODEL), 0.02),
        'w_up': nrm(ks[22], (DEPTH, D_MODEL, D_FF), D_MODEL ** -0.5),
        'w_down': nrm(ks[23], (DEPTH, D_FF, D_MODEL), D_FF ** -0.5),
        'norm_final': 1.0 + nrm(ks[24], (D_MODEL,), 0.02),
    }


def reference(x_prompt, x_sample, cache_kv_latent, cache_k_rope, state_ssm, state_conv, page_table,
              norm_attn, w_in, q_norm, kv_norm, w_uq, w_uk, w_uv, conv_w, conv_b, dt_bias, a_log, d_skip,
              ssm_norm, w_out, norm_mlp, w_up, w_down, norm_final):
    bp = x_prompt.shape[0]
    bs, n_pages = page_table.shape
    page = cache_kv_latent.shape[2]
    xp, xs = x_prompt, x_sample
    ckv_p, ckr_p, ssm_p, conv_p = [], [], [], []
    ckv_s, ckr_s, ssm_s, conv_s = [], [], [], []
    for l in range(DEPTH):
        lw = (norm_attn[l], w_in[l], q_norm[l], kv_norm[l], w_uq[l], w_uk[l], w_uv[l], conv_w[l], conv_b[l],
              dt_bias[l], a_log[l], d_skip[l], ssm_norm[l], w_out[l])
        att, c_new, kr_new, h_new, cv_new = mixer_block(
            xp, jnp.zeros((bp, 0, KV_LORA), xp.dtype), jnp.zeros((bp, 0, QK_ROPE), xp.dtype),
            jnp.zeros((bp, SSM_HEADS, SSM_HEADDIM, D_STATE), xp.dtype), jnp.zeros((bp, CONV_K - 1, CONV_DIM), xp.dtype), *lw)
        xp = xp + att
        xp = xp + sq_relu_mlp(xp, norm_mlp[l], w_up[l], w_down[l])
        ckv_p.append(c_new); ckr_p.append(kr_new); ssm_p.append(h_new); conv_p.append(cv_new)
        c_past = cache_kv_latent[l][page_table].reshape(bs, n_pages * page, KV_LORA)
        kr_past = cache_k_rope[l][page_table].reshape(bs, n_pages * page, QK_ROPE)
        att, c_new, kr_new, h_new, cv_new = mixer_block(xs, c_past, kr_past, state_ssm[l], state_conv[l], *lw)
        xs = xs + att
        xs = xs + sq_relu_mlp(xs, norm_mlp[l], w_up[l], w_down[l])
        ckv_s.append(c_new); ckr_s.append(kr_new); ssm_s.append(h_new); conv_s.append(cv_new)
    y_prompt = rmsnorm(xp, norm_final)
    y_sample = rmsnorm(xs, norm_final)
    return (y_prompt, y_sample,
            jnp.stack(ckv_p), jnp.stack(ckr_p), jnp.stack(ssm_p), jnp.stack(conv_p),
            jnp.stack(ckv_s), jnp.stack(ckr_s), jnp.stack(ssm_s), jnp.stack(conv_s))
```

```python
import functools

import jax
import jax.numpy as jnp
from jax import lax
from jax.experimental import pallas as pl
from jax.experimental.pallas import tpu as pltpu

D_MODEL = 1024
D_ATTN = 512
D_SSM = 512
V_HEAD = 64
MLA_HEADS = 8
QK_NOPE = 64
QK_ROPE = 32
KV_LORA = 256
Q_LORA = 384
ROPE_THETA = 10000.0
SSM_HEADDIM = 64
SSM_HEADS = 8
SSM_GROUPS = 2
D_STATE = 128
CONV_K = 4
CHUNK = 256
CONV_DIM = D_SSM + 2 * SSM_GROUPS * D_STATE
D_FF = 4096
EPS = 1e-6

LANES = 128
QK_PAD = KV_LORA + LANES
VMEM_LIMIT = 56 * 1024 * 1024

_C_QLAT = 0
_C_KV = _C_QLAT + Q_LORA
_C_KPE = _C_KV + KV_LORA
_C_KPES = _C_KPE + LANES
_C_Z = _C_KPES + LANES
_C_XBC = _C_Z + D_SSM
_C_DT = _C_XBC + CONV_DIM
_N_IN_PAD = _C_DT + LANES
_Q_NOPE = 0
_Q_ROPE = MLA_HEADS * LANES
_Q_ROPES = _Q_ROPE + MLA_HEADS * QK_ROPE
_N_UQ_PAD = _Q_ROPES + MLA_HEADS * QK_ROPE

_BF = jnp.bfloat16
_F32 = jnp.float32


def _dot(a, b):
    return jnp.dot(a, b, preferred_element_type=_F32)


def _dot_nt(a, b):
    return lax.dot_general(a, b, (((1,), (1,)), ((), ())), preferred_element_type=_F32)


def _dot_tn(a, b):
    return lax.dot_general(a, b, (((0,), (0,)), ((), ())), preferred_element_type=_F32)


def _rms(x, g):
    return x * lax.rsqrt(jnp.mean(x * x, axis=-1, keepdims=True) + EPS) * g


def _silu(x):
    return x * (1.0 / (1.0 + jnp.exp(-x)))


def _const_spec(shape):
    nd = len(shape)
    return pl.BlockSpec(shape, lambda *_: (0,) * nd)


def _in_proj_kernel(x_ref, g_ref, w_ref, qn_ref, kvn_ref, wuq_ref, wuk_ref, cosq_ref, sinq_ref, cosk_ref, sink_ref,
                    q_ref, kv_ref, c_ref, kr_ref, z_ref, xbc_ref, dt_ref, *, tb, tl):
    tm = tb * tl

    def rows(tab_ref):
        t = tab_ref[...]
        if tb == 1:
            return t
        return jnp.broadcast_to(t[None], (tb, tl, t.shape[-1])).reshape(tm, t.shape[-1])

    h = _rms(x_ref[...], g_ref[...]).astype(_BF)
    proj = _dot(h, w_ref[...])
    z_ref[...] = proj[:, _C_Z:_C_XBC]
    xbc_ref[...] = proj[:, _C_XBC:_C_DT]
    dt_ref[...] = proj[:, _C_DT:_N_IN_PAD]

    c = _rms(proj[:, _C_KV:_C_KPE], kvn_ref[...])
    kr = proj[:, _C_KPE:_C_KPES] * rows(cosk_ref) + proj[:, _C_KPES:_C_Z] * rows(sink_ref)
    c_ref[...] = c
    kr_ref[...] = kr[:, :QK_ROPE]
    kv_ref[:, :KV_LORA] = c.astype(_BF)
    kv_ref[:, KV_LORA:] = kr.astype(_BF)

    qn = _rms(proj[:, _C_QLAT:_C_KV], qn_ref[...]).astype(_BF)
    qq = _dot(qn, wuq_ref[...])
    q_pe = qq[:, _Q_ROPE:_Q_ROPES] * rows(cosq_ref) + qq[:, _Q_ROPES:_N_UQ_PAD] * rows(sinq_ref)
    lane = lax.broadcasted_iota(jnp.int32, (tm, LANES), 1)
    heads_per_group = LANES // QK_ROPE
    for hd in range(MLA_HEADS):
        q_abs = _dot(qq[:, hd * LANES:(hd + 1) * LANES].astype(_BF), wuk_ref[hd])
        grp = q_pe[:, (hd // heads_per_group) * LANES:(hd // heads_per_group + 1) * LANES]
        shift = (LANES - QK_ROPE * (hd % heads_per_group)) % LANES
        if shift:
            grp = pltpu.roll(grp, shift, 1)
        pe = jnp.where(lane < QK_ROPE, grp, 0.0)
        q_ref[:, hd, :, :KV_LORA] = q_abs.astype(_BF).reshape(tb, tl, KV_LORA)
        q_ref[:, hd, :, KV_LORA:] = pe.astype(_BF).reshape(tb, tl, LANES)


def _in_proj(x2d, b, l, past_len, p, tb, tl):
    t = b * l
    tm = tb * tl
    n_l = l // tl
    grid = (t // tm,)
    pos = (past_len + jnp.arange(l, dtype=jnp.int32)).astype(_F32)
    inv = ROPE_THETA ** (-(jnp.arange(0, QK_ROPE, 2, dtype=_F32) / QK_ROPE))
    ang = pos[:, None] * inv[None, :]
    cos, sin = jnp.cos(ang), jnp.sin(ang)
    cos32 = jnp.concatenate([cos, cos], axis=1)
    sin32 = jnp.concatenate([-sin, sin], axis=1)
    padk = jnp.zeros((l, LANES - QK_ROPE), _F32)
    cosk = jnp.concatenate([cos32, padk], axis=1)
    sink = jnp.concatenate([sin32, padk], axis=1)
    cosq = jnp.tile(cos32, (1, MLA_HEADS))
    sinq = jnp.tile(sin32, (1, MLA_HEADS))

    row_spec = lambda w: pl.BlockSpec((tm, w), lambda i: (i, 0))
    tab_spec = lambda w: pl.BlockSpec((tl, w), lambda i: (i % n_l, 0))
    out_shape = (
        jax.ShapeDtypeStruct((b, MLA_HEADS, l, QK_PAD), _BF),
        jax.ShapeDtypeStruct((t, QK_PAD), _BF),
        jax.ShapeDtypeStruct((t, KV_LORA), _F32),
        jax.ShapeDtypeStruct((t, QK_ROPE), _F32),
        jax.ShapeDtypeStruct((t, D_SSM), _F32),
        jax.ShapeDtypeStruct((t, CONV_DIM), _F32),
        jax.ShapeDtypeStruct((t, LANES), _F32),
    )
    out_specs = (
        pl.BlockSpec((tb, MLA_HEADS, tl, QK_PAD), lambda i: (i // n_l, 0, i % n_l, 0)),
        row_spec(QK_PAD), row_spec(KV_LORA), row_spec(QK_ROPE), row_spec(D_SSM), row_spec(CONV_DIM), row_spec(LANES),
    )
    in_specs = [
        row_spec(D_MODEL), _const_spec((1, D_MODEL)), _const_spec((D_MODEL, _N_IN_PAD)),
        _const_spec((1, Q_LORA)), _const_spec((1, KV_LORA)), _const_spec((Q_LORA, _N_UQ_PAD)),
        _const_spec((MLA_HEADS, LANES, KV_LORA)),
        tab_spec(MLA_HEADS * QK_ROPE), tab_spec(MLA_HEADS * QK_ROPE), tab_spec(LANES), tab_spec(LANES),
    ]
    return pl.pallas_call(
        functools.partial(_in_proj_kernel, tb=tb, tl=tl),
        grid=grid, in_specs=in_specs, out_specs=out_specs, out_shape=out_shape,
        compiler_params=pltpu.CompilerParams(dimension_semantics=("arbitrary",), vmem_limit_bytes=VMEM_LIMIT),
        name="in_proj",
    )(x2d, p["g_attn"], p["w_in"], p["q_norm"], p["kv_norm"], p["w_uq"], p["w_uk"], cosq, sinq, cosk, sink)


def _softmax_step(s, kv_bf, m_ref, l_ref, acc_ref):
    m_prev = m_ref[...]
    m_new = jnp.maximum(m_prev, jnp.max(s, axis=1, keepdims=True))
    alpha = jnp.exp(m_prev - m_new)
    pr = jnp.exp(s - m_new)
    l_ref[...] = alpha * l_ref[...] + jnp.sum(pr, axis=1, keepdims=True)
    acc_ref[...] = alpha * acc_ref[...] + _dot(pr.astype(_BF), kv_bf[:, :KV_LORA])
    m_ref[...] = m_new


def _attn_prompt_kernel(q_ref, kv_ref, wuv_ref, o_ref, m_ref, l_ref, acc_ref, *, tq, tk, scale):
    i = pl.program_id(1)
    j = pl.program_id(2)
    m_rows = MLA_HEADS * tq

    @pl.when(j == 0)
    def _():
        m_ref[...] = jnp.full(m_ref.shape, -jnp.inf, _F32)
        l_ref[...] = jnp.zeros(l_ref.shape, _F32)
        acc_ref[...] = jnp.zeros(acc_ref.shape, _F32)

    def scores():
        q = q_ref[0].reshape(m_rows, QK_PAD)
        return _dot_nt(q, kv_ref[...]) * scale

    @pl.when(j < i)
    def _():
        _softmax_step(scores(), kv_ref[...], m_ref, l_ref, acc_ref)

    @pl.when(j == i)
    def _():
        s = scores()
        qpos = lax.broadcasted_iota(jnp.int32, (MLA_HEADS, tq, tk), 1).reshape(m_rows, tk)
        kpos = lax.broadcasted_iota(jnp.int32, (m_rows, tk), 1)
        s = jnp.where(kpos <= qpos, s, -jnp.inf)
        _softmax_step(s, kv_ref[...], m_ref, l_ref, acc_ref)
        o = (acc_ref[...] / l_ref[...]).astype(_BF)
        for hd in range(MLA_HEADS):
            o_ref[:, hd * V_HEAD:(hd + 1) * V_HEAD] = _dot(o[hd * tq:(hd + 1) * tq], wuv_ref[hd]).astype(o_ref.dtype)


def _attn_prompt(q, kv, wuv, b, l, tq):
    tk = tq
    nq = l // tq
    scale = (QK_NOPE + QK_ROPE) ** -0.5
    m_rows = MLA_HEADS * tq
    return pl.pallas_call(
        functools.partial(_attn_prompt_kernel, tq=tq, tk=tk, scale=scale),
        grid=(b, nq, nq),
        in_specs=[
            pl.BlockSpec((1, MLA_HEADS, tq, QK_PAD), lambda bi, i, j: (bi, 0, i, 0)),
            pl.BlockSpec((tk, QK_PAD), lambda bi, i, j: (bi * nq + jnp.minimum(j, i), 0)),
            _const_spec((MLA_HEADS, KV_LORA, V_HEAD)),
        ],
        out_specs=pl.BlockSpec((tq, D_ATTN), lambda bi, i, j: (bi * nq + i, 0)),
        out_shape=jax.ShapeDtypeStruct((b * l, D_ATTN), _BF),
        scratch_shapes=[pltpu.VMEM((m_rows, 1), _F32), pltpu.VMEM((m_rows, 1), _F32),
                        pltpu.VMEM((m_rows, KV_LORA), _F32)],
        compiler_params=pltpu.CompilerParams(dimension_semantics=("arbitrary", "arbitrary", "arbitrary"),
                                             vmem_limit_bytes=VMEM_LIMIT),
        name="attn_prompt",
    )(q, kv, wuv)


def _attn_sample_kernel(pt_ref, q_ref, kvn_ref, *rest, pages, page, l_new, scale):
    del pt_ref
    c_refs = rest[:pages]
    kr_refs = rest[pages:2 * pages]
    o_ref, kc_ref, kr_ref, m_ref, l_ref, acc_ref = rest[2 * pages:]
    j = pl.program_id(1)

    @pl.when(j == 0)
    def _():
        m_ref[...] = jnp.full(m_ref.shape, -jnp.inf, _F32)
        l_ref[...] = jnp.zeros(l_ref.shape, _F32)
        acc_ref[...] = jnp.zeros(acc_ref.shape, _F32)

    for pg in range(pages):
        kc_ref[pg * page:(pg + 1) * page, :] = c_refs[pg][...].astype(_BF)
        kr_ref[pg * page:(pg + 1) * page, :] = kr_refs[pg][...].astype(_BF)
    q = q_ref[0]
    s = (_dot_nt(q[:, :KV_LORA], kc_ref[...]) + _dot_nt(q[:, KV_LORA:KV_LORA + QK_ROPE], kr_ref[...])) * scale
    _softmax_step(s, kc_ref[...], m_ref, l_ref, acc_ref)

    @pl.when(j == pl.num_programs(1) - 1)
    def _():
        kvn = kvn_ref[0]
        m_rows = MLA_HEADS * l_new
        sn = _dot_nt(q, kvn) * scale
        qpos = lax.broadcasted_iota(jnp.int32, (MLA_HEADS, l_new, l_new), 1).reshape(m_rows, l_new)
        kpos = lax.broadcasted_iota(jnp.int32, (m_rows, l_new), 1)
        sn = jnp.where(kpos <= qpos, sn, -jnp.inf)
        _softmax_step(sn, kvn, m_ref, l_ref, acc_ref)
        o_ref[0] = (acc_ref[...] / l_ref[...]).astype(o_ref.dtype)


def _attn_sample(q, kvn, cache_c, cache_kr, page_table, pages):
    b, m_rows, _ = q.shape
    l_new = m_rows // MLA_HEADS
    n_pages = page_table.shape[1]
    page = cache_c.shape[1]
    scale = (QK_NOPE + QK_ROPE) ** -0.5
    nj = n_pages // pages

    def c_spec(pg):
        return pl.BlockSpec((None, page, KV_LORA), lambda bi, j, pt: (pt[bi, j * pages + pg], 0, 0))

    def kr_spec(pg):
        return pl.BlockSpec((None, page, QK_ROPE), lambda bi, j, pt: (pt[bi, j * pages + pg], 0, 0))

    grid_spec = pltpu.PrefetchScalarGridSpec(
        num_scalar_prefetch=1,
        grid=(b, nj),
        in_specs=[pl.BlockSpec((1, m_rows, QK_PAD), lambda bi, j, pt: (bi, 0, 0)),
                  pl.BlockSpec((1, l_new, QK_PAD), lambda bi, j, pt: (bi, 0, 0))]
        + [c_spec(pg) for pg in range(pages)] + [kr_spec(pg) for pg in range(pages)],
        out_specs=pl.BlockSpec((1, m_rows, KV_LORA), lambda bi, j, pt: (bi, 0, 0)),
        scratch_shapes=[pltpu.VMEM((pages * page, KV_LORA), _BF), pltpu.VMEM((pages * page, QK_ROPE), _BF),
                        pltpu.VMEM((m_rows, 1), _F32), pltpu.VMEM((m_rows, 1), _F32),
                        pltpu.VMEM((m_rows, KV_LORA), _F32)],
    )
    return pl.pallas_call(
        functools.partial(_attn_sample_kernel, pages=pages, page=page, l_new=l_new, scale=scale),
        grid_spec=grid_spec,
        out_shape=jax.ShapeDtypeStruct((b, m_rows, KV_LORA), _BF),
        compiler_params=pltpu.CompilerParams(dimension_semantics=("arbitrary", "arbitrary"),
                                             vmem_limit_bytes=VMEM_LIMIT),
        name="attn_sample",
    )(page_table, q, kvn, *([cache_c] * pages), *([cache_kr] * pages))


def _uv_proj_kernel(o_ref, wuv_ref, out_ref, *, tb, l_new):
    for hd in range(MLA_HEADS):
        o = o_ref[:, hd].reshape(tb * l_new, KV_LORA)
        out_ref[:, hd * V_HEAD:(hd + 1) * V_HEAD] = _dot(o, wuv_ref[hd]).astype(out_ref.dtype)


def _uv_proj(o_lat, wuv, tb):
    b, _, l_new, _ = o_lat.shape
    return pl.pallas_call(
        functools.partial(_uv_proj_kernel, tb=tb, l_new=l_new),
        grid=(b // tb,),
        in_specs=[pl.BlockSpec((tb, MLA_HEADS, l_new, KV_LORA), lambda i: (i, 0, 0, 0)),
                  _const_spec((MLA_HEADS, KV_LORA, V_HEAD))],
        out_specs=pl.BlockSpec((tb * l_new, D_ATTN), lambda i: (i, 0)),
        out_shape=jax.ShapeDtypeStruct((b * l_new, D_ATTN), _BF),
        compiler_params=pltpu.CompilerParams(dimension_semantics=("arbitrary",), vmem_limit_bytes=VMEM_LIMIT),
        name="uv_proj",
    )(o_lat, wuv)


_XP_OFF = 8


def _ssd_kernel(xbc_ref, z_ref, dtr_ref, cbuf_ref, h0_ref, cw_ref, cb_ref, dtb_ref, alog_ref, dskip_ref, nrm_ref,
                sel_ref, o_ref, h_ref, cnew_ref, xp_ref, y_ref, *, g_items, q):
    c_idx = pl.program_id(1)
    last = pl.num_programs(1) - 1
    heads_per_group = SSM_HEADS // SSM_GROUPS
    row = lax.broadcasted_iota(jnp.int32, (q, q), 0)
    col = lax.broadcasted_iota(jnp.int32, (q, q), 1)
    causal = row >= col
    tril = jnp.where(causal, 1.0, 0.0).astype(_F32)
    a_neg = -jnp.exp(alog_ref[...])

    for g in range(g_items):
        @pl.when(c_idx == 0)
        def _():
            xp_ref[g, _XP_OFF - (CONV_K - 1):_XP_OFF, :] = cbuf_ref[g]
            h_ref[g] = h0_ref[g]

        xp_ref[g, _XP_OFF:_XP_OFF + q, :] = xbc_ref[g * q:(g + 1) * q, :]
        conv = cb_ref[...]
        for k in range(CONV_K):
            conv = conv + xp_ref[g, _XP_OFF - (CONV_K - 1) + k:_XP_OFF - (CONV_K - 1) + k + q, :] * cw_ref[k:k + 1, :]
        conv = _silu(conv)
        tail = xp_ref[g, _XP_OFF + q - (CONV_K - 1):_XP_OFF + q, :]
        xp_ref[g, _XP_OFF - (CONV_K - 1):_XP_OFF, :] = tail

        @pl.when(c_idx == last)
        def _():
            cnew_ref[g] = tail

        xs = conv[:, :D_SSM]
        bmat = conv[:, D_SSM:D_SSM + SSM_GROUPS * D_STATE].astype(_BF)
        cmat = conv[:, D_SSM + SSM_GROUPS * D_STATE:].astype(_BF)
        dtv = dtr_ref[g * q:(g + 1) * q, :] + dtb_ref[...]
        dt = jnp.maximum(dtv, 0.0) + jnp.log1p(jnp.exp(-jnp.abs(dtv)))
        acs = jnp.dot(tril, dt * a_neg, preferred_element_type=_F32, precision=lax.Precision.HIGHEST)
        acs_t = lax.dot_general(sel_ref[...], acs, (((1,), (1,)), ((), ())), preferred_element_type=_F32,
                                precision=lax.Precision.HIGHEST)
        for grp in range(SSM_GROUPS):
            b_g = bmat[:, grp * D_STATE:(grp + 1) * D_STATE]
            c_g = cmat[:, grp * D_STATE:(grp + 1) * D_STATE]
            cb = _dot_nt(c_g, b_g)
            for hd in range(grp * heads_per_group, (grp + 1) * heads_per_group):
                acs_h = acs[:, hd:hd + 1]
                acs_last = acs[q - 1:q, hd:hd + 1]
                lmat = jnp.exp(jnp.where(causal, acs_h - acs_t[hd:hd + 1, :], -jnp.inf))
                x_h = xs[:, hd * SSM_HEADDIM:(hd + 1) * SSM_HEADDIM]
                x_dt = x_h * dt[:, hd:hd + 1]
                h_prev = h_ref[g, hd]
                y = _dot((cb * lmat).astype(_BF), x_dt.astype(_BF))
                y = y + _dot_nt(c_g, h_prev.astype(_BF)) * jnp.exp(acs_h)
                xw = (x_dt * jnp.exp(acs_last - acs_h)).astype(_BF)
                h_ref[g, hd] = jnp.exp(acs_last) * h_prev + _dot_tn(xw, b_g)
                y_ref[:, hd * SSM_HEADDIM:(hd + 1) * SSM_HEADDIM] = y
        yv = y_ref[...] + dskip_ref[...] * xs
        gated = yv * _silu(z_ref[g * q:(g + 1) * q, :])
        gw = D_SSM // SSM_GROUPS
        parts = []
        for grp in range(SSM_GROUPS):
            gg = gated[:, grp * gw:(grp + 1) * gw]
            parts.append(gg * lax.rsqrt(jnp.mean(gg * gg, axis=-1, keepdims=True) + EPS))
        o_ref[g * q:(g + 1) * q, :] = (jnp.concatenate(parts, axis=1) * nrm_ref[...]).astype(o_ref.dtype)


def _ssd(xbc, z, dtr, conv_buf, h0, p, b, l, q, g_items):
    nc = l // q
    rows = g_items * q
    row_spec = lambda w: pl.BlockSpec((rows, w), lambda bi, c: (bi * nc + c, 0))
    in_specs = [
        row_spec(CONV_DIM), row_spec(D_SSM), row_spec(LANES),
        pl.BlockSpec((g_items, CONV_K - 1, CONV_DIM), lambda bi, c: (bi, 0, 0)),
        pl.BlockSpec((g_items, SSM_HEADS, SSM_HEADDIM, D_STATE), lambda bi, c: (bi, 0, 0, 0)),
        _const_spec((CONV_K, CONV_DIM)), _const_spec((1, CONV_DIM)), _const_spec((1, LANES)), _const_spec((1, LANES)),
        _const_spec((1, D_SSM)), _const_spec((1, D_SSM)), _const_spec((SSM_HEADS, LANES)),
    ]
    out_specs = (
        row_spec(D_SSM),
        pl.BlockSpec((g_items, SSM_HEADS, SSM_HEADDIM, D_STATE), lambda bi, c: (bi, 0, 0, 0)),
        pl.BlockSpec((g_items, CONV_K - 1, CONV_DIM), lambda bi, c: (bi, 0, 0)),
    )
    out_shape = (
        jax.ShapeDtypeStruct((b * l, D_SSM), _BF),
        jax.ShapeDtypeStruct((b, SSM_HEADS, SSM_HEADDIM, D_STATE), _F32),
        jax.ShapeDtypeStruct((b, CONV_K - 1, CONV_DIM), _F32),
    )
    return pl.pallas_call(
        functools.partial(_ssd_kernel, g_items=g_items, q=q),
        grid=(b // g_items, nc), in_specs=in_specs, out_specs=out_specs, out_shape=out_shape,
        scratch_shapes=[pltpu.VMEM((g_items, _XP_OFF + q, CONV_DIM), _F32), pltpu.VMEM((q, D_SSM), _F32)],
        compiler_params=pltpu.CompilerParams(dimension_semantics=("arbitrary", "arbitrary"),
                                             vmem_limit_bytes=VMEM_LIMIT),
        name="ssd",
    )(xbc, z, dtr, conv_buf, h0, p["conv_w"], p["conv_b"], p["dt_bias"], p["a_log"], p["d_skip"], p["ssm_norm"],
      p["sel"])


def _out_mlp_kernel(x_ref, oa_ref, os_ref, wo_ref, gm_ref, wu_ref, wd_ref, gf_ref, y_ref, *, f_chunk):
    mix = jnp.concatenate([oa_ref[...], os_ref[...]], axis=1)
    x1 = x_ref[...] + _dot(mix, wo_ref[...])
    h2 = _rms(x1, gm_ref[...]).astype(_BF)
    acc = x1
    for f0 in range(0, D_FF, f_chunk):
        u = jnp.maximum(_dot(h2, wu_ref[:, f0:f0 + f_chunk]), 0.0)
        acc = acc + _dot((u * u).astype(_BF), wd_ref[f0:f0 + f_chunk, :])
    y_ref[...] = _rms(acc, gf_ref[...])


def _out_mlp(x2d, oa, osm, p, tm):
    t = x2d.shape[0]
    row_spec = lambda w: pl.BlockSpec((tm, w), lambda i: (i, 0))
    once = lambda shape: pl.BlockSpec(shape, lambda i: (0,) * len(shape), pipeline_mode=pl.Buffered(1))
    return pl.pallas_call(
        functools.partial(_out_mlp_kernel, f_chunk=1024),
        grid=(t // tm,),
        in_specs=[row_spec(D_MODEL), row_spec(D_ATTN), row_spec(D_SSM), once((D_MODEL, D_MODEL)),
                  once((1, D_MODEL)), once((D_MODEL, D_FF)), once((D_FF, D_MODEL)), once((1, D_MODEL))],
        out_specs=row_spec(D_MODEL),
        out_shape=jax.ShapeDtypeStruct((t, D_MODEL), _F32),
        compiler_params=pltpu.CompilerParams(dimension_semantics=("arbitrary",), vmem_limit_bytes=VMEM_LIMIT),
        name="out_mlp",
    )(x2d, oa, osm, p["w_out"], p["g_mlp"], p["w_up"], p["w_down"], p["g_final"])


def _prep_params(norm_attn, w_in, q_norm, kv_norm, w_uq, w_uk, w_uv, conv_w, conv_b, dt_bias, a_log, d_skip,
                 ssm_norm, w_out, norm_mlp, w_up, w_down, norm_final):
    half = QK_ROPE // 2

    def swap(w):
        return jnp.concatenate([w[..., half:], w[..., :half]], axis=-1)

    def pad_cols(w, n):
        return jnp.pad(w, ((0, 0), (0, n - w.shape[1])))

    o1 = Q_LORA
    o2 = o1 + KV_LORA
    o3 = o2 + QK_ROPE
    o4 = o3 + D_SSM
    o5 = o4 + CONV_DIM
    w_kpe = w_in[:, o2:o3]
    w_in_p = jnp.concatenate([
        w_in[:, :o2], pad_cols(w_kpe, LANES), pad_cols(swap(w_kpe), LANES), w_in[:, o3:o5],
        pad_cols(w_in[:, o5:], LANES)], axis=1).astype(_BF)
    uq = w_uq.reshape(Q_LORA, MLA_HEADS, QK_NOPE + QK_ROPE)
    uq_nope = jnp.pad(uq[:, :, :QK_NOPE], ((0, 0), (0, 0), (0, LANES - QK_NOPE))).reshape(Q_LORA, MLA_HEADS * LANES)
    uq_rope = uq[:, :, QK_NOPE:]
    w_uq_p = jnp.concatenate([uq_nope, uq_rope.reshape(Q_LORA, -1), swap(uq_rope).reshape(Q_LORA, -1)],
                             axis=1).astype(_BF)
    w_uk_p = jnp.pad(w_uk, ((0, 0), (0, LANES - QK_NOPE), (0, 0))).astype(_BF)
    pad_heads = lambda v: jnp.pad(v.reshape(1, SSM_HEADS), ((0, 0), (0, LANES - SSM_HEADS)))
    return dict(
        g_attn=norm_attn.reshape(1, D_MODEL), w_in=w_in_p, q_norm=q_norm.reshape(1, Q_LORA),
        kv_norm=kv_norm.reshape(1, KV_LORA), w_uq=w_uq_p, w_uk=w_uk_p, w_uv=w_uv.astype(_BF),
        conv_w=conv_w, conv_b=conv_b.reshape(1, CONV_DIM), dt_bias=pad_heads(dt_bias), a_log=pad_heads(a_log),
        d_skip=jnp.repeat(d_skip, SSM_HEADDIM).reshape(1, D_SSM), ssm_norm=ssm_norm.reshape(1, D_SSM),
        sel=jnp.eye(SSM_HEADS, LANES, dtype=_F32),
        w_out=w_out.astype(_BF), g_mlp=norm_mlp.reshape(1, D_MODEL), w_up=w_up.astype(_BF),
        w_down=w_down.astype(_BF), g_final=norm_final.reshape(1, D_MODEL),
    )


def _largest_divisor(n, cap):
    d = min(n, cap)
    while n % d:
        d -= 1
    return d


def kernel(x_prompt, x_sample, cache_kv_latent, cache_k_rope, state_ssm, state_conv, page_table, norm_attn, w_in, q_norm, kv_norm, w_uq, w_uk, w_uv, conv_w, conv_b, dt_bias, a_log, d_skip, ssm_norm, w_out, norm_mlp, w_up, w_down, norm_final):
    assert norm_attn.shape[0] == 1, "single-layer model"
    p = _prep_params(norm_attn[0], w_in[0], q_norm[0], kv_norm[0], w_uq[0], w_uk[0], w_uv[0], conv_w[0], conv_b[0],
                     dt_bias[0], a_log[0], d_skip[0], ssm_norm[0], w_out[0], norm_mlp[0], w_up[0], w_down[0],
                     norm_final)
    bp, lp, _ = x_prompt.shape
    bs, ls, _ = x_sample.shape
    n_pages = page_table.shape[1]
    page = cache_kv_latent.shape[2]
    past_len = n_pages * page

    xp2 = x_prompt.reshape(bp * lp, D_MODEL)
    tl = _largest_divisor(lp, 512)
    q, kv, c_new, kr_new, z, xbc, dtr = _in_proj(xp2, bp, lp, 0, p, 1, tl)
    o_attn = _attn_prompt(q, kv, p["w_uv"], bp, lp, _largest_divisor(lp, 256))
    chunk = _largest_divisor(lp, CHUNK)
    o_ssm, h_p, conv_p = _ssd(xbc, z, dtr, jnp.zeros((bp, CONV_K - 1, CONV_DIM), _F32),
                              jnp.zeros((bp, SSM_HEADS, SSM_HEADDIM, D_STATE), _F32), p, bp, lp, chunk, 1)
    y_p = _out_mlp(xp2, o_attn, o_ssm, p, _largest_divisor(bp * lp, 512))
    outs_p = (y_p.reshape(bp, lp, D_MODEL), c_new.reshape(1, bp, lp, KV_LORA), kr_new.reshape(1, bp, lp, QK_ROPE),
              h_p[None], conv_p[None])

    xs2 = x_sample.reshape(bs * ls, D_MODEL)
    tb = _largest_divisor(bs, 512 // ls)
    q, kv, c_new, kr_new, z, xbc, dtr = _in_proj(xs2, bs, ls, past_len, p, tb, ls)
    o_lat = _attn_sample(q.reshape(bs, MLA_HEADS * ls, QK_PAD), kv.reshape(bs, ls, QK_PAD), cache_kv_latent[0],
                         cache_k_rope[0], page_table, _largest_divisor(n_pages, 8))
    o_attn = _uv_proj(o_lat.reshape(bs, MLA_HEADS, ls, KV_LORA), p["w_uv"], tb)
    g_items = _largest_divisor(bs, 8)
    o_ssm, h_s, conv_s = _ssd(xbc, z, dtr, state_conv[0], state_ssm[0], p, bs, ls, ls, g_items)
    y_s = _out_mlp(xs2, o_attn, o_ssm, p, _largest_divisor(bs * ls, 512))
    outs_s = (y_s.reshape(bs, ls, D_MODEL), c_new.reshape(1, bs, ls, KV_LORA), kr_new.reshape(1, bs, ls, QK_ROPE),
              h_s[None], conv_s[None])

    return (outs_p[0], outs_s[0], outs_p[1], outs_p[2], outs_p[3], outs_p[4],
            outs_s[1], outs_s[2], outs_s[3], outs_s[4])
```

```python
import functools

import jax
import jax.numpy as jnp
from jax import lax
from jax.experimental import pallas as pl
from jax.experimental.pallas import tpu as pltpu

D_MODEL = 1024
D_ATTN = 512
D_SSM = 512
V_HEAD = 64
MLA_HEADS = 8
QK_NOPE = 64
QK_ROPE = 32
KV_LORA = 256
Q_LORA = 384
ROPE_THETA = 10000.0
SSM_HEADDIM = 64
SSM_HEADS = 8
SSM_GROUPS = 2
D_STATE = 128
CONV_K = 4
CHUNK = 256
CONV_DIM = D_SSM + 2 * SSM_GROUPS * D_STATE
D_FF = 4096
EPS = 1e-6

LANES = 128
QK_PAD = KV_LORA + LANES
VMEM_LIMIT = 56 * 1024 * 1024
SCALE_LOG2 = (QK_NOPE + QK_ROPE) ** -0.5 * 1.4426950408889634

_C_QLAT = 0
_C_KV = _C_QLAT + Q_LORA
_C_KPE = _C_KV + KV_LORA
_C_KPES = _C_KPE + LANES
_C_Z = _C_KPES + LANES
_C_XBC = _C_Z + D_SSM
_C_DT = _C_XBC + CONV_DIM
_N_IN_PAD = _C_DT + LANES
_Q_NOPE = 0
_Q_ROPE = MLA_HEADS * LANES
_Q_ROPES = _Q_ROPE + MLA_HEADS * QK_ROPE
_N_UQ_PAD = _Q_ROPES + MLA_HEADS * QK_ROPE

_BF = jnp.bfloat16
_F32 = jnp.float32


def _dot(a, b):
    return jnp.dot(a, b, preferred_element_type=_F32)


def _dot_nt(a, b):
    return lax.dot_general(a, b, (((1,), (1,)), ((), ())), preferred_element_type=_F32)


def _dot_tn(a, b):
    return lax.dot_general(a, b, (((0,), (0,)), ((), ())), preferred_element_type=_F32)


def _rms(x, g):
    return x * lax.rsqrt(jnp.mean(x * x, axis=-1, keepdims=True) + EPS) * g


def _silu(x):
    return x * (1.0 / (1.0 + jnp.exp(-x)))


def _const_spec(shape):
    nd = len(shape)
    return pl.BlockSpec(shape, lambda *_: (0,) * nd)


def _rope_tables(l, past_len):
    pos = (past_len + jnp.arange(l, dtype=jnp.int32)).astype(_F32)
    inv = ROPE_THETA ** (-(jnp.arange(0, QK_ROPE, 2, dtype=_F32) / QK_ROPE))
    ang = pos[:, None] * inv[None, :]
    cos, sin = jnp.cos(ang), jnp.sin(ang)
    return jnp.concatenate([cos, cos], axis=1), jnp.concatenate([-sin, sin], axis=1)


def _in_proj_kernel(x_ref, g_ref, w_ref, qn_ref, kvn_ref, wuq_ref, wuk_ref, cosq_ref, sinq_ref, cosk_ref, sink_ref,
                    q_ref, kv_ref, c_ref, kr_ref, z_ref, xbc_ref, dt_ref, *, tb, tl):
    tm = tb * tl

    def rows(tab_ref):
        t = tab_ref[...]
        if tb == 1:
            return t
        return jnp.broadcast_to(t[None], (tb, tl, t.shape[-1])).reshape(tm, t.shape[-1])

    h = _rms(x_ref[...], g_ref[...]).astype(_BF)
    proj = _dot(h, w_ref[...])
    z_ref[...] = proj[:, _C_Z:_C_XBC]
    xbc_ref[...] = proj[:, _C_XBC:_C_DT]
    dt_ref[...] = proj[:, _C_DT:_N_IN_PAD]

    c = _rms(proj[:, _C_KV:_C_KPE], kvn_ref[...])
    kr = proj[:, _C_KPE:_C_KPES] * rows(cosk_ref) + proj[:, _C_KPES:_C_Z] * rows(sink_ref)
    c_ref[...] = c
    kr_ref[...] = kr[:, :QK_ROPE]
    kv_ref[:, :KV_LORA] = c.astype(_BF)
    kv_ref[:, KV_LORA:] = kr.astype(_BF)

    qn = _rms(proj[:, _C_QLAT:_C_KV], qn_ref[...]).astype(_BF)
    qq = _dot(qn, wuq_ref[...])
    q_pe = qq[:, _Q_ROPE:_Q_ROPES] * rows(cosq_ref) + qq[:, _Q_ROPES:_N_UQ_PAD] * rows(sinq_ref)
    lane = lax.broadcasted_iota(jnp.int32, (tm, LANES), 1)
    heads_per_group = LANES // QK_ROPE
    for hd in range(MLA_HEADS):
        q_abs = _dot(qq[:, hd * LANES:(hd + 1) * LANES].astype(_BF), wuk_ref[hd]) * SCALE_LOG2
        grp = q_pe[:, (hd // heads_per_group) * LANES:(hd // heads_per_group + 1) * LANES]
        shift = (LANES - QK_ROPE * (hd % heads_per_group)) % LANES
        if shift:
            grp = pltpu.roll(grp, shift, 1)
        pe = jnp.where(lane < QK_ROPE, grp * SCALE_LOG2, 0.0)
        q_ref[:, hd, :, :KV_LORA] = q_abs.astype(_BF).reshape(tb, tl, KV_LORA)
        q_ref[:, hd, :, KV_LORA:] = pe.astype(_BF).reshape(tb, tl, LANES)


def _in_proj(x2d, b, l, past_len, p, tb, tl):
    t = b * l
    tm = tb * tl
    n_l = l // tl
    cos32, sin32 = _rope_tables(l, past_len)
    padk = jnp.zeros((l, LANES - QK_ROPE), _F32)
    cosk = jnp.concatenate([cos32, padk], axis=1)
    sink = jnp.concatenate([sin32, padk], axis=1)
    cosq = jnp.tile(cos32, (1, MLA_HEADS))
    sinq = jnp.tile(sin32, (1, MLA_HEADS))

    row_spec = lambda w: pl.BlockSpec((tm, w), lambda i: (i, 0))
    tab_spec = lambda w: pl.BlockSpec((tl, w), lambda i: (i % n_l, 0))
    out_shape = (
        jax.ShapeDtypeStruct((b, MLA_HEADS, l, QK_PAD), _BF),
        jax.ShapeDtypeStruct((t, QK_PAD), _BF),
        jax.ShapeDtypeStruct((t, KV_LORA), _F32),
        jax.ShapeDtypeStruct((t, QK_ROPE), _F32),
        jax.ShapeDtypeStruct((t, D_SSM), _F32),
        jax.ShapeDtypeStruct((t, CONV_DIM), _F32),
        jax.ShapeDtypeStruct((t, LANES), _F32),
    )
    out_specs = (
        pl.BlockSpec((tb, MLA_HEADS, tl, QK_PAD), lambda i: (i // n_l, 0, i % n_l, 0)),
        row_spec(QK_PAD), row_spec(KV_LORA), row_spec(QK_ROPE), row_spec(D_SSM), row_spec(CONV_DIM), row_spec(LANES),
    )
    in_specs = [
        row_spec(D_MODEL), _const_spec((1, D_MODEL)), _const_spec((D_MODEL, _N_IN_PAD)),
        _const_spec((1, Q_LORA)), _const_spec((1, KV_LORA)), _const_spec((Q_LORA, _N_UQ_PAD)),
        _const_spec((MLA_HEADS, LANES, KV_LORA)),
        tab_spec(MLA_HEADS * QK_ROPE), tab_spec(MLA_HEADS * QK_ROPE), tab_spec(LANES), tab_spec(LANES),
    ]
    return pl.pallas_call(
        functools.partial(_in_proj_kernel, tb=tb, tl=tl),
        grid=(t // tm,), in_specs=in_specs, out_specs=out_specs, out_shape=out_shape,
        compiler_params=pltpu.CompilerParams(dimension_semantics=("arbitrary",), vmem_limit_bytes=VMEM_LIMIT),
        name="in_proj",
    )(x2d, p["g_attn"], p["w_in"], p["q_norm"], p["kv_norm"], p["w_uq"], p["w_uk"], cosq, sinq, cosk, sink)


def _in_proj_prompt_kernel(x_ref, g_ref, w_ref, qn_ref, kvn_ref, wuqt_ref, wk_ref, wuvt_ref, cost_ref, sint_ref,
                           cosk_ref, sink_ref, qt_ref, kh_ref, vt_ref, c_ref, kr_ref, z_ref, xbc_ref, dt_ref):
    h = _rms(x_ref[...], g_ref[...]).astype(_BF)
    proj = _dot(h, w_ref[...])
    z_ref[...] = proj[:, _C_Z:_C_XBC]
    xbc_ref[...] = proj[:, _C_XBC:_C_DT]
    dt_ref[...] = proj[:, _C_DT:_N_IN_PAD]

    c = _rms(proj[:, _C_KV:_C_KPE], kvn_ref[...])
    kr = proj[:, _C_KPE:_C_KPES] * cosk_ref[...] + proj[:, _C_KPES:_C_Z] * sink_ref[...]
    c_ref[...] = c
    kr_ref[...] = kr[:, QK_NOPE:QK_NOPE + QK_ROPE]
    c_bf = c.astype(_BF)
    kn = _dot(c_bf, wk_ref[...])
    for hd in range(MLA_HEADS):
        kh_ref[:, hd * LANES:(hd + 1) * LANES] = (kn[:, hd * LANES:(hd + 1) * LANES] + kr).astype(_BF)
    vt_ref[0] = _dot_nt(wuvt_ref[...], c_bf).astype(_BF)

    qn = _rms(proj[:, _C_QLAT:_C_KV], qn_ref[...]).astype(_BF)
    qqt = _dot_nt(wuqt_ref[...], qn)
    half = MLA_HEADS * LANES
    for hd in range(MLA_HEADS):
        qt = (qqt[hd * LANES:(hd + 1) * LANES] * cost_ref[...]
              + qqt[half + hd * LANES:half + (hd + 1) * LANES] * sint_ref[...])
        qt_ref[0, hd] = qt.astype(_BF)


def _in_proj_prompt(x2d, b, l, p, tl):
    t = b * l
    n_l = l // tl
    cos32, sin32 = _rope_tables(l, 0)
    lead = jnp.zeros((l, QK_NOPE), _F32)
    trail = jnp.zeros((l, LANES - QK_NOPE - QK_ROPE), _F32)
    cosk = jnp.concatenate([lead, cos32, trail], axis=1)
    sink = jnp.concatenate([lead, sin32, trail], axis=1)
    cost = (jnp.concatenate([lead + 1.0, cos32, trail], axis=1) * SCALE_LOG2).T
    sint = (sink * SCALE_LOG2).T

    row_spec = lambda w: pl.BlockSpec((tl, w), lambda i: (i, 0))
    out_shape = (
        jax.ShapeDtypeStruct((b, MLA_HEADS, LANES, l), _BF),
        jax.ShapeDtypeStruct((t, MLA_HEADS * LANES), _BF),
        jax.ShapeDtypeStruct((b, D_ATTN, l), _BF),
        jax.ShapeDtypeStruct((t, KV_LORA), _F32),
        jax.ShapeDtypeStruct((t, QK_ROPE), _F32),
        jax.ShapeDtypeStruct((t, D_SSM), _F32),
        jax.ShapeDtypeStruct((t, CONV_DIM), _F32),
        jax.ShapeDtypeStruct((t, LANES), _F32),
    )
    out_specs = (
        pl.BlockSpec((1, MLA_HEADS, LANES, tl), lambda i: (i // n_l, 0, 0, i % n_l)),
        row_spec(MLA_HEADS * LANES),
        pl.BlockSpec((1, D_ATTN, tl), lambda i: (i // n_l, 0, i % n_l)),
        row_spec(KV_LORA), row_spec(QK_ROPE), row_spec(D_SSM), row_spec(CONV_DIM), row_spec(LANES),
    )
    in_specs = [
        row_spec(D_MODEL), _const_spec((1, D_MODEL)), _const_spec((D_MODEL, _N_IN_PAD)),
        _const_spec((1, Q_LORA)), _const_spec((1, KV_LORA)), _const_spec((2 * MLA_HEADS * LANES, Q_LORA)),
        _const_spec((KV_LORA, MLA_HEADS * LANES)), _const_spec((D_ATTN, KV_LORA)),
        pl.BlockSpec((LANES, tl), lambda i: (0, i % n_l)), pl.BlockSpec((LANES, tl), lambda i: (0, i % n_l)),
        pl.BlockSpec((tl, LANES), lambda i: (i % n_l, 0)), pl.BlockSpec((tl, LANES), lambda i: (i % n_l, 0)),
    ]
    return pl.pallas_call(
        _in_proj_prompt_kernel,
        grid=(t // tl,), in_specs=in_specs, out_specs=out_specs, out_shape=out_shape,
        compiler_params=pltpu.CompilerParams(dimension_semantics=("arbitrary",), vmem_limit_bytes=VMEM_LIMIT),
        name="in_proj_prompt",
    )(x2d, p["g_attn"], p["w_in_prompt"], p["q_norm"], p["kv_norm"], p["w_uq_t"], p["w_k"], p["w_uv_t"],
      cost, sint, cosk, sink)


def _attn_prompt_kernel(it_ref, jt_ref, qt_ref, kh_ref, vt_ref, o_ref, m_ref, l_ref, acc_ref, *, tq):
    s_idx = pl.program_id(1)
    i = it_ref[s_idx]
    j = jt_ref[s_idx]

    @pl.when(j == 0)
    def _():
        m_ref[...] = jnp.full(m_ref.shape, -jnp.inf, _F32)
        l_ref[...] = jnp.zeros(l_ref.shape, _F32)
        acc_ref[...] = jnp.zeros(acc_ref.shape, _F32)

    def step(masked):
        if masked:
            krow = lax.broadcasted_iota(jnp.int32, (tq, tq), 0)
            qcol = lax.broadcasted_iota(jnp.int32, (tq, tq), 1)
            keep = krow <= qcol
        for hd in range(MLA_HEADS):
            st = _dot(kh_ref[:, hd * LANES:(hd + 1) * LANES], qt_ref[0, hd])
            if masked:
                st = jnp.where(keep, st, -jnp.inf)
            m_prev = m_ref[hd:hd + 1, :]
            m_new = jnp.maximum(m_prev, jnp.max(st, axis=0, keepdims=True))
            alpha = jnp.exp2(m_prev - m_new)
            pr = jnp.exp2(st - m_new)
            l_ref[hd:hd + 1, :] = alpha * l_ref[hd:hd + 1, :] + jnp.sum(pr, axis=0, keepdims=True)
            rows = slice(hd * V_HEAD, (hd + 1) * V_HEAD)
            acc_ref[rows, :] = alpha * acc_ref[rows, :] + _dot(vt_ref[0, rows, :], pr.astype(_BF))
            m_ref[hd:hd + 1, :] = m_new

    @pl.when(j < i)
    def _():
        step(False)

    @pl.when(j == i)
    def _():
        step(True)
        for hd in range(MLA_HEADS):
            rows = slice(hd * V_HEAD, (hd + 1) * V_HEAD)
            acc_ref[rows, :] = acc_ref[rows, :] / l_ref[hd:hd + 1, :]
        o_ref[...] = acc_ref[...].T.astype(o_ref.dtype)


def _attn_prompt(qt, kh, vt, b, l, tq):
    nq = l // tq
    pairs = [(i, j) for i in range(nq) for j in range(i + 1)]
    it = jnp.asarray([pr[0] for pr in pairs], jnp.int32)
    jt = jnp.asarray([pr[1] for pr in pairs], jnp.int32)
    grid_spec = pltpu.PrefetchScalarGridSpec(
        num_scalar_prefetch=2,
        grid=(b, len(pairs)),
        in_specs=[
            pl.BlockSpec((1, MLA_HEADS, LANES, tq), lambda bi, s, it, jt: (bi, 0, 0, it[s])),
            pl.BlockSpec((tq, MLA_HEADS * LANES), lambda bi, s, it, jt: (bi * nq + jt[s], 0)),
            pl.BlockSpec((1, D_ATTN, tq), lambda bi, s, it, jt: (bi, 0, jt[s])),
        ],
        out_specs=pl.BlockSpec((tq, D_ATTN), lambda bi, s, it, jt: (bi * nq + it[s], 0)),
        scratch_shapes=[pltpu.VMEM((MLA_HEADS, tq), _F32), pltpu.VMEM((MLA_HEADS, tq), _F32),
                        pltpu.VMEM((D_ATTN, tq), _F32)],
    )
    return pl.pallas_call(
        functools.partial(_attn_prompt_kernel, tq=tq),
        grid_spec=grid_spec,
        out_shape=jax.ShapeDtypeStruct((b * l, D_ATTN), _BF),
        compiler_params=pltpu.CompilerParams(dimension_semantics=("arbitrary", "arbitrary"),
                                             vmem_limit_bytes=VMEM_LIMIT),
        name="attn_prompt",
    )(it, jt, qt, kh, vt)


def _attn_sample_kernel(pt_ref, q_ref, kvn_ref, cache_c_ref, cache_krt_ref, o_ref, cbuf_ref, krbuf_ref, kc_ref,
                        sem_ref, *, n_pages, page, l_new):
    b = pl.program_id(0)
    nb = pl.num_programs(0)
    slot = b % 2

    def page_copies(item, slot_, pg):
        pid = pt_ref[item, pg]
        dst_rows = pl.ds(pl.multiple_of(pg * page, page), page)
        return (pltpu.make_async_copy(cache_c_ref.at[pid], cbuf_ref.at[slot_, dst_rows, :], sem_ref.at[slot_, 0]),
                pltpu.make_async_copy(cache_krt_ref.at[pid], krbuf_ref.at[slot_, :, dst_rows], sem_ref.at[slot_, 1]))

    def start_item(item, slot_):
        def body(pg, carry):
            for cp in page_copies(item, slot_, pg):
                cp.start()
            return carry
        lax.fori_loop(0, n_pages, body, 0)

    @pl.when(b == 0)
    def _():
        start_item(0, 0)

    @pl.when(b + 1 < nb)
    def _():
        start_item(b + 1, 1 - slot)

    def wait_body(pg, carry):
        for cp in page_copies(b, slot, pg):
            cp.wait()
        return carry
    lax.fori_loop(0, n_pages, wait_body, 0)

    m_rows = MLA_HEADS * l_new
    kc_ref[...] = cbuf_ref[slot].astype(_BF)
    q = q_ref[0]
    s = _dot_nt(q[:, :KV_LORA], kc_ref[...]) + _dot(q[:, KV_LORA:KV_LORA + QK_ROPE], krbuf_ref[slot].astype(_BF))
    kvn = kvn_ref[0]
    sn = _dot_nt(q, kvn)
    qpos = lax.broadcasted_iota(jnp.int32, (MLA_HEADS, l_new, l_new), 1).reshape(m_rows, l_new)
    kpos = lax.broadcasted_iota(jnp.int32, (m_rows, l_new), 1)
    sn = jnp.where(kpos <= qpos, sn, -jnp.inf)
    m = jnp.maximum(jnp.max(s, axis=1, keepdims=True), jnp.max(sn, axis=1, keepdims=True))
    pr = jnp.exp2(s - m)
    pn = jnp.exp2(sn - m)
    den = jnp.sum(pr, axis=1, keepdims=True) + jnp.sum(pn, axis=1, keepdims=True)
    o = _dot(pr.astype(_BF), kc_ref[...]) + _dot(pn.astype(_BF), kvn[:, :KV_LORA])
    o_ref[0] = (o / den).astype(o_ref.dtype)


def _attn_sample(q, kvn, cache_c, cache_krt, page_table):
    b, m_rows, _ = q.shape
    l_new = m_rows // MLA_HEADS
    n_pages = page_table.shape[1]
    page = cache_c.shape[1]
    keys = n_pages * page
    grid_spec = pltpu.PrefetchScalarGridSpec(
        num_scalar_prefetch=1,
        grid=(b,),
        in_specs=[pl.BlockSpec((1, m_rows, QK_PAD), lambda bi, pt: (bi, 0, 0)),
                  pl.BlockSpec((1, l_new, QK_PAD), lambda bi, pt: (bi, 0, 0)),
                  pl.BlockSpec(memory_space=pl.ANY), pl.BlockSpec(memory_space=pl.ANY)],
        out_specs=pl.BlockSpec((1, m_rows, KV_LORA), lambda bi, pt: (bi, 0, 0)),
        scratch_shapes=[pltpu.VMEM((2, keys, KV_LORA), _F32), pltpu.VMEM((2, QK_ROPE, keys), _F32),
                        pltpu.VMEM((keys, KV_LORA), _BF), pltpu.SemaphoreType.DMA((2, 2))],
    )
    return pl.pallas_call(
        functools.partial(_attn_sample_kernel, n_pages=n_pages, page=page, l_new=l_new),
        grid_spec=grid_spec,
        out_shape=jax.ShapeDtypeStruct((b, m_rows, KV_LORA), _BF),
        compiler_params=pltpu.CompilerParams(dimension_semantics=("arbitrary",), vmem_limit_bytes=VMEM_LIMIT),
        name="attn_sample",
    )(page_table, q, kvn, cache_c, cache_krt)


def _uv_proj_kernel(o_ref, wuv_ref, out_ref, *, tb, l_new):
    for hd in range(MLA_HEADS):
        o = o_ref[:, hd].reshape(tb * l_new, KV_LORA)
        out_ref[:, hd * V_HEAD:(hd + 1) * V_HEAD] = _dot(o, wuv_ref[hd]).astype(out_ref.dtype)


def _uv_proj(o_lat, wuv, tb):
    b, _, l_new, _ = o_lat.shape
    return pl.pallas_call(
        functools.partial(_uv_proj_kernel, tb=tb, l_new=l_new),
        grid=(b // tb,),
        in_specs=[pl.BlockSpec((tb, MLA_HEADS, l_new, KV_LORA), lambda i: (i, 0, 0, 0)),
                  _const_spec((MLA_HEADS, KV_LORA, V_HEAD))],
        out_specs=pl.BlockSpec((tb * l_new, D_ATTN), lambda i: (i, 0)),
        out_shape=jax.ShapeDtypeStruct((b * l_new, D_ATTN), _BF),
        compiler_params=pltpu.CompilerParams(dimension_semantics=("arbitrary",), vmem_limit_bytes=VMEM_LIMIT),
        name="uv_proj",
    )(o_lat, wuv)


_XP_OFF = 8


def _ssd_kernel(xbc_ref, z_ref, dtr_ref, cbuf_ref, h0_ref, cw_ref, cb_ref, dtb_ref, alog_ref, dskip_ref, nrm_ref,
                sel_ref, o_ref, h_ref, cnew_ref, xp_ref, y_ref, *, g_items, q):
    c_idx = pl.program_id(1)
    last = pl.num_programs(1) - 1
    heads_per_group = SSM_HEADS // SSM_GROUPS
    row = lax.broadcasted_iota(jnp.int32, (q, q), 0)
    col = lax.broadcasted_iota(jnp.int32, (q, q), 1)
    causal = row >= col
    tril = jnp.where(causal, 1.0, 0.0).astype(_F32)
    a_neg = -jnp.exp(alog_ref[...])

    for g in range(g_items):
        @pl.when(c_idx == 0)
        def _():
            xp_ref[g, _XP_OFF - (CONV_K - 1):_XP_OFF, :] = cbuf_ref[g]
            h_ref[g] = h0_ref[g]

        xp_ref[g, _XP_OFF:_XP_OFF + q, :] = xbc_ref[g * q:(g + 1) * q, :]
        conv = cb_ref[...]
        for k in range(CONV_K):
            conv = conv + xp_ref[g, _XP_OFF - (CONV_K - 1) + k:_XP_OFF - (CONV_K - 1) + k + q, :] * cw_ref[k:k + 1, :]
        conv = _silu(conv)
        tail = xp_ref[g, _XP_OFF + q - (CONV_K - 1):_XP_OFF + q, :]
        xp_ref[g, _XP_OFF - (CONV_K - 1):_XP_OFF, :] = tail

        @pl.when(c_idx == last)
        def _():
            cnew_ref[g] = tail

        xs = conv[:, :D_SSM]
        bmat = conv[:, D_SSM:D_SSM + SSM_GROUPS * D_STATE].astype(_BF)
        cmat = conv[:, D_SSM + SSM_GROUPS * D_STATE:].astype(_BF)
        dtv = dtr_ref[g * q:(g + 1) * q, :] + dtb_ref[...]
        dt = jnp.maximum(dtv, 0.0) + jnp.log1p(jnp.exp(-jnp.abs(dtv)))
        acs = jnp.dot(tril, dt * a_neg, preferred_element_type=_F32, precision=lax.Precision.HIGHEST)
        acs_t = lax.dot_general(sel_ref[...], acs, (((1,), (1,)), ((), ())), preferred_element_type=_F32,
                                precision=lax.Precision.HIGHEST)
        for grp in range(SSM_GROUPS):
            b_g = bmat[:, grp * D_STATE:(grp + 1) * D_STATE]
            c_g = cmat[:, grp * D_STATE:(grp + 1) * D_STATE]
            cb = _dot_nt(c_g, b_g)
            for hd in range(grp * heads_per_group, (grp + 1) * heads_per_group):
                acs_h = acs[:, hd:hd + 1]
                acs_last = acs[q - 1:q, hd:hd + 1]
                lmat = jnp.exp(jnp.where(causal, acs_h - acs_t[hd:hd + 1, :], -jnp.inf))
                x_h = xs[:, hd * SSM_HEADDIM:(hd + 1) * SSM_HEADDIM]
                x_dt = x_h * dt[:, hd:hd + 1]
                h_prev = h_ref[g, hd]
                y = _dot((cb * lmat).astype(_BF), x_dt.astype(_BF))
                y = y + _dot_nt(c_g, h_prev.astype(_BF)) * jnp.exp(acs_h)
                xw = (x_dt * jnp.exp(acs_last - acs_h)).astype(_BF)
                h_ref[g, hd] = jnp.exp(acs_last) * h_prev + _dot_tn(xw, b_g)
                y_ref[:, hd * SSM_HEADDIM:(hd + 1) * SSM_HEADDIM] = y
        yv = y_ref[...] + dskip_ref[...] * xs
        gated = yv * _silu(z_ref[g * q:(g + 1) * q, :])
        gw = D_SSM // SSM_GROUPS
        parts = []
        for grp in range(SSM_GROUPS):
            gg = gated[:, grp * gw:(grp + 1) * gw]
            parts.append(gg * lax.rsqrt(jnp.mean(gg * gg, axis=-1, keepdims=True) + EPS))
        o_ref[g * q:(g + 1) * q, :] = (jnp.concatenate(parts, axis=1) * nrm_ref[...]).astype(o_ref.dtype)


def _ssd(xbc, z, dtr, conv_buf, h0, p, b, l, q, g_items):
    nc = l // q
    rows = g_items * q
    row_spec = lambda w: pl.BlockSpec((rows, w), lambda bi, c: (bi * nc + c, 0))
    in_specs = [
        row_spec(CONV_DIM), row_spec(D_SSM), row_spec(LANES),
        pl.BlockSpec((g_items, CONV_K - 1, CONV_DIM), lambda bi, c: (bi, 0, 0)),
        pl.BlockSpec((g_items, SSM_HEADS, SSM_HEADDIM, D_STATE), lambda bi, c: (bi, 0, 0, 0)),
        _const_spec((CONV_K, CONV_DIM)), _const_spec((1, CONV_DIM)), _const_spec((1, LANES)), _const_spec((1, LANES)),
        _const_spec((1, D_SSM)), _const_spec((1, D_SSM)), _const_spec((SSM_HEADS, LANES)),
    ]
    out_specs = (
        row_spec(D_SSM),
        pl.BlockSpec((g_items, SSM_HEADS, SSM_HEADDIM, D_STATE), lambda bi, c: (bi, 0, 0, 0)),
        pl.BlockSpec((g_items, CONV_K - 1, CONV_DIM), lambda bi, c: (bi, 0, 0)),
    )
    out_shape = (
        jax.ShapeDtypeStruct((b * l, D_SSM), _BF),
        jax.ShapeDtypeStruct((b, SSM_HEADS, SSM_HEADDIM, D_STATE), _F32),
        jax.ShapeDtypeStruct((b, CONV_K - 1, CONV_DIM), _F32),
    )
    return pl.pallas_call(
        functools.partial(_ssd_kernel, g_items=g_items, q=q),
        grid=(b // g_items, nc), in_specs=in_specs, out_specs=out_specs, out_shape=out_shape,
        scratch_shapes=[pltpu.VMEM((g_items, _XP_OFF + q, CONV_DIM), _F32), pltpu.VMEM((q, D_SSM), _F32)],
        compiler_params=pltpu.CompilerParams(dimension_semantics=("arbitrary", "arbitrary"),
                                             vmem_limit_bytes=VMEM_LIMIT),
        name="ssd",
    )(xbc, z, dtr, conv_buf, h0, p["conv_w"], p["conv_b"], p["dt_bias"], p["a_log"], p["d_skip"], p["ssm_norm"],
      p["sel"])


def _out_mlp_kernel(x_ref, oa_ref, os_ref, wo_ref, gm_ref, wu_ref, wd_ref, gf_ref, y_ref, *, f_chunk):
    mix = jnp.concatenate([oa_ref[...], os_ref[...]], axis=1)
    x1 = x_ref[...] + _dot(mix, wo_ref[...])
    h2 = _rms(x1, gm_ref[...]).astype(_BF)
    acc = x1
    for f0 in range(0, D_FF, f_chunk):
        u = jnp.maximum(_dot(h2, wu_ref[:, f0:f0 + f_chunk]), 0.0)
        acc = acc + _dot((u * u).astype(_BF), wd_ref[f0:f0 + f_chunk, :])
    y_ref[...] = _rms(acc, gf_ref[...])


def _out_mlp(x2d, oa, osm, p, tm):
    t = x2d.shape[0]
    row_spec = lambda w: pl.BlockSpec((tm, w), lambda i: (i, 0))
    once = lambda shape: pl.BlockSpec(shape, lambda i: (0,) * len(shape), pipeline_mode=pl.Buffered(1))
    return pl.pallas_call(
        functools.partial(_out_mlp_kernel, f_chunk=1024),
        grid=(t // tm,),
        in_specs=[row_spec(D_MODEL), row_spec(D_ATTN), row_spec(D_SSM), once((D_MODEL, D_MODEL)),
                  once((1, D_MODEL)), once((D_MODEL, D_FF)), once((D_FF, D_MODEL)), once((1, D_MODEL))],
        out_specs=row_spec(D_MODEL),
        out_shape=jax.ShapeDtypeStruct((t, D_MODEL), _F32),
        compiler_params=pltpu.CompilerParams(dimension_semantics=("arbitrary",), vmem_limit_bytes=VMEM_LIMIT),
        name="out_mlp",
    )(x2d, oa, osm, p["w_out"], p["g_mlp"], p["w_up"], p["w_down"], p["g_final"])


def _prep_params(norm_attn, w_in, q_norm, kv_norm, w_uq, w_uk, w_uv, conv_w, conv_b, dt_bias, a_log, d_skip,
                 ssm_norm, w_out, norm_mlp, w_up, w_down, norm_final):
    half = QK_ROPE // 2

    def swap(w):
        return jnp.concatenate([w[..., half:], w[..., :half]], axis=-1)

    def place(w, lead, n):
        return jnp.pad(w, ((0, 0), (lead, n - lead - w.shape[1])))

    o1 = Q_LORA
    o2 = o1 + KV_LORA
    o3 = o2 + QK_ROPE
    o5 = o3 + D_SSM + CONV_DIM
    w_kpe = w_in[:, o2:o3]

    def w_in_padded(lead):
        return jnp.concatenate([
            w_in[:, :o2], place(w_kpe, lead, LANES), place(swap(w_kpe), lead, LANES), w_in[:, o3:o5],
            place(w_in[:, o5:], 0, LANES)], axis=1).astype(_BF)

    uq = w_uq.reshape(Q_LORA, MLA_HEADS, QK_NOPE + QK_ROPE)
    uq_nope = uq[:, :, :QK_NOPE]
    uq_rope = uq[:, :, QK_NOPE:]
    w_uq_p = jnp.concatenate([
        jnp.pad(uq_nope, ((0, 0), (0, 0), (0, LANES - QK_NOPE))).reshape(Q_LORA, MLA_HEADS * LANES),
        uq_rope.reshape(Q_LORA, -1), swap(uq_rope).reshape(Q_LORA, -1)], axis=1).astype(_BF)
    w_uk_p = jnp.pad(w_uk, ((0, 0), (0, LANES - QK_NOPE), (0, 0))).astype(_BF)
    tail = LANES - QK_NOPE - QK_ROPE
    plain = jnp.pad(uq, ((0, 0), (0, 0), (0, tail)))
    swapped = jnp.pad(swap(uq_rope), ((0, 0), (0, 0), (QK_NOPE, tail)))
    w_uq_t = jnp.concatenate([plain.reshape(Q_LORA, -1), swapped.reshape(Q_LORA, -1)], axis=1).T.astype(_BF)
    w_k = jnp.pad(jnp.transpose(w_uk, (2, 0, 1)), ((0, 0), (0, 0), (0, LANES - QK_NOPE)))
    w_k = w_k.reshape(KV_LORA, MLA_HEADS * LANES).astype(_BF)
    w_uv_t = jnp.transpose(w_uv, (0, 2, 1)).reshape(D_ATTN, KV_LORA).astype(_BF)
    pad_heads = lambda v: jnp.pad(v.reshape(1, SSM_HEADS), ((0, 0), (0, LANES - SSM_HEADS)))
    return dict(
        g_attn=norm_attn.reshape(1, D_MODEL), w_in=w_in_padded(0), w_in_prompt=w_in_padded(QK_NOPE),
        q_norm=q_norm.reshape(1, Q_LORA), kv_norm=kv_norm.reshape(1, KV_LORA), w_uq=w_uq_p, w_uk=w_uk_p,
        w_uq_t=w_uq_t, w_k=w_k, w_uv_t=w_uv_t, w_uv=w_uv.astype(_BF),
        conv_w=conv_w, conv_b=conv_b.reshape(1, CONV_DIM), dt_bias=pad_heads(dt_bias), a_log=pad_heads(a_log),
        d_skip=jnp.repeat(d_skip, SSM_HEADDIM).reshape(1, D_SSM), ssm_norm=ssm_norm.reshape(1, D_SSM),
        sel=jnp.eye(SSM_HEADS, LANES, dtype=_F32),
        w_out=w_out.astype(_BF), g_mlp=norm_mlp.reshape(1, D_MODEL), w_up=w_up.astype(_BF),
        w_down=w_down.astype(_BF), g_final=norm_final.reshape(1, D_MODEL),
    )


def _largest_divisor(n, cap):
    d = min(n, cap)
    while n % d:
        d -= 1
    return d


def kernel(x_prompt, x_sample, cache_kv_latent, cache_k_rope, state_ssm, state_conv, page_table, norm_attn, w_in, q_norm, kv_norm, w_uq, w_uk, w_uv, conv_w, conv_b, dt_bias, a_log, d_skip, ssm_norm, w_out, norm_mlp, w_up, w_down, norm_final):
    assert norm_attn.shape[0] == 1, "single-layer model"
    p = _prep_params(norm_attn[0], w_in[0], q_norm[0], kv_norm[0], w_uq[0], w_uk[0], w_uv[0], conv_w[0], conv_b[0],
                     dt_bias[0], a_log[0], d_skip[0], ssm_norm[0], w_out[0], norm_mlp[0], w_up[0], w_down[0],
                     norm_final)
    bp, lp, _ = x_prompt.shape
    bs, ls, _ = x_sample.shape
    n_pages = page_table.shape[1]
    page = cache_kv_latent.shape[2]
    past_len = n_pages * page

    xp2 = x_prompt.reshape(bp * lp, D_MODEL)
    qt, kh, vt, c_new, kr_new, z, xbc, dtr = _in_proj_prompt(xp2, bp, lp, p, _largest_divisor(lp, 512))
    o_attn = _attn_prompt(qt, kh, vt, bp, lp, _largest_divisor(lp, 256))
    chunk = _largest_divisor(lp, CHUNK)
    o_ssm, h_p, conv_p = _ssd(xbc, z, dtr, jnp.zeros((bp, CONV_K - 1, CONV_DIM), _F32),
                              jnp.zeros((bp, SSM_HEADS, SSM_HEADDIM, D_STATE), _F32), p, bp, lp, chunk, 1)
    y_p = _out_mlp(xp2, o_attn, o_ssm, p, _largest_divisor(bp * lp, 512))
    outs_p = (y_p.reshape(bp, lp, D_MODEL), c_new.reshape(1, bp, lp, KV_LORA), kr_new.reshape(1, bp, lp, QK_ROPE),
              h_p[None], conv_p[None])

    xs2 = x_sample.reshape(bs * ls, D_MODEL)
    tb = _largest_divisor(bs, 512 // ls)
    q, kv, c_new, kr_new, z, xbc, dtr = _in_proj(xs2, bs, ls, past_len, p, tb, ls)
    cache_krt = jnp.swapaxes(cache_k_rope[0], 1, 2)
    o_lat = _attn_sample(q.reshape(bs, MLA_HEADS * ls, QK_PAD), kv.reshape(bs, ls, QK_PAD), cache_kv_latent[0],
                         cache_krt, page_table)
    o_attn = _uv_proj(o_lat.reshape(bs, MLA_HEADS, ls, KV_LORA), p["w_uv"], tb)
    g_items = _largest_divisor(bs, 8)
    o_ssm, h_s, conv_s = _ssd(xbc, z, dtr, state_conv[0], state_ssm[0], p, bs, ls, ls, g_items)
    y_s = _out_mlp(xs2, o_attn, o_ssm, p, _largest_divisor(bs * ls, 512))
    outs_s = (y_s.reshape(bs, ls, D_MODEL), c_new.reshape(1, bs, ls, KV_LORA), kr_new.reshape(1, bs, ls, QK_ROPE),
              h_s[None], conv_s[None])

    return (outs_p[0], outs_s[0], outs_p[1], outs_p[2], outs_p[3], outs_p[4],
            outs_s[1], outs_s[2], outs_s[3], outs_s[4])
```

```python
import functools

import jax
import jax.numpy as jnp
from jax import lax
from jax.experimental import pallas as pl
from jax.experimental.pallas import tpu as pltpu

D_MODEL = 1024
D_ATTN = 512
D_SSM = 512
V_HEAD = 64
MLA_HEADS = 8
QK_NOPE = 64
QK_ROPE = 32
KV_LORA = 256
Q_LORA = 384
ROPE_THETA = 10000.0
SSM_HEADDIM = 64
SSM_HEADS = 8
SSM_GROUPS = 2
D_STATE = 128
CONV_K = 4
CHUNK = 256
CONV_DIM = D_SSM + 2 * SSM_GROUPS * D_STATE
D_FF = 4096
EPS = 1e-6

LANES = 128
SUBLANES = 8
QK_AHEAD = 2
QK_PAD = KV_LORA + LANES
VMEM_LIMIT = 56 * 1024 * 1024
SCALE_LOG2 = (QK_NOPE + QK_ROPE) ** -0.5 * 1.4426950408889634

_C_QLAT = 0
_C_KV = _C_QLAT + Q_LORA
_C_KPE = _C_KV + KV_LORA
_C_KPES = _C_KPE + LANES
_C_Z = _C_KPES + LANES
_C_XBC = _C_Z + D_SSM
_C_DT = _C_XBC + CONV_DIM
_N_IN_PAD = _C_DT + LANES
_Q_NOPE = 0
_Q_ROPE = MLA_HEADS * LANES
_Q_ROPES = _Q_ROPE + MLA_HEADS * QK_ROPE
_N_UQ_PAD = _Q_ROPES + MLA_HEADS * QK_ROPE

_BF = jnp.bfloat16
_F32 = jnp.float32


def _dot(a, b):
    return jnp.dot(a, b, preferred_element_type=_F32)


def _dot_nt(a, b):
    return lax.dot_general(a, b, (((1,), (1,)), ((), ())), preferred_element_type=_F32)


def _dot_tn(a, b):
    return lax.dot_general(a, b, (((0,), (0,)), ((), ())), preferred_element_type=_F32)


def _rms(x, g):
    return x * lax.rsqrt(jnp.mean(x * x, axis=-1, keepdims=True) + EPS) * g


def _silu(x):
    return x * (1.0 / (1.0 + jnp.exp(-x)))


def _sublane_all(op, x):
    for shift in (4, 2, 1):
        x = op(x, pltpu.roll(x, shift, 0))
    return x


def _const_spec(shape):
    nd = len(shape)
    return pl.BlockSpec(shape, lambda *_: (0,) * nd)


def _rope_tables(l, past_len):
    pos = (past_len + jnp.arange(l, dtype=jnp.int32)).astype(_F32)
    inv = ROPE_THETA ** (-(jnp.arange(0, QK_ROPE, 2, dtype=_F32) / QK_ROPE))
    ang = pos[:, None] * inv[None, :]
    cos, sin = jnp.cos(ang), jnp.sin(ang)
    return jnp.concatenate([cos, cos], axis=1), jnp.concatenate([-sin, sin], axis=1)


def _in_proj_kernel(x_ref, g_ref, w_ref, qn_ref, kvn_ref, wuq_ref, wuk_ref, cosq_ref, sinq_ref, cosk_ref, sink_ref,
                    q_ref, kv_ref, c_ref, kr_ref, z_ref, xbc_ref, dt_ref, *, tb, tl):
    tm = tb * tl

    def rows(tab_ref):
        t = tab_ref[...]
        if tb == 1:
            return t
        return jnp.broadcast_to(t[None], (tb, tl, t.shape[-1])).reshape(tm, t.shape[-1])

    h = _rms(x_ref[...], g_ref[...]).astype(_BF)
    proj = _dot(h, w_ref[...])
    z_ref[...] = proj[:, _C_Z:_C_XBC]
    xbc_ref[...] = proj[:, _C_XBC:_C_DT]
    dt_ref[...] = proj[:, _C_DT:_N_IN_PAD]

    c = _rms(proj[:, _C_KV:_C_KPE], kvn_ref[...])
    kr = proj[:, _C_KPE:_C_KPES] * rows(cosk_ref) + proj[:, _C_KPES:_C_Z] * rows(sink_ref)
    c_ref[...] = c
    kr_ref[...] = kr[:, :QK_ROPE]
    kv_ref[:, :KV_LORA] = c.astype(_BF)
    kv_ref[:, KV_LORA:] = kr.astype(_BF)

    qn = _rms(proj[:, _C_QLAT:_C_KV], qn_ref[...]).astype(_BF)
    qq = _dot(qn, wuq_ref[...])
    q_pe = qq[:, _Q_ROPE:_Q_ROPES] * rows(cosq_ref) + qq[:, _Q_ROPES:_N_UQ_PAD] * rows(sinq_ref)
    lane = lax.broadcasted_iota(jnp.int32, (tm, LANES), 1)
    heads_per_group = LANES // QK_ROPE
    for hd in range(MLA_HEADS):
        q_abs = _dot(qq[:, hd * LANES:(hd + 1) * LANES].astype(_BF), wuk_ref[hd]) * SCALE_LOG2
        grp = q_pe[:, (hd // heads_per_group) * LANES:(hd // heads_per_group + 1) * LANES]
        shift = (LANES - QK_ROPE * (hd % heads_per_group)) % LANES
        if shift:
            grp = pltpu.roll(grp, shift, 1)
        pe = jnp.where(lane < QK_ROPE, grp * SCALE_LOG2, 0.0)
        q_ref[:, hd, :, :KV_LORA] = q_abs.astype(_BF).reshape(tb, tl, KV_LORA)
        q_ref[:, hd, :, KV_LORA:] = pe.astype(_BF).reshape(tb, tl, LANES)


def _in_proj(x2d, b, l, past_len, p, tb, tl):
    t = b * l
    tm = tb * tl
    n_l = l // tl
    cos32, sin32 = _rope_tables(l, past_len)
    padk = jnp.zeros((l, LANES - QK_ROPE), _F32)
    cosk = jnp.concatenate([cos32, padk], axis=1)
    sink = jnp.concatenate([sin32, padk], axis=1)
    cosq = jnp.tile(cos32, (1, MLA_HEADS))
    sinq = jnp.tile(sin32, (1, MLA_HEADS))

    row_spec = lambda w: pl.BlockSpec((tm, w), lambda i: (i, 0))
    tab_spec = lambda w: pl.BlockSpec((tl, w), lambda i: (i % n_l, 0))
    out_shape = (
        jax.ShapeDtypeStruct((b, MLA_HEADS, l, QK_PAD), _BF),
        jax.ShapeDtypeStruct((t, QK_PAD), _BF),
        jax.ShapeDtypeStruct((t, KV_LORA), _F32),
        jax.ShapeDtypeStruct((t, QK_ROPE), _F32),
        jax.ShapeDtypeStruct((t, D_SSM), _F32),
        jax.ShapeDtypeStruct((t, CONV_DIM), _F32),
        jax.ShapeDtypeStruct((t, LANES), _F32),
    )
    out_specs = (
        pl.BlockSpec((tb, MLA_HEADS, tl, QK_PAD), lambda i: (i // n_l, 0, i % n_l, 0)),
        row_spec(QK_PAD), row_spec(KV_LORA), row_spec(QK_ROPE), row_spec(D_SSM), row_spec(CONV_DIM), row_spec(LANES),
    )
    in_specs = [
        row_spec(D_MODEL), _const_spec((1, D_MODEL)), _const_spec((D_MODEL, _N_IN_PAD)),
        _const_spec((1, Q_LORA)), _const_spec((1, KV_LORA)), _const_spec((Q_LORA, _N_UQ_PAD)),
        _const_spec((MLA_HEADS, LANES, KV_LORA)),
        tab_spec(MLA_HEADS * QK_ROPE), tab_spec(MLA_HEADS * QK_ROPE), tab_spec(LANES), tab_spec(LANES),
    ]
    return pl.pallas_call(
        functools.partial(_in_proj_kernel, tb=tb, tl=tl),
        grid=(t // tm,), in_specs=in_specs, out_specs=out_specs, out_shape=out_shape,
        compiler_params=pltpu.CompilerParams(dimension_semantics=("arbitrary",), vmem_limit_bytes=VMEM_LIMIT),
        name="in_proj",
    )(x2d, p["g_attn"], p["w_in"], p["q_norm"], p["kv_norm"], p["w_uq"], p["w_uk"], cosq, sinq, cosk, sink)


def _in_proj_prompt_kernel(x_ref, g_ref, w_ref, qn_ref, kvn_ref, wuqt_ref, wk_ref, wuvt_ref, cost_ref, sint_ref,
                           cosk_ref, sink_ref, qt_ref, kh_ref, vt_ref, c_ref, kr_ref, z_ref, xbc_ref, dt_ref):
    h = _rms(x_ref[...], g_ref[...]).astype(_BF)
    proj = _dot(h, w_ref[...])
    z_ref[...] = proj[:, _C_Z:_C_XBC]
    xbc_ref[...] = proj[:, _C_XBC:_C_DT]
    dt_ref[...] = proj[:, _C_DT:_N_IN_PAD]

    c = _rms(proj[:, _C_KV:_C_KPE], kvn_ref[...])
    kr = proj[:, _C_KPE:_C_KPES] * cosk_ref[...] + proj[:, _C_KPES:_C_Z] * sink_ref[...]
    c_ref[...] = c
    kr_ref[...] = kr[:, QK_NOPE:QK_NOPE + QK_ROPE]
    c_bf = c.astype(_BF)
    kn = _dot(c_bf, wk_ref[...])
    for hd in range(MLA_HEADS):
        kh_ref[:, hd * LANES:(hd + 1) * LANES] = (kn[:, hd * LANES:(hd + 1) * LANES] + kr).astype(_BF)
    vt_ref[0] = _dot_nt(wuvt_ref[...], c_bf).astype(_BF)

    qn = _rms(proj[:, _C_QLAT:_C_KV], qn_ref[...]).astype(_BF)
    qqt = _dot_nt(wuqt_ref[...], qn)
    half = MLA_HEADS * LANES
    for hd in range(MLA_HEADS):
        qt = (qqt[hd * LANES:(hd + 1) * LANES] * cost_ref[...]
              + qqt[half + hd * LANES:half + (hd + 1) * LANES] * sint_ref[...])
        qt_ref[0, hd] = qt.astype(_BF)


def _in_proj_prompt(x2d, b, l, p, tl):
    t = b * l
    n_l = l // tl
    cos32, sin32 = _rope_tables(l, 0)
    lead = jnp.zeros((l, QK_NOPE), _F32)
    trail = jnp.zeros((l, LANES - QK_NOPE - QK_ROPE), _F32)
    cosk = jnp.concatenate([lead, cos32, trail], axis=1)
    sink = jnp.concatenate([lead, sin32, trail], axis=1)
    cost = (jnp.concatenate([lead + 1.0, cos32, trail], axis=1) * SCALE_LOG2).T
    sint = (sink * SCALE_LOG2).T

    row_spec = lambda w: pl.BlockSpec((tl, w), lambda i: (i, 0))
    out_shape = (
        jax.ShapeDtypeStruct((b, MLA_HEADS, LANES, l), _BF),
        jax.ShapeDtypeStruct((t, MLA_HEADS * LANES), _BF),
        jax.ShapeDtypeStruct((b, D_ATTN, l), _BF),
        jax.ShapeDtypeStruct((t, KV_LORA), _F32),
        jax.ShapeDtypeStruct((t, QK_ROPE), _F32),
        jax.ShapeDtypeStruct((t, D_SSM), _F32),
        jax.ShapeDtypeStruct((t, CONV_DIM), _F32),
        jax.ShapeDtypeStruct((t, LANES), _F32),
    )
    out_specs = (
        pl.BlockSpec((1, MLA_HEADS, LANES, tl), lambda i: (i // n_l, 0, 0, i % n_l)),
        row_spec(MLA_HEADS * LANES),
        pl.BlockSpec((1, D_ATTN, tl), lambda i: (i // n_l, 0, i % n_l)),
        row_spec(KV_LORA), row_spec(QK_ROPE), row_spec(D_SSM), row_spec(CONV_DIM), row_spec(LANES),
    )
    in_specs = [
        row_spec(D_MODEL), _const_spec((1, D_MODEL)), _const_spec((D_MODEL, _N_IN_PAD)),
        _const_spec((1, Q_LORA)), _const_spec((1, KV_LORA)), _const_spec((2 * MLA_HEADS * LANES, Q_LORA)),
        _const_spec((KV_LORA, MLA_HEADS * LANES)), _const_spec((D_ATTN, KV_LORA)),
        pl.BlockSpec((LANES, tl), lambda i: (0, i % n_l)), pl.BlockSpec((LANES, tl), lambda i: (0, i % n_l)),
        pl.BlockSpec((tl, LANES), lambda i: (i % n_l, 0)), pl.BlockSpec((tl, LANES), lambda i: (i % n_l, 0)),
    ]
    return pl.pallas_call(
        _in_proj_prompt_kernel,
        grid=(t // tl,), in_specs=in_specs, out_specs=out_specs, out_shape=out_shape,
        compiler_params=pltpu.CompilerParams(dimension_semantics=("arbitrary",), vmem_limit_bytes=VMEM_LIMIT),
        name="in_proj_prompt",
    )(x2d, p["g_attn"], p["w_in_prompt"], p["q_norm"], p["kv_norm"], p["w_uq_t"], p["w_k"], p["w_uv_t"],
      cost, sint, cosk, sink)


def _attn_prompt_kernel(it_ref, jt_ref, qt_ref, kh_ref, vt_ref, o_ref, m_ref, l_ref, acc_ref, *, tq):
    s_idx = pl.program_id(1)
    i = it_ref[s_idx]
    j = jt_ref[s_idx]

    @pl.when(j == 0)
    def _():
        m_ref[...] = jnp.full(m_ref.shape, -jnp.inf, _F32)
        l_ref[...] = jnp.zeros(l_ref.shape, _F32)
        acc_ref[...] = jnp.zeros(acc_ref.shape, _F32)

    def step(masked):
        if masked:
            krow = lax.broadcasted_iota(jnp.int32, (tq, tq), 0)
            qcol = lax.broadcasted_iota(jnp.int32, (tq, tq), 1)
            keep = krow <= qcol

        def scores(hd):
            st = _dot(kh_ref[:, hd * LANES:(hd + 1) * LANES], qt_ref[0, hd])
            if masked:
                st = jnp.where(keep, st, -jnp.inf)
            return st.reshape(tq // SUBLANES, SUBLANES, tq)

        ahead = {hd: scores(hd) for hd in range(QK_AHEAD)}
        for hd in range(MLA_HEADS):
            if hd + QK_AHEAD < MLA_HEADS:
                ahead[hd + QK_AHEAD] = scores(hd + QK_AHEAD)
            st = ahead.pop(hd)
            m_prev = m_ref[hd]
            m_new = jnp.maximum(m_prev, _sublane_all(jnp.maximum, jnp.max(st, axis=0)))
            alpha = jnp.exp2(m_prev - m_new)
            pr = jnp.exp2(st - m_new[None])
            l_ref[hd] = alpha * l_ref[hd] + jnp.sum(pr, axis=0)
            rows = slice(hd * V_HEAD, (hd + 1) * V_HEAD)
            acc = acc_ref[rows, :].reshape(V_HEAD // SUBLANES, SUBLANES, tq) * alpha[None]
            acc_ref[rows, :] = acc.reshape(V_HEAD, tq) + _dot(vt_ref[0, rows, :], pr.reshape(tq, tq).astype(_BF))
            m_ref[hd] = m_new

    @pl.when(j < i)
    def _():
        step(False)

    @pl.when(j == i)
    def _():
        step(True)
        for hd in range(MLA_HEADS):
            rows = slice(hd * V_HEAD, (hd + 1) * V_HEAD)
            den = _sublane_all(jnp.add, l_ref[hd])
            acc = acc_ref[rows, :].reshape(V_HEAD // SUBLANES, SUBLANES, tq) / den[None]
            acc_ref[rows, :] = acc.reshape(V_HEAD, tq)
        o_ref[...] = acc_ref[...].T.astype(o_ref.dtype)


def _attn_prompt(qt, kh, vt, b, l, tq):
    nq = l // tq
    pairs = [(i, j) for i in range(nq) for j in range(i + 1)]
    it = jnp.asarray([pr[0] for pr in pairs], jnp.int32)
    jt = jnp.asarray([pr[1] for pr in pairs], jnp.int32)
    grid_spec = pltpu.PrefetchScalarGridSpec(
        num_scalar_prefetch=2,
        grid=(b, len(pairs)),
        in_specs=[
            pl.BlockSpec((1, MLA_HEADS, LANES, tq), lambda bi, s, it, jt: (bi, 0, 0, it[s])),
            pl.BlockSpec((tq, MLA_HEADS * LANES), lambda bi, s, it, jt: (bi * nq + jt[s], 0)),
            pl.BlockSpec((1, D_ATTN, tq), lambda bi, s, it, jt: (bi, 0, jt[s])),
        ],
        out_specs=pl.BlockSpec((tq, D_ATTN), lambda bi, s, it, jt: (bi * nq + it[s], 0)),
        scratch_shapes=[pltpu.VMEM((MLA_HEADS, SUBLANES, tq), _F32), pltpu.VMEM((MLA_HEADS, SUBLANES, tq), _F32),
                        pltpu.VMEM((D_ATTN, tq), _F32)],
    )
    return pl.pallas_call(
        functools.partial(_attn_prompt_kernel, tq=tq),
        grid_spec=grid_spec,
        out_shape=jax.ShapeDtypeStruct((b * l, D_ATTN), _BF),
        compiler_params=pltpu.CompilerParams(dimension_semantics=("arbitrary", "arbitrary"),
                                             vmem_limit_bytes=VMEM_LIMIT),
        name="attn_prompt",
    )(it, jt, qt, kh, vt)


def _attn_sample_kernel(pt_ref, q_ref, kvn_ref, cache_c_ref, cache_krt_ref, o_ref, cbuf_ref, krbuf_ref, kc_ref,
                        sem_ref, *, n_pages, page, l_new, n_split):
    b = pl.program_id(0)
    nb = pl.num_programs(0)
    slot = b % 2

    def page_copies(item, slot_, pg):
        pid = pt_ref[item, pg]
        dst_rows = pl.ds(pl.multiple_of(pg * page, page), page)
        return (pltpu.make_async_copy(cache_c_ref.at[pid], cbuf_ref.at[slot_, dst_rows, :], sem_ref.at[slot_, 0]),
                pltpu.make_async_copy(cache_krt_ref.at[pid], krbuf_ref.at[slot_, :, dst_rows], sem_ref.at[slot_, 1]))

    def start_item(item, slot_):
        def body(pg, carry):
            for cp in page_copies(item, slot_, pg):
                cp.start()
            return carry
        lax.fori_loop(0, n_pages, body, 0)

    def wait_item(item, slot_):
        def body(pg, carry):
            for cp in page_copies(item, slot_, pg):
                cp.wait()
            return carry
        lax.fori_loop(0, n_pages, body, 0)

    @pl.when(b == 0)
    def _():
        start_item(0, 0)

    wait_item(b, slot)
    nxt = jnp.minimum(b + 1, nb - 1)
    for pg in range(n_pages):
        for cp in page_copies(nxt, 1 - slot, pg):
            cp.start()

    m_rows = MLA_HEADS * l_new
    q = q_ref[0]
    q_abs = q[:, :KV_LORA]
    q_pe = q[:, KV_LORA:KV_LORA + QK_ROPE]
    ks = (n_pages // n_split) * page
    def scores(sp):
        rows = pl.ds(sp * ks, ks)
        kc_ref[rows, :] = cbuf_ref[slot, rows, :].astype(_BF)
        return _dot_nt(q_abs, kc_ref[rows, :]) + _dot(q_pe, krbuf_ref[slot, :, rows].astype(_BF))

    partials = []
    s_next = scores(0)
    for sp in range(n_split):
        s = s_next
        if sp + 1 < n_split:
            s_next = scores(sp + 1)
        m = jnp.max(s, axis=1, keepdims=True)
        pr = jnp.exp2(s - m)
        partials.append((m, jnp.sum(pr, axis=1, keepdims=True),
                         _dot(pr.astype(_BF), kc_ref[pl.ds(sp * ks, ks), :])))
    kvn = kvn_ref[0]
    sn = _dot_nt(q, kvn)
    qpos = lax.broadcasted_iota(jnp.int32, (MLA_HEADS, l_new, l_new), 1).reshape(m_rows, l_new)
    kpos = lax.broadcasted_iota(jnp.int32, (m_rows, l_new), 1)
    sn = jnp.where(kpos <= qpos, sn, -jnp.inf)
    m = jnp.max(sn, axis=1, keepdims=True)
    pn = jnp.exp2(sn - m)
    partials.append((m, jnp.sum(pn, axis=1, keepdims=True), _dot(pn.astype(_BF), kvn[:, :KV_LORA])))
    m_all = functools.reduce(jnp.maximum, [pt[0] for pt in partials])
    den = sum(pt[1] * jnp.exp2(pt[0] - m_all) for pt in partials)
    o = sum(pt[2] * jnp.exp2(pt[0] - m_all) for pt in partials)
    o_ref[0] = (o / den).astype(o_ref.dtype)

    @pl.when(b == nb - 1)
    def _():
        wait_item(nxt, 1 - slot)


def _attn_sample(q, kvn, cache_c, cache_krt, page_table):
    b, m_rows, _ = q.shape
    l_new = m_rows // MLA_HEADS
    n_pages = page_table.shape[1]
    page = cache_c.shape[1]
    keys = n_pages * page
    grid_spec = pltpu.PrefetchScalarGridSpec(
        num_scalar_prefetch=1,
        grid=(b,),
        in_specs=[pl.BlockSpec((1, m_rows, QK_PAD), lambda bi, pt: (bi, 0, 0)),
                  pl.BlockSpec((1, l_new, QK_PAD), lambda bi, pt: (bi, 0, 0)),
                  pl.BlockSpec(memory_space=pl.ANY), pl.BlockSpec(memory_space=pl.ANY)],
        out_specs=pl.BlockSpec((1, m_rows, KV_LORA), lambda bi, pt: (bi, 0, 0)),
        scratch_shapes=[pltpu.VMEM((2, keys, KV_LORA), _F32), pltpu.VMEM((2, QK_ROPE, keys), _F32),
                        pltpu.VMEM((keys, KV_LORA), _BF), pltpu.SemaphoreType.DMA((2, 2))],
    )
    return pl.pallas_call(
        functools.partial(_attn_sample_kernel, n_pages=n_pages, page=page, l_new=l_new,
                          n_split=_largest_divisor(n_pages, 4)),
        grid_spec=grid_spec,
        out_shape=jax.ShapeDtypeStruct((b, m_rows, KV_LORA), _BF),
        compiler_params=pltpu.CompilerParams(dimension_semantics=("arbitrary",), vmem_limit_bytes=VMEM_LIMIT),
        name="attn_sample",
    )(page_table, q, kvn, cache_c, cache_krt)


def _uv_proj_kernel(o_ref, wuv_ref, out_ref, *, tb, l_new):
    for hd in range(MLA_HEADS):
        o = o_ref[:, hd].reshape(tb * l_new, KV_LORA)
        out_ref[:, hd * V_HEAD:(hd + 1) * V_HEAD] = _dot(o, wuv_ref[hd]).astype(out_ref.dtype)


def _uv_proj(o_lat, wuv, tb):
    b, _, l_new, _ = o_lat.shape
    return pl.pallas_call(
        functools.partial(_uv_proj_kernel, tb=tb, l_new=l_new),
        grid=(b // tb,),
        in_specs=[pl.BlockSpec((tb, MLA_HEADS, l_new, KV_LORA), lambda i: (i, 0, 0, 0)),
                  _const_spec((MLA_HEADS, KV_LORA, V_HEAD))],
        out_specs=pl.BlockSpec((tb * l_new, D_ATTN), lambda i: (i, 0)),
        out_shape=jax.ShapeDtypeStruct((b * l_new, D_ATTN), _BF),
        compiler_params=pltpu.CompilerParams(dimension_semantics=("arbitrary",), vmem_limit_bytes=VMEM_LIMIT),
        name="uv_proj",
    )(o_lat, wuv)


_XP_OFF = 8


def _ssd_kernel(xbc_ref, z_ref, dtr_ref, cbuf_ref, h0_ref, cw_ref, cb_ref, dtb_ref, alog_ref, dskip_ref, nrm_ref,
                sel_ref, o_ref, h_ref, cnew_ref, xp_ref, y_ref, *, g_items, q):
    c_idx = pl.program_id(1)
    last = pl.num_programs(1) - 1
    heads_per_group = SSM_HEADS // SSM_GROUPS
    row = lax.broadcasted_iota(jnp.int32, (q, q), 0)
    col = lax.broadcasted_iota(jnp.int32, (q, q), 1)
    causal = row >= col
    tril = jnp.where(causal, 1.0, 0.0).astype(_F32)
    a_neg = -jnp.exp(alog_ref[...])

    for g in range(g_items):
        @pl.when(c_idx == 0)
        def _():
            xp_ref[g, _XP_OFF - (CONV_K - 1):_XP_OFF, :] = cbuf_ref[g]
            h_ref[g] = h0_ref[g]

        xp_ref[g, _XP_OFF:_XP_OFF + q, :] = xbc_ref[g * q:(g + 1) * q, :]
        conv = cb_ref[...]
        for k in range(CONV_K):
            conv = conv + xp_ref[g, _XP_OFF - (CONV_K - 1) + k:_XP_OFF - (CONV_K - 1) + k + q, :] * cw_ref[k:k + 1, :]
        conv = _silu(conv)
        tail = xp_ref[g, _XP_OFF + q - (CONV_K - 1):_XP_OFF + q, :]
        xp_ref[g, _XP_OFF - (CONV_K - 1):_XP_OFF, :] = tail

        @pl.when(c_idx == last)
        def _():
            cnew_ref[g] = tail

        xs = conv[:, :D_SSM]
        bmat = conv[:, D_SSM:D_SSM + SSM_GROUPS * D_STATE].astype(_BF)
        cmat = conv[:, D_SSM + SSM_GROUPS * D_STATE:].astype(_BF)
        dtv = dtr_ref[g * q:(g + 1) * q, :] + dtb_ref[...]
        dt = jnp.maximum(dtv, 0.0) + jnp.log1p(jnp.exp(-jnp.abs(dtv)))
        acs = jnp.dot(tril, dt * a_neg, preferred_element_type=_F32, precision=lax.Precision.HIGHEST)
        acs_t = lax.dot_general(sel_ref[...], acs, (((1,), (1,)), ((), ())), preferred_element_type=_F32,
                                precision=lax.Precision.HIGHEST)
        for grp in range(SSM_GROUPS):
            b_g = bmat[:, grp * D_STATE:(grp + 1) * D_STATE]
            c_g = cmat[:, grp * D_STATE:(grp + 1) * D_STATE]
            cb = _dot_nt(c_g, b_g)
            for hd in range(grp * heads_per_group, (grp + 1) * heads_per_group):
                acs_h = acs[:, hd:hd + 1]
                acs_last = acs[q - 1:q, hd:hd + 1]
                lmat = jnp.exp(jnp.where(causal, acs_h - acs_t[hd:hd + 1, :], -jnp.inf))
                x_h = xs[:, hd * SSM_HEADDIM:(hd + 1) * SSM_HEADDIM]
                x_dt = x_h * dt[:, hd:hd + 1]
                h_prev = h_ref[g, hd]
                y = _dot((cb * lmat).astype(_BF), x_dt.astype(_BF))
                y = y + _dot_nt(c_g, h_prev.astype(_BF)) * jnp.exp(acs_h)
                xw = (x_dt * jnp.exp(acs_last - acs_h)).astype(_BF)
                h_ref[g, hd] = jnp.exp(acs_last) * h_prev + _dot_tn(xw, b_g)
                y_ref[:, hd * SSM_HEADDIM:(hd + 1) * SSM_HEADDIM] = y
        yv = y_ref[...] + dskip_ref[...] * xs
        gated = yv * _silu(z_ref[g * q:(g + 1) * q, :])
        gw = D_SSM // SSM_GROUPS
        parts = []
        for grp in range(SSM_GROUPS):
            gg = gated[:, grp * gw:(grp + 1) * gw]
            parts.append(gg * lax.rsqrt(jnp.mean(gg * gg, axis=-1, keepdims=True) + EPS))
        o_ref[g * q:(g + 1) * q, :] = (jnp.concatenate(parts, axis=1) * nrm_ref[...]).astype(o_ref.dtype)


def _ssd(xbc, z, dtr, conv_buf, h0, p, b, l, q, g_items):
    nc = l // q
    rows = g_items * q
    row_spec = lambda w: pl.BlockSpec((rows, w), lambda bi, c: (bi * nc + c, 0))
    in_specs = [
        row_spec(CONV_DIM), row_spec(D_SSM), row_spec(LANES),
        pl.BlockSpec((g_items, CONV_K - 1, CONV_DIM), lambda bi, c: (bi, 0, 0)),
        pl.BlockSpec((g_items, SSM_HEADS, SSM_HEADDIM, D_STATE), lambda bi, c: (bi, 0, 0, 0)),
        _const_spec((CONV_K, CONV_DIM)), _const_spec((1, CONV_DIM)), _const_spec((1, LANES)), _const_spec((1, LANES)),
        _const_spec((1, D_SSM)), _const_spec((1, D_SSM)), _const_spec((SSM_HEADS, LANES)),
    ]
    out_specs = (
        row_spec(D_SSM),
        pl.BlockSpec((g_items, SSM_HEADS, SSM_HEADDIM, D_STATE), lambda bi, c: (bi, 0, 0, 0)),
        pl.BlockSpec((g_items, CONV_K - 1, CONV_DIM), lambda bi, c: (bi, 0, 0)),
    )
    out_shape = (
        jax.ShapeDtypeStruct((b * l, D_SSM), _BF),
        jax.ShapeDtypeStruct((b, SSM_HEADS, SSM_HEADDIM, D_STATE), _F32),
        jax.ShapeDtypeStruct((b, CONV_K - 1, CONV_DIM), _F32),
    )
    return pl.pallas_call(
        functools.partial(_ssd_kernel, g_items=g_items, q=q),
        grid=(b // g_items, nc), in_specs=in_specs, out_specs=out_specs, out_shape=out_shape,
        scratch_shapes=[pltpu.VMEM((g_items, _XP_OFF + q, CONV_DIM), _F32), pltpu.VMEM((q, D_SSM), _F32)],
        compiler_params=pltpu.CompilerParams(dimension_semantics=("arbitrary", "arbitrary"),
                                             vmem_limit_bytes=VMEM_LIMIT),
        name="ssd",
    )(xbc, z, dtr, conv_buf, h0, p["conv_w"], p["conv_b"], p["dt_bias"], p["a_log"], p["d_skip"], p["ssm_norm"],
      p["sel"])


def _out_mlp_kernel(x_ref, oa_ref, os_ref, wo_ref, gm_ref, wu_ref, wd_ref, gf_ref, y_ref, *, f_chunk):
    mix = jnp.concatenate([oa_ref[...], os_ref[...]], axis=1)
    x1 = x_ref[...] + _dot(mix, wo_ref[...])
    h2 = _rms(x1, gm_ref[...]).astype(_BF)
    acc = x1
    for f0 in range(0, D_FF, f_chunk):
        u = jnp.maximum(_dot(h2, wu_ref[:, f0:f0 + f_chunk]), 0.0)
        acc = acc + _dot((u * u).astype(_BF), wd_ref[f0:f0 + f_chunk, :])
    y_ref[...] = _rms(acc, gf_ref[...])


def _out_mlp(x2d, oa, osm, p, tm):
    t = x2d.shape[0]
    row_spec = lambda w: pl.BlockSpec((tm, w), lambda i: (i, 0))
    once = lambda shape: pl.BlockSpec(shape, lambda i: (0,) * len(shape), pipeline_mode=pl.Buffered(1))
    return pl.pallas_call(
        functools.partial(_out_mlp_kernel, f_chunk=1024),
        grid=(t // tm,),
        in_specs=[row_spec(D_MODEL), row_spec(D_ATTN), row_spec(D_SSM), once((D_MODEL, D_MODEL)),
                  once((1, D_MODEL)), once((D_MODEL, D_FF)), once((D_FF, D_MODEL)), once((1, D_MODEL))],
        out_specs=row_spec(D_MODEL),
        out_shape=jax.ShapeDtypeStruct((t, D_MODEL), _F32),
        compiler_params=pltpu.CompilerParams(dimension_semantics=("arbitrary",), vmem_limit_bytes=VMEM_LIMIT),
        name="out_mlp",
    )(x2d, oa, osm, p["w_out"], p["g_mlp"], p["w_up"], p["w_down"], p["g_final"])


def _prep_params(norm_attn, w_in, q_norm, kv_norm, w_uq, w_uk, w_uv, conv_w, conv_b, dt_bias, a_log, d_skip,
                 ssm_norm, w_out, norm_mlp, w_up, w_down, norm_final):
    half = QK_ROPE // 2

    def swap(w):
        return jnp.concatenate([w[..., half:], w[..., :half]], axis=-1)

    def place(w, lead, n):
        return jnp.pad(w, ((0, 0), (lead, n - lead - w.shape[1])))

    o1 = Q_LORA
    o2 = o1 + KV_LORA
    o3 = o2 + QK_ROPE
    o5 = o3 + D_SSM + CONV_DIM
    w_kpe = w_in[:, o2:o3]

    def w_in_padded(lead):
        return jnp.concatenate([
            w_in[:, :o2], place(w_kpe, lead, LANES), place(swap(w_kpe), lead, LANES), w_in[:, o3:o5],
            place(w_in[:, o5:], 0, LANES)], axis=1).astype(_BF)

    uq = w_uq.reshape(Q_LORA, MLA_HEADS, QK_NOPE + QK_ROPE)
    uq_nope = uq[:, :, :QK_NOPE]
    uq_rope = uq[:, :, QK_NOPE:]
    w_uq_p = jnp.concatenate([
        jnp.pad(uq_nope, ((0, 0), (0, 0), (0, LANES - QK_NOPE))).reshape(Q_LORA, MLA_HEADS * LANES),
        uq_rope.reshape(Q_LORA, -1), swap(uq_rope).reshape(Q_LORA, -1)], axis=1).astype(_BF)
    w_uk_p = jnp.pad(w_uk, ((0, 0), (0, LANES - QK_NOPE), (0, 0))).astype(_BF)
    tail = LANES - QK_NOPE - QK_ROPE
    plain = jnp.pad(uq, ((0, 0), (0, 0), (0, tail)))
    swapped = jnp.pad(swap(uq_rope), ((0, 0), (0, 0), (QK_NOPE, tail)))
    w_uq_t = jnp.concatenate([plain.reshape(Q_LORA, -1), swapped.reshape(Q_LORA, -1)], axis=1).T.astype(_BF)
    w_k = jnp.pad(jnp.transpose(w_uk, (2, 0, 1)), ((0, 0), (0, 0), (0, LANES - QK_NOPE)))
    w_k = w_k.reshape(KV_LORA, MLA_HEADS * LANES).astype(_BF)
    w_uv_t = jnp.transpose(w_uv, (0, 2, 1)).reshape(D_ATTN, KV_LORA).astype(_BF)
    pad_heads = lambda v: jnp.pad(v.reshape(1, SSM_HEADS), ((0, 0), (0, LANES - SSM_HEADS)))
    return dict(
        g_attn=norm_attn.reshape(1, D_MODEL), w_in=w_in_padded(0), w_in_prompt=w_in_padded(QK_NOPE),
        q_norm=q_norm.reshape(1, Q_LORA), kv_norm=kv_norm.reshape(1, KV_LORA), w_uq=w_uq_p, w_uk=w_uk_p,
        w_uq_t=w_uq_t, w_k=w_k, w_uv_t=w_uv_t, w_uv=w_uv.astype(_BF),
        conv_w=conv_w, conv_b=conv_b.reshape(1, CONV_DIM), dt_bias=pad_heads(dt_bias), a_log=pad_heads(a_log),
        d_skip=jnp.repeat(d_skip, SSM_HEADDIM).reshape(1, D_SSM), ssm_norm=ssm_norm.reshape(1, D_SSM),
        sel=jnp.eye(SSM_HEADS, LANES, dtype=_F32),
        w_out=w_out.astype(_BF), g_mlp=norm_mlp.reshape(1, D_MODEL), w_up=w_up.astype(_BF),
        w_down=w_down.astype(_BF), g_final=norm_final.reshape(1, D_MODEL),
    )


def _largest_divisor(n, cap):
    d = min(n, cap)
    while n % d:
        d -= 1
    return d


def kernel(x_prompt, x_sample, cache_kv_latent, cache_k_rope, state_ssm, state_conv, page_table, norm_attn, w_in, q_norm, kv_norm, w_uq, w_uk, w_uv, conv_w, conv_b, dt_bias, a_log, d_skip, ssm_norm, w_out, norm_mlp, w_up, w_down, norm_final):
    assert norm_attn.shape[0] == 1, "single-layer model"
    p = _prep_params(norm_attn[0], w_in[0], q_norm[0], kv_norm[0], w_uq[0], w_uk[0], w_uv[0], conv_w[0], conv_b[0],
                     dt_bias[0], a_log[0], d_skip[0], ssm_norm[0], w_out[0], norm_mlp[0], w_up[0], w_down[0],
                     norm_final)
    bp, lp, _ = x_prompt.shape
    bs, ls, _ = x_sample.shape
    n_pages = page_table.shape[1]
    page = cache_kv_latent.shape[2]
    past_len = n_pages * page

    xp2 = x_prompt.reshape(bp * lp, D_MODEL)
    qt, kh, vt, c_new, kr_new, z, xbc, dtr = _in_proj_prompt(xp2, bp, lp, p, _largest_divisor(lp, 512))
    o_attn = _attn_prompt(qt, kh, vt, bp, lp, _largest_divisor(lp, 256))
    chunk = _largest_divisor(lp, CHUNK)
    o_ssm, h_p, conv_p = _ssd(xbc, z, dtr, jnp.zeros((bp, CONV_K - 1, CONV_DIM), _F32),
                              jnp.zeros((bp, SSM_HEADS, SSM_HEADDIM, D_STATE), _F32), p, bp, lp, chunk, 1)
    y_p = _out_mlp(xp2, o_attn, o_ssm, p, _largest_divisor(bp * lp, 512))
    outs_p = (y_p.reshape(bp, lp, D_MODEL), c_new.reshape(1, bp, lp, KV_LORA), kr_new.reshape(1, bp, lp, QK_ROPE),
              h_p[None], conv_p[None])

    xs2 = x_sample.reshape(bs * ls, D_MODEL)
    tb = _largest_divisor(bs, 512 // ls)
    q, kv, c_new, kr_new, z, xbc, dtr = _in_proj(xs2, bs, ls, past_len, p, tb, ls)
    cache_krt = jnp.swapaxes(cache_k_rope[0], 1, 2)
    o_lat = _attn_sample(q.reshape(bs, MLA_HEADS * ls, QK_PAD), kv.reshape(bs, ls, QK_PAD), cache_kv_latent[0],
                         cache_krt, page_table)
    o_attn = _uv_proj(o_lat.reshape(bs, MLA_HEADS, ls, KV_LORA), p["w_uv"], tb)
    g_items = _largest_divisor(bs, 8)
    o_ssm, h_s, conv_s = _ssd(xbc, z, dtr, state_conv[0], state_ssm[0], p, bs, ls, ls, g_items)
    y_s = _out_mlp(xs2, o_attn, o_ssm, p, _largest_divisor(bs * ls, 512))
    outs_s = (y_s.reshape(bs, ls, D_MODEL), c_new.reshape(1, bs, ls, KV_LORA), kr_new.reshape(1, bs, ls, QK_ROPE),
              h_s[None], conv_s[None])

    return (outs_p[0], outs_s[0], outs_p[1], outs_p[2], outs_p[3], outs_p[4],
            outs_s[1], outs_s[2], outs_s[3], outs_s[4])
```

```python
import functools

import jax
import jax.numpy as jnp
from jax import lax
from jax.experimental import pallas as pl
from jax.experimental.pallas import tpu as pltpu

D_MODEL = 1024
D_ATTN = 512
D_SSM = 512
V_HEAD = 64
MLA_HEADS = 8
QK_NOPE = 64
QK_ROPE = 32
KV_LORA = 256
Q_LORA = 384
ROPE_THETA = 10000.0
SSM_HEADDIM = 64
SSM_HEADS = 8
SSM_GROUPS = 2
D_STATE = 128
CONV_K = 4
CHUNK = 256
CONV_DIM = D_SSM + 2 * SSM_GROUPS * D_STATE
D_FF = 4096
EPS = 1e-6

LANES = 128
SUBLANES = 8
QK_AHEAD = 2
QK_PAD = KV_LORA + LANES
VMEM_LIMIT = 56 * 1024 * 1024
SCALE_LOG2 = (QK_NOPE + QK_ROPE) ** -0.5 * 1.4426950408889634

_C_QLAT = 0
_C_KV = _C_QLAT + Q_LORA
_C_KPE = _C_KV + KV_LORA
_C_KPES = _C_KPE + LANES
_C_Z = _C_KPES + LANES
_C_XBC = _C_Z + D_SSM
_C_DT = _C_XBC + CONV_DIM
_N_IN_PAD = _C_DT + LANES
_Q_NOPE = 0
_Q_ROPE = MLA_HEADS * LANES
_Q_ROPES = _Q_ROPE + MLA_HEADS * QK_ROPE
_N_UQ_PAD = _Q_ROPES + MLA_HEADS * QK_ROPE

_BF = jnp.bfloat16
_F32 = jnp.float32


def _dot(a, b):
    return jnp.dot(a, b, preferred_element_type=_F32)


def _dot_nt(a, b):
    return lax.dot_general(a, b, (((1,), (1,)), ((), ())), preferred_element_type=_F32)


def _dot_tn(a, b):
    return lax.dot_general(a, b, (((0,), (0,)), ((), ())), preferred_element_type=_F32)


def _split3(a):
    hi = a.astype(_BF)
    r1 = a - hi.astype(_F32)
    mid = r1.astype(_BF)
    lo = (r1 - mid.astype(_F32)).astype(_BF)
    return jnp.concatenate([hi, mid, lo], axis=1)


def _rms(x, g):
    return x * lax.rsqrt(jnp.mean(x * x, axis=-1, keepdims=True) + EPS) * g


def _silu(x):
    return x * (1.0 / (1.0 + jnp.exp(-x)))


def _sublane_all(op, x):
    for shift in (4, 2, 1):
        x = op(x, pltpu.roll(x, shift, 0))
    return x


def _const_spec(shape):
    nd = len(shape)
    return pl.BlockSpec(shape, lambda *_: (0,) * nd)


def _rope_tables(l, past_len):
    pos = (past_len + jnp.arange(l, dtype=jnp.int32)).astype(_F32)
    inv = ROPE_THETA ** (-(jnp.arange(0, QK_ROPE, 2, dtype=_F32) / QK_ROPE))
    ang = pos[:, None] * inv[None, :]
    cos, sin = jnp.cos(ang), jnp.sin(ang)
    return jnp.concatenate([cos, cos], axis=1), jnp.concatenate([-sin, sin], axis=1)


def _in_proj_kernel(x_ref, g_ref, w_ref, qn_ref, kvn_ref, wuq_ref, wuk_ref, cosq_ref, sinq_ref, cosk_ref, sink_ref,
                    q_ref, kv_ref, c_ref, kr_ref, z_ref, xbc_ref, dt_ref, *, tb, tl):
    tm = tb * tl

    def rows(tab_ref):
        t = tab_ref[...]
        if tb == 1:
            return t
        return jnp.broadcast_to(t[None], (tb, tl, t.shape[-1])).reshape(tm, t.shape[-1])

    h = _rms(x_ref[...], g_ref[...]).astype(_BF)
    proj = _dot(h, w_ref[...])
    z_ref[...] = proj[:, _C_Z:_C_XBC]
    xbc_ref[...] = proj[:, _C_XBC:_C_DT]
    dt_ref[...] = proj[:, _C_DT:_N_IN_PAD]

    c = _rms(proj[:, _C_KV:_C_KPE], kvn_ref[...])
    kr = proj[:, _C_KPE:_C_KPES] * rows(cosk_ref) + proj[:, _C_KPES:_C_Z] * rows(sink_ref)
    c_ref[...] = c
    kr_ref[...] = kr[:, :QK_ROPE]
    kv_ref[:, :KV_LORA] = c.astype(_BF)
    kv_ref[:, KV_LORA:] = kr.astype(_BF)

    qn = _rms(proj[:, _C_QLAT:_C_KV], qn_ref[...]).astype(_BF)
    qq = _dot(qn, wuq_ref[...])
    q_pe = qq[:, _Q_ROPE:_Q_ROPES] * rows(cosq_ref) + qq[:, _Q_ROPES:_N_UQ_PAD] * rows(sinq_ref)
    lane = lax.broadcasted_iota(jnp.int32, (tm, LANES), 1)
    heads_per_group = LANES // QK_ROPE
    for hd in range(MLA_HEADS):
        q_abs = _dot(qq[:, hd * LANES:(hd + 1) * LANES].astype(_BF), wuk_ref[hd]) * SCALE_LOG2
        grp = q_pe[:, (hd // heads_per_group) * LANES:(hd // heads_per_group + 1) * LANES]
        shift = (LANES - QK_ROPE * (hd % heads_per_group)) % LANES
        if shift:
            grp = pltpu.roll(grp, shift, 1)
        pe = jnp.where(lane < QK_ROPE, grp * SCALE_LOG2, 0.0)
        q_ref[:, hd, :, :KV_LORA] = q_abs.astype(_BF).reshape(tb, tl, KV_LORA)
        q_ref[:, hd, :, KV_LORA:] = pe.astype(_BF).reshape(tb, tl, LANES)


def _in_proj(x2d, b, l, past_len, p, tb, tl):
    t = b * l
    tm = tb * tl
    n_l = l // tl
    cos32, sin32 = _rope_tables(l, past_len)
    padk = jnp.zeros((l, LANES - QK_ROPE), _F32)
    cosk = jnp.concatenate([cos32, padk], axis=1)
    sink = jnp.concatenate([sin32, padk], axis=1)
    cosq = jnp.tile(cos32, (1, MLA_HEADS))
    sinq = jnp.tile(sin32, (1, MLA_HEADS))

    row_spec = lambda w: pl.BlockSpec((tm, w), lambda i: (i, 0))
    tab_spec = lambda w: pl.BlockSpec((tl, w), lambda i: (i % n_l, 0))
    out_shape = (
        jax.ShapeDtypeStruct((b, MLA_HEADS, l, QK_PAD), _BF),
        jax.ShapeDtypeStruct((t, QK_PAD), _BF),
        jax.ShapeDtypeStruct((t, KV_LORA), _F32),
        jax.ShapeDtypeStruct((t, QK_ROPE), _F32),
        jax.ShapeDtypeStruct((t, D_SSM), _F32),
        jax.ShapeDtypeStruct((t, CONV_DIM), _F32),
        jax.ShapeDtypeStruct((t, LANES), _F32),
    )
    out_specs = (
        pl.BlockSpec((tb, MLA_HEADS, tl, QK_PAD), lambda i: (i // n_l, 0, i % n_l, 0)),
        row_spec(QK_PAD), row_spec(KV_LORA), row_spec(QK_ROPE), row_spec(D_SSM), row_spec(CONV_DIM), row_spec(LANES),
    )
    in_specs = [
        row_spec(D_MODEL), _const_spec((1, D_MODEL)), _const_spec((D_MODEL, _N_IN_PAD)),
        _const_spec((1, Q_LORA)), _const_spec((1, KV_LORA)), _const_spec((Q_LORA, _N_UQ_PAD)),
        _const_spec((MLA_HEADS, LANES, KV_LORA)),
        tab_spec(MLA_HEADS * QK_ROPE), tab_spec(MLA_HEADS * QK_ROPE), tab_spec(LANES), tab_spec(LANES),
    ]
    return pl.pallas_call(
        functools.partial(_in_proj_kernel, tb=tb, tl=tl),
        grid=(t // tm,), in_specs=in_specs, out_specs=out_specs, out_shape=out_shape,
        compiler_params=pltpu.CompilerParams(dimension_semantics=("arbitrary",), vmem_limit_bytes=VMEM_LIMIT),
        name="in_proj",
    )(x2d, p["g_attn"], p["w_in"], p["q_norm"], p["kv_norm"], p["w_uq"], p["w_uk"], cosq, sinq, cosk, sink)


def _in_proj_prompt_kernel(x_ref, g_ref, w_ref, qn_ref, kvn_ref, wuqt_ref, wk_ref, wuvt_ref, cost_ref, sint_ref,
                           cosk_ref, sink_ref, qt_ref, kh_ref, vt_ref, c_ref, kr_ref, z_ref, xbc_ref, dt_ref):
    h = _rms(x_ref[...], g_ref[...]).astype(_BF)
    proj = _dot(h, w_ref[...])
    z_ref[...] = proj[:, _C_Z:_C_XBC]
    xbc_ref[...] = proj[:, _C_XBC:_C_DT]
    dt_ref[...] = proj[:, _C_DT:_N_IN_PAD]

    c = _rms(proj[:, _C_KV:_C_KPE], kvn_ref[...])
    kr = proj[:, _C_KPE:_C_KPES] * cosk_ref[...] + proj[:, _C_KPES:_C_Z] * sink_ref[...]
    c_ref[...] = c
    kr_ref[...] = kr[:, QK_NOPE:QK_NOPE + QK_ROPE]
    c_bf = c.astype(_BF)
    kn = _dot(c_bf, wk_ref[...])
    for hd in range(MLA_HEADS):
        kh_ref[:, hd * LANES:(hd + 1) * LANES] = (kn[:, hd * LANES:(hd + 1) * LANES] + kr).astype(_BF)
    vt_ref[0] = _dot_nt(wuvt_ref[...], c_bf).astype(_BF)

    qn = _rms(proj[:, _C_QLAT:_C_KV], qn_ref[...]).astype(_BF)
    qqt = _dot_nt(wuqt_ref[...], qn)
    half = MLA_HEADS * LANES
    for hd in range(MLA_HEADS):
        qt = (qqt[hd * LANES:(hd + 1) * LANES] * cost_ref[...]
              + qqt[half + hd * LANES:half + (hd + 1) * LANES] * sint_ref[...])
        qt_ref[0, hd] = qt.astype(_BF)


def _in_proj_prompt(x2d, b, l, p, tl):
    t = b * l
    n_l = l // tl
    cos32, sin32 = _rope_tables(l, 0)
    lead = jnp.zeros((l, QK_NOPE), _F32)
    trail = jnp.zeros((l, LANES - QK_NOPE - QK_ROPE), _F32)
    cosk = jnp.concatenate([lead, cos32, trail], axis=1)
    sink = jnp.concatenate([lead, sin32, trail], axis=1)
    cost = (jnp.concatenate([lead + 1.0, cos32, trail], axis=1) * SCALE_LOG2).T
    sint = (sink * SCALE_LOG2).T

    row_spec = lambda w: pl.BlockSpec((tl, w), lambda i: (i, 0))
    out_shape = (
        jax.ShapeDtypeStruct((b, MLA_HEADS, LANES, l), _BF),
        jax.ShapeDtypeStruct((t, MLA_HEADS * LANES), _BF),
        jax.ShapeDtypeStruct((b, D_ATTN, l), _BF),
        jax.ShapeDtypeStruct((t, KV_LORA), _F32),
        jax.ShapeDtypeStruct((t, QK_ROPE), _F32),
        jax.ShapeDtypeStruct((t, D_SSM), _F32),
        jax.ShapeDtypeStruct((t, CONV_DIM), _F32),
        jax.ShapeDtypeStruct((t, LANES), _F32),
    )
    out_specs = (
        pl.BlockSpec((1, MLA_HEADS, LANES, tl), lambda i: (i // n_l, 0, 0, i % n_l)),
        row_spec(MLA_HEADS * LANES),
        pl.BlockSpec((1, D_ATTN, tl), lambda i: (i // n_l, 0, i % n_l)),
        row_spec(KV_LORA), row_spec(QK_ROPE), row_spec(D_SSM), row_spec(CONV_DIM), row_spec(LANES),
    )
    in_specs = [
        row_spec(D_MODEL), _const_spec((1, D_MODEL)), _const_spec((D_MODEL, _N_IN_PAD)),
        _const_spec((1, Q_LORA)), _const_spec((1, KV_LORA)), _const_spec((2 * MLA_HEADS * LANES, Q_LORA)),
        _const_spec((KV_LORA, MLA_HEADS * LANES)), _const_spec((D_ATTN, KV_LORA)),
        pl.BlockSpec((LANES, tl), lambda i: (0, i % n_l)), pl.BlockSpec((LANES, tl), lambda i: (0, i % n_l)),
        pl.BlockSpec((tl, LANES), lambda i: (i % n_l, 0)), pl.BlockSpec((tl, LANES), lambda i: (i % n_l, 0)),
    ]
    return pl.pallas_call(
        _in_proj_prompt_kernel,
        grid=(t // tl,), in_specs=in_specs, out_specs=out_specs, out_shape=out_shape,
        compiler_params=pltpu.CompilerParams(dimension_semantics=("arbitrary",), vmem_limit_bytes=VMEM_LIMIT),
        name="in_proj_prompt",
    )(x2d, p["g_attn"], p["w_in_prompt"], p["q_norm"], p["kv_norm"], p["w_uq_t"], p["w_k"], p["w_uv_t"],
      cost, sint, cosk, sink)


def _attn_prompt_kernel(it_ref, jt_ref, qt_ref, kh_ref, vt_ref, o_ref, m_ref, l_ref, acc_ref, *, tq):
    s_idx = pl.program_id(1)
    i = it_ref[s_idx]
    j = jt_ref[s_idx]

    @pl.when(j == 0)
    def _():
        m_ref[...] = jnp.full(m_ref.shape, -jnp.inf, _F32)
        l_ref[...] = jnp.zeros(l_ref.shape, _F32)
        acc_ref[...] = jnp.zeros(acc_ref.shape, _F32)

    def step(masked):
        if masked:
            krow = lax.broadcasted_iota(jnp.int32, (tq, tq), 0)
            qcol = lax.broadcasted_iota(jnp.int32, (tq, tq), 1)
            keep = krow <= qcol

        def scores(hd):
            st = _dot(kh_ref[:, hd * LANES:(hd + 1) * LANES], qt_ref[0, hd])
            if masked:
                st = jnp.where(keep, st, -jnp.inf)
            return st.reshape(tq // SUBLANES, SUBLANES, tq)

        ahead = {hd: scores(hd) for hd in range(QK_AHEAD)}
        for hd in range(MLA_HEADS):
            if hd + QK_AHEAD < MLA_HEADS:
                ahead[hd + QK_AHEAD] = scores(hd + QK_AHEAD)
            st = ahead.pop(hd)
            m_prev = m_ref[hd]
            m_new = jnp.maximum(m_prev, _sublane_all(jnp.maximum, jnp.max(st, axis=0)))
            alpha = jnp.exp2(m_prev - m_new)
            pr = jnp.exp2(st - m_new[None])
            l_ref[hd] = alpha * l_ref[hd] + jnp.sum(pr, axis=0)
            rows = slice(hd * V_HEAD, (hd + 1) * V_HEAD)
            acc = acc_ref[rows, :].reshape(V_HEAD // SUBLANES, SUBLANES, tq) * alpha[None]
            acc_ref[rows, :] = acc.reshape(V_HEAD, tq) + _dot(vt_ref[0, rows, :], pr.reshape(tq, tq).astype(_BF))
            m_ref[hd] = m_new

    @pl.when(j < i)
    def _():
        step(False)

    @pl.when(j == i)
    def _():
        step(True)
        for hd in range(MLA_HEADS):
            rows = slice(hd * V_HEAD, (hd + 1) * V_HEAD)
            den = _sublane_all(jnp.add, l_ref[hd])
            acc = acc_ref[rows, :].reshape(V_HEAD // SUBLANES, SUBLANES, tq) / den[None]
            acc_ref[rows, :] = acc.reshape(V_HEAD, tq)
        o_ref[...] = acc_ref[...].T.astype(o_ref.dtype)


def _attn_prompt(qt, kh, vt, b, l, tq):
    nq = l // tq
    pairs = [(i, j) for i in range(nq) for j in range(i + 1)]
    it = jnp.asarray([pr[0] for pr in pairs], jnp.int32)
    jt = jnp.asarray([pr[1] for pr in pairs], jnp.int32)
    grid_spec = pltpu.PrefetchScalarGridSpec(
        num_scalar_prefetch=2,
        grid=(b, len(pairs)),
        in_specs=[
            pl.BlockSpec((1, MLA_HEADS, LANES, tq), lambda bi, s, it, jt: (bi, 0, 0, it[s])),
            pl.BlockSpec((tq, MLA_HEADS * LANES), lambda bi, s, it, jt: (bi * nq + jt[s], 0)),
            pl.BlockSpec((1, D_ATTN, tq), lambda bi, s, it, jt: (bi, 0, jt[s])),
        ],
        out_specs=pl.BlockSpec((tq, D_ATTN), lambda bi, s, it, jt: (bi * nq + it[s], 0)),
        scratch_shapes=[pltpu.VMEM((MLA_HEADS, SUBLANES, tq), _F32), pltpu.VMEM((MLA_HEADS, SUBLANES, tq), _F32),
                        pltpu.VMEM((D_ATTN, tq), _F32)],
    )
    return pl.pallas_call(
        functools.partial(_attn_prompt_kernel, tq=tq),
        grid_spec=grid_spec,
        out_shape=jax.ShapeDtypeStruct((b * l, D_ATTN), _BF),
        compiler_params=pltpu.CompilerParams(dimension_semantics=("arbitrary", "arbitrary"),
                                             vmem_limit_bytes=VMEM_LIMIT),
        name="attn_prompt",
    )(it, jt, qt, kh, vt)


def _attn_sample_kernel(pt_ref, q_ref, kvn_ref, cache_c_ref, cache_krt_ref, o_ref, cbuf_ref, krbuf_ref, kc_ref,
                        sem_ref, *, n_pages, page, l_new, n_split):
    b = pl.program_id(0)
    nb = pl.num_programs(0)
    slot = b % 2

    def page_copies(item, slot_, pg):
        pid = pt_ref[item, pg]
        dst_rows = pl.ds(pl.multiple_of(pg * page, page), page)
        return (pltpu.make_async_copy(cache_c_ref.at[pid], cbuf_ref.at[slot_, dst_rows, :], sem_ref.at[slot_, 0]),
                pltpu.make_async_copy(cache_krt_ref.at[pid], krbuf_ref.at[slot_, :, dst_rows], sem_ref.at[slot_, 1]))

    def start_page(item, slot_, pg):
        c_copy, kr_copy = page_copies(item, slot_, pg)
        c_copy.start()
        kr_copy.start(priority=1)

    def start_item(item, slot_):
        def body(pg, carry):
            start_page(item, slot_, pg)
            return carry
        lax.fori_loop(0, n_pages, body, 0)

    def wait_item(item, slot_):
        def body(pg, carry):
            for cp in page_copies(item, slot_, pg):
                cp.wait()
            return carry
        lax.fori_loop(0, n_pages, body, 0)

    @pl.when(b == 0)
    def _():
        start_item(0, 0)

    wait_item(b, slot)
    nxt = jnp.minimum(b + 1, nb - 1)
    for pg in range(n_pages):
        start_page(nxt, 1 - slot, pg)

    m_rows = MLA_HEADS * l_new
    q = q_ref[0]
    q_abs = q[:, :KV_LORA]
    q_pe = q[:, KV_LORA:KV_LORA + QK_ROPE]
    ks = (n_pages // n_split) * page
    def scores(sp):
        rows = pl.ds(sp * ks, ks)
        kc_ref[rows, :] = cbuf_ref[slot, rows, :].astype(_BF)
        return _dot_nt(q_abs, kc_ref[rows, :]) + _dot(q_pe, krbuf_ref[slot, :, rows].astype(_BF))

    partials = []
    s_next = scores(0)
    for sp in range(n_split):
        s = s_next
        if sp + 1 < n_split:
            s_next = scores(sp + 1)
        m = jnp.max(s, axis=1, keepdims=True)
        pr = jnp.exp2(s - m)
        partials.append((m, jnp.sum(pr, axis=1, keepdims=True),
                         _dot(pr.astype(_BF), kc_ref[pl.ds(sp * ks, ks), :])))
    kvn = kvn_ref[0]
    sn = _dot_nt(q, kvn)
    qpos = lax.broadcasted_iota(jnp.int32, (MLA_HEADS, l_new, l_new), 1).reshape(m_rows, l_new)
    kpos = lax.broadcasted_iota(jnp.int32, (m_rows, l_new), 1)
    sn = jnp.where(kpos <= qpos, sn, -jnp.inf)
    m = jnp.max(sn, axis=1, keepdims=True)
    pn = jnp.exp2(sn - m)
    partials.append((m, jnp.sum(pn, axis=1, keepdims=True), _dot(pn.astype(_BF), kvn[:, :KV_LORA])))
    m_all = functools.reduce(jnp.maximum, [pt[0] for pt in partials])
    den = sum(pt[1] * jnp.exp2(pt[0] - m_all) for pt in partials)
    o = sum(pt[2] * jnp.exp2(pt[0] - m_all) for pt in partials)
    o_ref[0] = (o / den).astype(o_ref.dtype)

    @pl.when(b == nb - 1)
    def _():
        wait_item(nxt, 1 - slot)


def _attn_sample(q, kvn, cache_c, cache_krt, page_table):
    b, m_rows, _ = q.shape
    l_new = m_rows // MLA_HEADS
    n_pages = page_table.shape[1]
    page = cache_c.shape[1]
    keys = n_pages * page
    grid_spec = pltpu.PrefetchScalarGridSpec(
        num_scalar_prefetch=1,
        grid=(b,),
        in_specs=[pl.BlockSpec((1, m_rows, QK_PAD), lambda bi, pt: (bi, 0, 0)),
                  pl.BlockSpec((1, l_new, QK_PAD), lambda bi, pt: (bi, 0, 0)),
                  pl.BlockSpec(memory_space=pl.ANY), pl.BlockSpec(memory_space=pl.ANY)],
        out_specs=pl.BlockSpec((1, m_rows, KV_LORA), lambda bi, pt: (bi, 0, 0)),
        scratch_shapes=[pltpu.VMEM((2, keys, KV_LORA), _F32), pltpu.VMEM((2, QK_ROPE, keys), _F32),
                        pltpu.VMEM((keys, KV_LORA), _BF), pltpu.SemaphoreType.DMA((2, 2))],
    )
    return pl.pallas_call(
        functools.partial(_attn_sample_kernel, n_pages=n_pages, page=page, l_new=l_new,
                          n_split=_largest_divisor(n_pages, 4)),
        grid_spec=grid_spec,
        out_shape=jax.ShapeDtypeStruct((b, m_rows, KV_LORA), _BF),
        compiler_params=pltpu.CompilerParams(dimension_semantics=("arbitrary",), vmem_limit_bytes=VMEM_LIMIT),
        name="attn_sample",
    )(page_table, q, kvn, cache_c, cache_krt)


def _uv_proj_kernel(o_ref, wuv_ref, out_ref, *, tb, l_new):
    for hd in range(MLA_HEADS):
        o = o_ref[:, hd].reshape(tb * l_new, KV_LORA)
        out_ref[:, hd * V_HEAD:(hd + 1) * V_HEAD] = _dot(o, wuv_ref[hd]).astype(out_ref.dtype)


def _uv_proj(o_lat, wuv, tb):
    b, _, l_new, _ = o_lat.shape
    return pl.pallas_call(
        functools.partial(_uv_proj_kernel, tb=tb, l_new=l_new),
        grid=(b // tb,),
        in_specs=[pl.BlockSpec((tb, MLA_HEADS, l_new, KV_LORA), lambda i: (i, 0, 0, 0)),
                  _const_spec((MLA_HEADS, KV_LORA, V_HEAD))],
        out_specs=pl.BlockSpec((tb * l_new, D_ATTN), lambda i: (i, 0)),
        out_shape=jax.ShapeDtypeStruct((b * l_new, D_ATTN), _BF),
        compiler_params=pltpu.CompilerParams(dimension_semantics=("arbitrary",), vmem_limit_bytes=VMEM_LIMIT),
        name="uv_proj",
    )(o_lat, wuv)


_XP_OFF = 8


def _ssd_kernel(xbc_ref, z_ref, dtr_ref, cbuf_ref, h0_ref, cw_ref, cb_ref, dtb_ref, alog_ref, dskip_ref, nrm_ref,
                sel_ref, ex_ref, exw_ref, o_ref, h_ref, cnew_ref, xp_ref, y_ref, *, g_items, q):
    c_idx = pl.program_id(1)
    last = pl.num_programs(1) - 1
    heads_per_group = SSM_HEADS // SSM_GROUPS
    rows = g_items * q
    row = lax.broadcasted_iota(jnp.int32, (q, q), 0)
    col = lax.broadcasted_iota(jnp.int32, (q, q), 1)
    causal = row >= col

    def conv_item(g):
        @pl.when(c_idx == 0)
        def _():
            xp_ref[g, _XP_OFF - (CONV_K - 1):_XP_OFF, :] = cbuf_ref[g]
            h_ref[g] = h0_ref[g]

        xp_ref[g, _XP_OFF:_XP_OFF + q, :] = xbc_ref[g * q:(g + 1) * q, :]
        conv = cb_ref[...]
        for k in range(CONV_K):
            conv = conv + xp_ref[g, _XP_OFF - (CONV_K - 1) + k:_XP_OFF - (CONV_K - 1) + k + q, :] * cw_ref[k:k + 1, :]
        conv = _silu(conv)
        tail = xp_ref[g, _XP_OFF + q - (CONV_K - 1):_XP_OFF + q, :]
        xp_ref[g, _XP_OFF - (CONV_K - 1):_XP_OFF, :] = tail

        @pl.when(c_idx == last)
        def _():
            cnew_ref[g] = tail

        return (conv[:, :D_SSM], conv[:, D_SSM:D_SSM + SSM_GROUPS * D_STATE].astype(_BF),
                conv[:, D_SSM + SSM_GROUPS * D_STATE:].astype(_BF))

    convs = {0: conv_item(0)} if g_items == 1 else {}

    if g_items == 1:
        tril = jnp.where(causal, 1.0, 0.0).astype(_BF)
    else:
        r_all = lax.broadcasted_iota(jnp.int32, (rows, rows), 0)
        c_all = lax.broadcasted_iota(jnp.int32, (rows, rows), 1)
        same_item = (r_all // q) == (c_all // q)
        tril = jnp.where(r_all >= c_all, jnp.where(same_item, 1.0, 0.0), 0.0).astype(_BF)
    a_neg = -jnp.exp(alog_ref[...])
    dtv = dtr_ref[...] + dtb_ref[...]
    dt = jnp.maximum(dtv, 0.0) + jnp.log1p(jnp.exp(-jnp.abs(dtv)))
    cs = _dot(tril, _split3(dt * a_neg))
    acs = cs[:, :LANES] + cs[:, LANES:2 * LANES] + cs[:, 2 * LANES:]
    acs3 = _split3(acs)
    acs_t = _dot_nt(sel_ref[...], acs3)
    dt_x = _dot(_split3(dt), ex_ref[...])
    acs_x = _dot(acs3, ex_ref[...])
    acs_w = _dot(acs3, exw_ref[...])
    e_acs = jnp.exp(acs_x)

    for g in range(g_items):
        rs = slice(g * q, (g + 1) * q)
        xs, bmat, cmat = convs[g] if g in convs else conv_item(g)
        acs_last_x = acs_x[(g + 1) * q - 1:(g + 1) * q, :]
        x_dt = xs * dt_x[rs]
        xw = (x_dt * jnp.exp(acs_last_x - acs_x[rs])).astype(_BF)
        x_dt = x_dt.astype(_BF)
        for grp in range(SSM_GROUPS):
            b_g = bmat[:, grp * D_STATE:(grp + 1) * D_STATE]
            c_g = cmat[:, grp * D_STATE:(grp + 1) * D_STATE]
            cb = _dot_nt(c_g, b_g)
            for hd in range(grp * heads_per_group, (grp + 1) * heads_per_group):
                cols = slice(hd * SSM_HEADDIM, (hd + 1) * SSM_HEADDIM)
                acs_h = acs_w[rs, hd * LANES:(hd + 1) * LANES]
                acs_h = acs_h[:, :q] if q <= LANES else jnp.concatenate([acs_h] * (q // LANES), axis=1)
                lmat = jnp.exp(jnp.where(causal, acs_h - acs_t[hd:hd + 1, rs], -jnp.inf))
                h_prev = h_ref[g, hd]
                y = _dot((cb * lmat).astype(_BF), x_dt[:, cols])
                y = y + _dot_nt(c_g, h_prev.astype(_BF)) * e_acs[rs, cols]
                decay = jnp.exp(acs[(g + 1) * q - 1:(g + 1) * q, hd:hd + 1])
                h_ref[g, hd] = decay * h_prev + _dot_tn(xw[:, cols], b_g)
                y_ref[:, cols] = y
        yv = y_ref[...] + dskip_ref[...] * xs
        gated = yv * _silu(z_ref[g * q:(g + 1) * q, :])
        gw = D_SSM // SSM_GROUPS
        parts = []
        for grp in range(SSM_GROUPS):
            gg = gated[:, grp * gw:(grp + 1) * gw]
            parts.append(gg * lax.rsqrt(jnp.mean(gg * gg, axis=-1, keepdims=True) + EPS))
        o_ref[g * q:(g + 1) * q, :] = (jnp.concatenate(parts, axis=1) * nrm_ref[...]).astype(o_ref.dtype)


def _ssd(xbc, z, dtr, conv_buf, h0, p, b, l, q, g_items):
    nc = l // q
    rows = g_items * q
    row_spec = lambda w: pl.BlockSpec((rows, w), lambda bi, c: (bi * nc + c, 0))
    in_specs = [
        row_spec(CONV_DIM), row_spec(D_SSM), row_spec(LANES),
        pl.BlockSpec((g_items, CONV_K - 1, CONV_DIM), lambda bi, c: (bi, 0, 0)),
        pl.BlockSpec((g_items, SSM_HEADS, SSM_HEADDIM, D_STATE), lambda bi, c: (bi, 0, 0, 0)),
        _const_spec((CONV_K, CONV_DIM)), _const_spec((1, CONV_DIM)), _const_spec((1, LANES)), _const_spec((1, LANES)),
        _const_spec((1, D_SSM)), _const_spec((1, D_SSM)), _const_spec((SSM_HEADS, 3 * LANES)),
        _const_spec((3 * LANES, D_SSM)), _const_spec((3 * LANES, SSM_HEADS * LANES)),
    ]
    out_specs = (
        row_spec(D_SSM),
        pl.BlockSpec((g_items, SSM_HEADS, SSM_HEADDIM, D_STATE), lambda bi, c: (bi, 0, 0, 0)),
        pl.BlockSpec((g_items, CONV_K - 1, CONV_DIM), lambda bi, c: (bi, 0, 0)),
    )
    out_shape = (
        jax.ShapeDtypeStruct((b * l, D_SSM), _BF),
        jax.ShapeDtypeStruct((b, SSM_HEADS, SSM_HEADDIM, D_STATE), _F32),
        jax.ShapeDtypeStruct((b, CONV_K - 1, CONV_DIM), _F32),
    )
    return pl.pallas_call(
        functools.partial(_ssd_kernel, g_items=g_items, q=q),
        grid=(b // g_items, nc), in_specs=in_specs, out_specs=out_specs, out_shape=out_shape,
        scratch_shapes=[pltpu.VMEM((g_items, _XP_OFF + q, CONV_DIM), _F32), pltpu.VMEM((q, D_SSM), _F32)],
        compiler_params=pltpu.CompilerParams(dimension_semantics=("arbitrary", "arbitrary"),
                                             vmem_limit_bytes=VMEM_LIMIT),
        name="ssd",
    )(xbc, z, dtr, conv_buf, h0, p["conv_w"], p["conv_b"], p["dt_bias"], p["a_log"], p["d_skip"], p["ssm_norm"],
      p["sel"], p["head_to_cols"], p["head_to_lanes"])


def _out_mlp_kernel(x_ref, oa_ref, os_ref, wo_ref, gm_ref, wu_ref, wd_ref, gf_ref, y_ref, *, f_chunk):
    mix = jnp.concatenate([oa_ref[...], os_ref[...]], axis=1)
    x1 = x_ref[...] + _dot(mix, wo_ref[...])
    h2 = _rms(x1, gm_ref[...]).astype(_BF)
    acc = x1
    for f0 in range(0, D_FF, f_chunk):
        u = jnp.maximum(_dot(h2, wu_ref[:, f0:f0 + f_chunk]), 0.0)
        acc = acc + _dot((u * u).astype(_BF), wd_ref[f0:f0 + f_chunk, :])
    y_ref[...] = _rms(acc, gf_ref[...])


def _out_mlp(x2d, oa, osm, p, tm):
    t = x2d.shape[0]
    row_spec = lambda w: pl.BlockSpec((tm, w), lambda i: (i, 0))
    once = lambda shape: pl.BlockSpec(shape, lambda i: (0,) * len(shape), pipeline_mode=pl.Buffered(1))
    return pl.pallas_call(
        functools.partial(_out_mlp_kernel, f_chunk=1024),
        grid=(t // tm,),
        in_specs=[row_spec(D_MODEL), row_spec(D_ATTN), row_spec(D_SSM), once((D_MODEL, D_MODEL)),
                  once((1, D_MODEL)), once((D_MODEL, D_FF)), once((D_FF, D_MODEL)), once((1, D_MODEL))],
        out_specs=row_spec(D_MODEL),
        out_shape=jax.ShapeDtypeStruct((t, D_MODEL), _F32),
        compiler_params=pltpu.CompilerParams(dimension_semantics=("arbitrary",), vmem_limit_bytes=VMEM_LIMIT),
        name="out_mlp",
    )(x2d, oa, osm, p["w_out"], p["g_mlp"], p["w_up"], p["w_down"], p["g_final"])


def _prep_params(norm_attn, w_in, q_norm, kv_norm, w_uq, w_uk, w_uv, conv_w, conv_b, dt_bias, a_log, d_skip,
                 ssm_norm, w_out, norm_mlp, w_up, w_down, norm_final):
    half = QK_ROPE // 2

    def swap(w):
        return jnp.concatenate([w[..., half:], w[..., :half]], axis=-1)

    def place(w, lead, n):
        return jnp.pad(w, ((0, 0), (lead, n - lead - w.shape[1])))

    o1 = Q_LORA
    o2 = o1 + KV_LORA
    o3 = o2 + QK_ROPE
    o5 = o3 + D_SSM + CONV_DIM
    w_kpe = w_in[:, o2:o3]

    def w_in_padded(lead):
        return jnp.concatenate([
            w_in[:, :o2], place(w_kpe, lead, LANES), place(swap(w_kpe), lead, LANES), w_in[:, o3:o5],
            place(w_in[:, o5:], 0, LANES)], axis=1).astype(_BF)

    uq = w_uq.reshape(Q_LORA, MLA_HEADS, QK_NOPE + QK_ROPE)
    uq_nope = uq[:, :, :QK_NOPE]
    uq_rope = uq[:, :, QK_NOPE:]
    w_uq_p = jnp.concatenate([
        jnp.pad(uq_nope, ((0, 0), (0, 0), (0, LANES - QK_NOPE))).reshape(Q_LORA, MLA_HEADS * LANES),
        uq_rope.reshape(Q_LORA, -1), swap(uq_rope).reshape(Q_LORA, -1)], axis=1).astype(_BF)
    w_uk_p = jnp.pad(w_uk, ((0, 0), (0, LANES - QK_NOPE), (0, 0))).astype(_BF)
    tail = LANES - QK_NOPE - QK_ROPE
    plain = jnp.pad(uq, ((0, 0), (0, 0), (0, tail)))
    swapped = jnp.pad(swap(uq_rope), ((0, 0), (0, 0), (QK_NOPE, tail)))
    w_uq_t = jnp.concatenate([plain.reshape(Q_LORA, -1), swapped.reshape(Q_LORA, -1)], axis=1).T.astype(_BF)
    w_k = jnp.pad(jnp.transpose(w_uk, (2, 0, 1)), ((0, 0), (0, 0), (0, LANES - QK_NOPE)))
    w_k = w_k.reshape(KV_LORA, MLA_HEADS * LANES).astype(_BF)
    w_uv_t = jnp.transpose(w_uv, (0, 2, 1)).reshape(D_ATTN, KV_LORA).astype(_BF)
    pad_heads = lambda v: jnp.pad(v.reshape(1, SSM_HEADS), ((0, 0), (0, LANES - SSM_HEADS)))
    return dict(
        g_attn=norm_attn.reshape(1, D_MODEL), w_in=w_in_padded(0), w_in_prompt=w_in_padded(QK_NOPE),
        q_norm=q_norm.reshape(1, Q_LORA), kv_norm=kv_norm.reshape(1, KV_LORA), w_uq=w_uq_p, w_uk=w_uk_p,
        w_uq_t=w_uq_t, w_k=w_k, w_uv_t=w_uv_t, w_uv=w_uv.astype(_BF),
        conv_w=conv_w, conv_b=conv_b.reshape(1, CONV_DIM), dt_bias=pad_heads(dt_bias), a_log=pad_heads(a_log),
        d_skip=jnp.repeat(d_skip, SSM_HEADDIM).reshape(1, D_SSM), ssm_norm=ssm_norm.reshape(1, D_SSM),
        sel=jnp.tile(jnp.eye(SSM_HEADS, LANES, dtype=_BF), (1, 3)),
        head_to_cols=jnp.tile(jnp.repeat(jnp.eye(LANES, SSM_HEADS, dtype=_BF), SSM_HEADDIM, axis=1), (3, 1)),
        head_to_lanes=jnp.tile(jnp.repeat(jnp.eye(LANES, SSM_HEADS, dtype=_BF), LANES, axis=1), (3, 1)),
        w_out=w_out.astype(_BF), g_mlp=norm_mlp.reshape(1, D_MODEL), w_up=w_up.astype(_BF),
        w_down=w_down.astype(_BF), g_final=norm_final.reshape(1, D_MODEL),
    )


def _largest_divisor(n, cap):
    d = min(n, cap)
    while n % d:
        d -= 1
    return d


def kernel(x_prompt, x_sample, cache_kv_latent, cache_k_rope, state_ssm, state_conv, page_table, norm_attn, w_in, q_norm, kv_norm, w_uq, w_uk, w_uv, conv_w, conv_b, dt_bias, a_log, d_skip, ssm_norm, w_out, norm_mlp, w_up, w_down, norm_final):
    assert norm_attn.shape[0] == 1, "single-layer model"
    p = _prep_params(norm_attn[0], w_in[0], q_norm[0], kv_norm[0], w_uq[0], w_uk[0], w_uv[0], conv_w[0], conv_b[0],
                     dt_bias[0], a_log[0], d_skip[0], ssm_norm[0], w_out[0], norm_mlp[0], w_up[0], w_down[0],
                     norm_final)
    bp, lp, _ = x_prompt.shape
    bs, ls, _ = x_sample.shape
    n_pages = page_table.shape[1]
    page = cache_kv_latent.shape[2]
    past_len = n_pages * page

    xp2 = x_prompt.reshape(bp * lp, D_MODEL)
    qt, kh, vt, c_new, kr_new, z, xbc, dtr = _in_proj_prompt(xp2, bp, lp, p, _largest_divisor(lp, 512))
    o_attn = _attn_prompt(qt, kh, vt, bp, lp, _largest_divisor(lp, 512))
    chunk = _largest_divisor(lp, CHUNK)
    o_ssm, h_p, conv_p = _ssd(xbc, z, dtr, jnp.zeros((bp, CONV_K - 1, CONV_DIM), _F32),
                              jnp.zeros((bp, SSM_HEADS, SSM_HEADDIM, D_STATE), _F32), p, bp, lp, chunk, 1)
    y_p = _out_mlp(xp2, o_attn, o_ssm, p, _largest_divisor(bp * lp, 512))
    outs_p = (y_p.reshape(bp, lp, D_MODEL), c_new.reshape(1, bp, lp, KV_LORA), kr_new.reshape(1, bp, lp, QK_ROPE),
              h_p[None], conv_p[None])

    xs2 = x_sample.reshape(bs * ls, D_MODEL)
    tb = _largest_divisor(bs, 512 // ls)
    q, kv, c_new, kr_new, z, xbc, dtr = _in_proj(xs2, bs, ls, past_len, p, tb, ls)
    cache_krt = jnp.swapaxes(cache_k_rope[0], 1, 2)
    o_lat = _attn_sample(q.reshape(bs, MLA_HEADS * ls, QK_PAD), kv.reshape(bs, ls, QK_PAD), cache_kv_latent[0],
                         cache_krt, page_table)
    o_attn = _uv_proj(o_lat.reshape(bs, MLA_HEADS, ls, KV_LORA), p["w_uv"], tb)
    g_items = _largest_divisor(bs, 8)
    o_ssm, h_s, conv_s = _ssd(xbc, z, dtr, state_conv[0], state_ssm[0], p, bs, ls, ls, g_items)
    y_s = _out_mlp(xs2, o_attn, o_ssm, p, _largest_divisor(bs * ls, 512))
    outs_s = (y_s.reshape(bs, ls, D_MODEL), c_new.reshape(1, bs, ls, KV_LORA), kr_new.reshape(1, bs, ls, QK_ROPE),
              h_s[None], conv_s[None])

    return (outs_p[0], outs_s[0], outs_p[1], outs_p[2], outs_p[3], outs_p[4],
            outs_s[1], outs_s[2], outs_s[3], outs_s[4])
```

```python
import functools

import jax
import jax.numpy as jnp
from jax import lax
from jax.experimental import pallas as pl
from jax.experimental.pallas import tpu as pltpu

D_MODEL = 1024
D_ATTN = 512
D_SSM = 512
V_HEAD = 64
MLA_HEADS = 8
QK_NOPE = 64
QK_ROPE = 32
KV_LORA = 256
Q_LORA = 384
ROPE_THETA = 10000.0
SSM_HEADDIM = 64
SSM_HEADS = 8
SSM_GROUPS = 2
D_STATE = 128
CONV_K = 4
CHUNK = 256
CONV_DIM = D_SSM + 2 * SSM_GROUPS * D_STATE
D_FF = 4096
EPS = 1e-6

LANES = 128
SUBLANES = 8
QK_AHEAD = 2
SPLITS_AHEAD = 4
QK_PAD = KV_LORA + LANES
VMEM_LIMIT = 56 * 1024 * 1024
SCALE_LOG2 = (QK_NOPE + QK_ROPE) ** -0.5 * 1.4426950408889634

_C_QLAT = 0
_C_KV = _C_QLAT + Q_LORA
_C_KPE = _C_KV + KV_LORA
_C_KPES = _C_KPE + LANES
_C_Z = _C_KPES + LANES
_C_XBC = _C_Z + D_SSM
_C_DT = _C_XBC + CONV_DIM
_N_IN_PAD = _C_DT + LANES
_Q_NOPE = 0
_Q_ROPE = MLA_HEADS * LANES
_Q_ROPES = _Q_ROPE + MLA_HEADS * QK_ROPE
_N_UQ_PAD = _Q_ROPES + MLA_HEADS * QK_ROPE

_BF = jnp.bfloat16
_F32 = jnp.float32


def _dot(a, b):
    return jnp.dot(a, b, preferred_element_type=_F32)


def _dot_nt(a, b):
    return lax.dot_general(a, b, (((1,), (1,)), ((), ())), preferred_element_type=_F32)


def _dot_tn(a, b):
    return lax.dot_general(a, b, (((0,), (0,)), ((), ())), preferred_element_type=_F32)


def _split3(a):
    hi = a.astype(_BF)
    r1 = a - hi.astype(_F32)
    mid = r1.astype(_BF)
    lo = (r1 - mid.astype(_F32)).astype(_BF)
    return jnp.concatenate([hi, mid, lo], axis=1)


def _rms(x, g):
    return x * lax.rsqrt(jnp.mean(x * x, axis=-1, keepdims=True) + EPS) * g


def _silu(x):
    return x * (1.0 / (1.0 + jnp.exp(-x)))


def _sublane_all(op, x):
    for shift in (4, 2, 1):
        x = op(x, pltpu.roll(x, shift, 0))
    return x


def _const_spec(shape):
    nd = len(shape)
    return pl.BlockSpec(shape, lambda *_: (0,) * nd)


def _rope_tables(l, past_len):
    pos = (past_len + jnp.arange(l, dtype=jnp.int32)).astype(_F32)
    inv = ROPE_THETA ** (-(jnp.arange(0, QK_ROPE, 2, dtype=_F32) / QK_ROPE))
    ang = pos[:, None] * inv[None, :]
    cos, sin = jnp.cos(ang), jnp.sin(ang)
    return jnp.concatenate([cos, cos], axis=1), jnp.concatenate([-sin, sin], axis=1)


def _in_proj_kernel(x_ref, g_ref, w_ref, qn_ref, kvn_ref, wuq_ref, wuk_ref, cosq_ref, sinq_ref, cosk_ref, sink_ref,
                    q_ref, kv_ref, c_ref, kr_ref, z_ref, xbc_ref, dt_ref, *, tb, tl):
    tm = tb * tl

    def rows(tab_ref):
        t = tab_ref[...]
        if tb == 1:
            return t
        return jnp.broadcast_to(t[None], (tb, tl, t.shape[-1])).reshape(tm, t.shape[-1])

    h = _rms(x_ref[...], g_ref[...]).astype(_BF)
    proj = _dot(h, w_ref[...])
    z_ref[...] = proj[:, _C_Z:_C_XBC]
    xbc_ref[...] = proj[:, _C_XBC:_C_DT]
    dt_ref[...] = proj[:, _C_DT:_N_IN_PAD]

    c = _rms(proj[:, _C_KV:_C_KPE], kvn_ref[...])
    kr = proj[:, _C_KPE:_C_KPES] * rows(cosk_ref) + proj[:, _C_KPES:_C_Z] * rows(sink_ref)
    c_ref[...] = c
    kr_ref[...] = kr[:, :QK_ROPE]
    kv_ref[:, :KV_LORA] = c.astype(_BF)
    kv_ref[:, KV_LORA:] = kr.astype(_BF)

    qn = _rms(proj[:, _C_QLAT:_C_KV], qn_ref[...]).astype(_BF)
    qq = _dot(qn, wuq_ref[...])
    q_pe = qq[:, _Q_ROPE:_Q_ROPES] * rows(cosq_ref) + qq[:, _Q_ROPES:_N_UQ_PAD] * rows(sinq_ref)
    lane = lax.broadcasted_iota(jnp.int32, (tm, LANES), 1)
    heads_per_group = LANES // QK_ROPE
    for hd in range(MLA_HEADS):
        q_abs = _dot(qq[:, hd * LANES:(hd + 1) * LANES].astype(_BF), wuk_ref[hd]) * SCALE_LOG2
        grp = q_pe[:, (hd // heads_per_group) * LANES:(hd // heads_per_group + 1) * LANES]
        shift = (LANES - QK_ROPE * (hd % heads_per_group)) % LANES
        if shift:
            grp = pltpu.roll(grp, shift, 1)
        pe = jnp.where(lane < QK_ROPE, grp * SCALE_LOG2, 0.0)
        q_ref[:, hd, :, :KV_LORA] = q_abs.astype(_BF).reshape(tb, tl, KV_LORA)
        q_ref[:, hd, :, KV_LORA:] = pe.astype(_BF).reshape(tb, tl, LANES)


def _in_proj(x2d, b, l, past_len, p, tb, tl):
    t = b * l
    tm = tb * tl
    n_l = l // tl
    cos32, sin32 = _rope_tables(l, past_len)
    padk = jnp.zeros((l, LANES - QK_ROPE), _F32)
    cosk = jnp.concatenate([cos32, padk], axis=1)
    sink = jnp.concatenate([sin32, padk], axis=1)
    cosq = jnp.tile(cos32, (1, MLA_HEADS))
    sinq = jnp.tile(sin32, (1, MLA_HEADS))

    row_spec = lambda w: pl.BlockSpec((tm, w), lambda i: (i, 0))
    tab_spec = lambda w: pl.BlockSpec((tl, w), lambda i: (i % n_l, 0))
    out_shape = (
        jax.ShapeDtypeStruct((b, MLA_HEADS, l, QK_PAD), _BF),
        jax.ShapeDtypeStruct((t, QK_PAD), _BF),
        jax.ShapeDtypeStruct((t, KV_LORA), _F32),
        jax.ShapeDtypeStruct((t, QK_ROPE), _F32),
        jax.ShapeDtypeStruct((t, D_SSM), _F32),
        jax.ShapeDtypeStruct((t, CONV_DIM), _F32),
        jax.ShapeDtypeStruct((t, LANES), _F32),
    )
    out_specs = (
        pl.BlockSpec((tb, MLA_HEADS, tl, QK_PAD), lambda i: (i // n_l, 0, i % n_l, 0)),
        row_spec(QK_PAD), row_spec(KV_LORA), row_spec(QK_ROPE), row_spec(D_SSM), row_spec(CONV_DIM), row_spec(LANES),
    )
    in_specs = [
        row_spec(D_MODEL), _const_spec((1, D_MODEL)), _const_spec((D_MODEL, _N_IN_PAD)),
        _const_spec((1, Q_LORA)), _const_spec((1, KV_LORA)), _const_spec((Q_LORA, _N_UQ_PAD)),
        _const_spec((MLA_HEADS, LANES, KV_LORA)),
        tab_spec(MLA_HEADS * QK_ROPE), tab_spec(MLA_HEADS * QK_ROPE), tab_spec(LANES), tab_spec(LANES),
    ]
    return pl.pallas_call(
        functools.partial(_in_proj_kernel, tb=tb, tl=tl),
        grid=(t // tm,), in_specs=in_specs, out_specs=out_specs, out_shape=out_shape,
        compiler_params=pltpu.CompilerParams(dimension_semantics=("arbitrary",), vmem_limit_bytes=VMEM_LIMIT),
        name="in_proj",
    )(x2d, p["g_attn"], p["w_in"], p["q_norm"], p["kv_norm"], p["w_uq"], p["w_uk"], cosq, sinq, cosk, sink)


def _in_proj_prompt_kernel(x_ref, g_ref, w_ref, qn_ref, kvn_ref, wuqt_ref, wk_ref, wuvt_ref, cost_ref, sint_ref,
                           cosk_ref, sink_ref, qt_ref, kh_ref, vt_ref, c_ref, kr_ref, z_ref, xbc_ref, dt_ref):
    h = _rms(x_ref[...], g_ref[...]).astype(_BF)
    proj = _dot(h, w_ref[...])
    z_ref[...] = proj[:, _C_Z:_C_XBC]
    xbc_ref[...] = proj[:, _C_XBC:_C_DT]
    dt_ref[...] = proj[:, _C_DT:_N_IN_PAD]

    c = _rms(proj[:, _C_KV:_C_KPE], kvn_ref[...])
    kr = proj[:, _C_KPE:_C_KPES] * cosk_ref[...] + proj[:, _C_KPES:_C_Z] * sink_ref[...]
    c_ref[...] = c
    kr_ref[...] = kr[:, QK_NOPE:QK_NOPE + QK_ROPE]
    c_bf = c.astype(_BF)
    kn = _dot(c_bf, wk_ref[...])
    for hd in range(MLA_HEADS):
        kh_ref[:, hd * LANES:(hd + 1) * LANES] = (kn[:, hd * LANES:(hd + 1) * LANES] + kr).astype(_BF)
    vt_ref[0] = _dot_nt(wuvt_ref[...], c_bf).astype(_BF)

    qn = _rms(proj[:, _C_QLAT:_C_KV], qn_ref[...]).astype(_BF)
    qqt = _dot_nt(wuqt_ref[...], qn)
    half = MLA_HEADS * LANES
    for hd in range(MLA_HEADS):
        qt = (qqt[hd * LANES:(hd + 1) * LANES] * cost_ref[...]
              + qqt[half + hd * LANES:half + (hd + 1) * LANES] * sint_ref[...])
        qt_ref[0, hd] = qt.astype(_BF)


def _in_proj_prompt(x2d, b, l, p, tl):
    t = b * l
    n_l = l // tl
    cos32, sin32 = _rope_tables(l, 0)
    lead = jnp.zeros((l, QK_NOPE), _F32)
    trail = jnp.zeros((l, LANES - QK_NOPE - QK_ROPE), _F32)
    cosk = jnp.concatenate([lead, cos32, trail], axis=1)
    sink = jnp.concatenate([lead, sin32, trail], axis=1)
    cost = (jnp.concatenate([lead + 1.0, cos32, trail], axis=1) * SCALE_LOG2).T
    sint = (sink * SCALE_LOG2).T

    row_spec = lambda w: pl.BlockSpec((tl, w), lambda i: (i, 0))
    out_shape = (
        jax.ShapeDtypeStruct((b, MLA_HEADS, LANES, l), _BF),
        jax.ShapeDtypeStruct((t, MLA_HEADS * LANES), _BF),
        jax.ShapeDtypeStruct((b, D_ATTN, l), _BF),
        jax.ShapeDtypeStruct((t, KV_LORA), _F32),
        jax.ShapeDtypeStruct((t, QK_ROPE), _F32),
        jax.ShapeDtypeStruct((t, D_SSM), _F32),
        jax.ShapeDtypeStruct((t, CONV_DIM), _F32),
        jax.ShapeDtypeStruct((t, LANES), _F32),
    )
    out_specs = (
        pl.BlockSpec((1, MLA_HEADS, LANES, tl), lambda i: (i // n_l, 0, 0, i % n_l)),
        row_spec(MLA_HEADS * LANES),
        pl.BlockSpec((1, D_ATTN, tl), lambda i: (i // n_l, 0, i % n_l)),
        row_spec(KV_LORA), row_spec(QK_ROPE), row_spec(D_SSM), row_spec(CONV_DIM), row_spec(LANES),
    )
    in_specs = [
        row_spec(D_MODEL), _const_spec((1, D_MODEL)), _const_spec((D_MODEL, _N_IN_PAD)),
        _const_spec((1, Q_LORA)), _const_spec((1, KV_LORA)), _const_spec((2 * MLA_HEADS * LANES, Q_LORA)),
        _const_spec((KV_LORA, MLA_HEADS * LANES)), _const_spec((D_ATTN, KV_LORA)),
        pl.BlockSpec((LANES, tl), lambda i: (0, i % n_l)), pl.BlockSpec((LANES, tl), lambda i: (0, i % n_l)),
        pl.BlockSpec((tl, LANES), lambda i: (i % n_l, 0)), pl.BlockSpec((tl, LANES), lambda i: (i % n_l, 0)),
    ]
    return pl.pallas_call(
        _in_proj_prompt_kernel,
        grid=(t // tl,), in_specs=in_specs, out_specs=out_specs, out_shape=out_shape,
        compiler_params=pltpu.CompilerParams(dimension_semantics=("arbitrary",), vmem_limit_bytes=VMEM_LIMIT),
        name="in_proj_prompt",
    )(x2d, p["g_attn"], p["w_in_prompt"], p["q_norm"], p["kv_norm"], p["w_uq_t"], p["w_k"], p["w_uv_t"],
      cost, sint, cosk, sink)


def _attn_prompt_kernel(it_ref, jt_ref, qt_ref, kh_ref, vt_ref, o_ref, m_ref, l_ref, acc_ref, *, tq):
    s_idx = pl.program_id(1)
    i = it_ref[s_idx]
    j = jt_ref[s_idx]

    @pl.when(j == 0)
    def _():
        m_ref[...] = jnp.full(m_ref.shape, -jnp.inf, _F32)
        l_ref[...] = jnp.zeros(l_ref.shape, _F32)
        acc_ref[...] = jnp.zeros(acc_ref.shape, _F32)

    def step(masked):
        if masked:
            krow = lax.broadcasted_iota(jnp.int32, (tq, tq), 0)
            qcol = lax.broadcasted_iota(jnp.int32, (tq, tq), 1)
            keep = krow <= qcol

        def scores(hd):
            st = _dot(kh_ref[:, hd * LANES:(hd + 1) * LANES], qt_ref[0, hd])
            if masked:
                st = jnp.where(keep, st, -jnp.inf)
            return st.reshape(tq // SUBLANES, SUBLANES, tq)

        ahead = {hd: scores(hd) for hd in range(QK_AHEAD)}
        for hd in range(MLA_HEADS):
            if hd + QK_AHEAD < MLA_HEADS:
                ahead[hd + QK_AHEAD] = scores(hd + QK_AHEAD)
            st = ahead.pop(hd)
            m_prev = m_ref[hd]
            m_new = jnp.maximum(m_prev, _sublane_all(jnp.maximum, jnp.max(st, axis=0)))
            alpha = jnp.exp2(m_prev - m_new)
            pr = jnp.exp2(st - m_new[None])
            l_ref[hd] = alpha * l_ref[hd] + jnp.sum(pr, axis=0)
            rows = slice(hd * V_HEAD, (hd + 1) * V_HEAD)
            acc = acc_ref[rows, :].reshape(V_HEAD // SUBLANES, SUBLANES, tq) * alpha[None]
            acc_ref[rows, :] = acc.reshape(V_HEAD, tq) + _dot(vt_ref[0, rows, :], pr.reshape(tq, tq).astype(_BF))
            m_ref[hd] = m_new

    @pl.when(j < i)
    def _():
        step(False)

    @pl.when(j == i)
    def _():
        step(True)
        for hd in range(MLA_HEADS):
            rows = slice(hd * V_HEAD, (hd + 1) * V_HEAD)
            den = _sublane_all(jnp.add, l_ref[hd])
            acc = acc_ref[rows, :].reshape(V_HEAD // SUBLANES, SUBLANES, tq) / den[None]
            acc_ref[rows, :] = acc.reshape(V_HEAD, tq)
        o_ref[...] = acc_ref[...].T.astype(o_ref.dtype)


def _attn_prompt(qt, kh, vt, b, l, tq):
    nq = l // tq
    pairs = [(i, j) for i in range(nq) for j in range(i + 1)]
    it = jnp.asarray([pr[0] for pr in pairs], jnp.int32)
    jt = jnp.asarray([pr[1] for pr in pairs], jnp.int32)
    grid_spec = pltpu.PrefetchScalarGridSpec(
        num_scalar_prefetch=2,
        grid=(b, len(pairs)),
        in_specs=[
            pl.BlockSpec((1, MLA_HEADS, LANES, tq), lambda bi, s, it, jt: (bi, 0, 0, it[s])),
            pl.BlockSpec((tq, MLA_HEADS * LANES), lambda bi, s, it, jt: (bi * nq + jt[s], 0)),
            pl.BlockSpec((1, D_ATTN, tq), lambda bi, s, it, jt: (bi, 0, jt[s])),
        ],
        out_specs=pl.BlockSpec((tq, D_ATTN), lambda bi, s, it, jt: (bi * nq + it[s], 0)),
        scratch_shapes=[pltpu.VMEM((MLA_HEADS, SUBLANES, tq), _F32), pltpu.VMEM((MLA_HEADS, SUBLANES, tq), _F32),
                        pltpu.VMEM((D_ATTN, tq), _F32)],
    )
    return pl.pallas_call(
        functools.partial(_attn_prompt_kernel, tq=tq),
        grid_spec=grid_spec,
        out_shape=jax.ShapeDtypeStruct((b * l, D_ATTN), _BF),
        compiler_params=pltpu.CompilerParams(dimension_semantics=("arbitrary", "arbitrary"),
                                             vmem_limit_bytes=VMEM_LIMIT),
        name="attn_prompt",
    )(it, jt, qt, kh, vt)


def _attn_sample_kernel(pt_ref, q_ref, kvn_ref, cache_c_ref, cache_krt_ref, o_ref, cbuf_ref, krbuf_ref, kc_ref,
                        sem_ref, *, n_pages, page, l_new, n_split):
    b = pl.program_id(0)
    nb = pl.num_programs(0)
    slot = b % 2

    def page_copies(item, slot_, pg):
        pid = pt_ref[item, pg]
        dst_rows = pl.ds(pl.multiple_of(pg * page, page), page)
        return (pltpu.make_async_copy(cache_c_ref.at[pid], cbuf_ref.at[slot_, dst_rows, :], sem_ref.at[slot_, 0]),
                pltpu.make_async_copy(cache_krt_ref.at[pid], krbuf_ref.at[slot_, :, dst_rows], sem_ref.at[slot_, 1]))

    def start_page(item, slot_, pg):
        for cp in page_copies(item, slot_, pg):
            cp.start()

    def wait_item(slot_):
        pltpu.make_async_copy(cbuf_ref.at[slot_], cbuf_ref.at[slot_], sem_ref.at[slot_, 0]).wait()
        pltpu.make_async_copy(krbuf_ref.at[slot_], krbuf_ref.at[slot_], sem_ref.at[slot_, 1]).wait()

    @pl.when(b == 0)
    def _():
        def body(pg, carry):
            start_page(0, 0, pg)
            return carry
        lax.fori_loop(0, n_pages, body, 0)

    wait_item(slot)
    nxt = jnp.minimum(b + 1, nb - 1)
    pages_per_split = n_pages // n_split

    m_rows = MLA_HEADS * l_new
    q = q_ref[0]
    q_abs = q[:, :KV_LORA]
    q_pe = q[:, KV_LORA:KV_LORA + QK_ROPE]
    ks = pages_per_split * page

    def scores(sp):
        rows = pl.ds(sp * ks, ks)
        kc_ref[rows, :] = cbuf_ref[slot, rows, :].astype(_BF)
        return _dot_nt(q_abs, kc_ref[rows, :]) + _dot(q_pe, krbuf_ref[slot, :, rows].astype(_BF))

    def partial(s, values):
        m = jnp.max(s, axis=1, keepdims=True)
        pr = jnp.exp2(s - m)
        return m, jnp.sum(pr, axis=1, keepdims=True), _dot(pr.astype(_BF), values)

    def merge(run, new):
        m = jnp.maximum(run[0], new[0])
        a, c = jnp.exp2(run[0] - m), jnp.exp2(new[0] - m)
        return m, a * run[1] + c * new[1], a * run[2] + c * new[2]

    ahead = {sp: scores(sp) for sp in range(min(SPLITS_AHEAD, n_split))}
    kvn = kvn_ref[0]
    sn = _dot_nt(q, kvn)
    qpos = lax.broadcasted_iota(jnp.int32, (MLA_HEADS, l_new, l_new), 1).reshape(m_rows, l_new)
    kpos = lax.broadcasted_iota(jnp.int32, (m_rows, l_new), 1)
    run = partial(jnp.where(kpos <= qpos, sn, -jnp.inf), kvn[:, :KV_LORA])
    for sp in range(n_split):
        for pg in range(sp * pages_per_split, (sp + 1) * pages_per_split):
            start_page(nxt, 1 - slot, pg)
        if sp + SPLITS_AHEAD < n_split:
            ahead[sp + SPLITS_AHEAD] = scores(sp + SPLITS_AHEAD)
        run = merge(run, partial(ahead.pop(sp), kc_ref[pl.ds(sp * ks, ks), :]))
    o_ref[0] = (run[2] / run[1]).astype(o_ref.dtype)

    @pl.when(b == nb - 1)
    def _():
        wait_item(1 - slot)


def _attn_sample(q, kvn, cache_c, cache_krt, page_table):
    b, m_rows, _ = q.shape
    l_new = m_rows // MLA_HEADS
    n_pages = page_table.shape[1]
    page = cache_c.shape[1]
    keys = n_pages * page
    grid_spec = pltpu.PrefetchScalarGridSpec(
        num_scalar_prefetch=1,
        grid=(b,),
        in_specs=[pl.BlockSpec((1, m_rows, QK_PAD), lambda bi, pt: (bi, 0, 0)),
                  pl.BlockSpec((1, l_new, QK_PAD), lambda bi, pt: (bi, 0, 0)),
                  pl.BlockSpec(memory_space=pl.ANY), pl.BlockSpec(memory_space=pl.ANY)],
        out_specs=pl.BlockSpec((1, m_rows, KV_LORA), lambda bi, pt: (bi, 0, 0)),
        scratch_shapes=[pltpu.VMEM((2, keys, KV_LORA), _F32), pltpu.VMEM((2, QK_ROPE, keys), _F32),
                        pltpu.VMEM((keys, KV_LORA), _BF), pltpu.SemaphoreType.DMA((2, 2))],
    )
    return pl.pallas_call(
        functools.partial(_attn_sample_kernel, n_pages=n_pages, page=page, l_new=l_new,
                          n_split=_largest_divisor(n_pages, 8)),
        grid_spec=grid_spec,
        out_shape=jax.ShapeDtypeStruct((b, m_rows, KV_LORA), _BF),
        compiler_params=pltpu.CompilerParams(dimension_semantics=("arbitrary",), vmem_limit_bytes=VMEM_LIMIT),
        name="attn_sample",
    )(page_table, q, kvn, cache_c, cache_krt)


def _uv_proj_kernel(o_ref, wuv_ref, out_ref, *, tb, l_new):
    for hd in range(MLA_HEADS):
        o = o_ref[:, hd].reshape(tb * l_new, KV_LORA)
        out_ref[:, hd * V_HEAD:(hd + 1) * V_HEAD] = _dot(o, wuv_ref[hd]).astype(out_ref.dtype)


def _uv_proj(o_lat, wuv, tb):
    b, _, l_new, _ = o_lat.shape
    return pl.pallas_call(
        functools.partial(_uv_proj_kernel, tb=tb, l_new=l_new),
        grid=(b // tb,),
        in_specs=[pl.BlockSpec((tb, MLA_HEADS, l_new, KV_LORA), lambda i: (i, 0, 0, 0)),
                  _const_spec((MLA_HEADS, KV_LORA, V_HEAD))],
        out_specs=pl.BlockSpec((tb * l_new, D_ATTN), lambda i: (i, 0)),
        out_shape=jax.ShapeDtypeStruct((b * l_new, D_ATTN), _BF),
        compiler_params=pltpu.CompilerParams(dimension_semantics=("arbitrary",), vmem_limit_bytes=VMEM_LIMIT),
        name="uv_proj",
    )(o_lat, wuv)


_XP_OFF = 8


def _ssd_kernel(xbc_ref, z_ref, dtr_ref, cbuf_ref, h0_ref, cw_ref, cb_ref, dtb_ref, alog_ref, dskip_ref, nrm_ref,
                sel_ref, ex_ref, exw_ref, o_ref, h_ref, cnew_ref, xp_ref, y_ref, *, g_items, q):
    c_idx = pl.program_id(1)
    last = pl.num_programs(1) - 1
    heads_per_group = SSM_HEADS // SSM_GROUPS
    rows = g_items * q
    row = lax.broadcasted_iota(jnp.int32, (q, q), 0)
    col = lax.broadcasted_iota(jnp.int32, (q, q), 1)
    causal = row >= col

    def conv_item(g):
        @pl.when(c_idx == 0)
        def _():
            xp_ref[g, _XP_OFF - (CONV_K - 1):_XP_OFF, :] = cbuf_ref[g]
            h_ref[g] = h0_ref[g]

        xp_ref[g, _XP_OFF:_XP_OFF + q, :] = xbc_ref[g * q:(g + 1) * q, :]
        conv = cb_ref[...]
        for k in range(CONV_K):
            conv = conv + xp_ref[g, _XP_OFF - (CONV_K - 1) + k:_XP_OFF - (CONV_K - 1) + k + q, :] * cw_ref[k:k + 1, :]
        conv = _silu(conv)
        tail = xp_ref[g, _XP_OFF + q - (CONV_K - 1):_XP_OFF + q, :]
        xp_ref[g, _XP_OFF - (CONV_K - 1):_XP_OFF, :] = tail

        @pl.when(c_idx == last)
        def _():
            cnew_ref[g] = tail

        return (conv[:, :D_SSM], conv[:, D_SSM:D_SSM + SSM_GROUPS * D_STATE].astype(_BF),
                conv[:, D_SSM + SSM_GROUPS * D_STATE:].astype(_BF))

    convs = {0: conv_item(0)} if g_items == 1 else {}

    if g_items == 1:
        tril = jnp.where(causal, 1.0, 0.0).astype(_BF)
    else:
        r_all = lax.broadcasted_iota(jnp.int32, (rows, rows), 0)
        c_all = lax.broadcasted_iota(jnp.int32, (rows, rows), 1)
        same_item = (r_all // q) == (c_all // q)
        tril = jnp.where(r_all >= c_all, jnp.where(same_item, 1.0, 0.0), 0.0).astype(_BF)
    a_neg = -jnp.exp(alog_ref[...])
    dtv = dtr_ref[...] + dtb_ref[...]
    dt = jnp.maximum(dtv, 0.0) + jnp.log1p(jnp.exp(-jnp.abs(dtv)))
    cs = _dot(tril, _split3(dt * a_neg))
    acs = cs[:, :LANES] + cs[:, LANES:2 * LANES] + cs[:, 2 * LANES:]
    acs3 = _split3(acs)
    acs_t = _dot_nt(sel_ref[...], acs3)
    dt_x = _dot(_split3(dt), ex_ref[...])
    acs_x = _dot(acs3, ex_ref[...])
    acs_w = _dot(acs3, exw_ref[...])
    e_acs = jnp.exp(acs_x)

    for g in range(g_items):
        rs = slice(g * q, (g + 1) * q)
        xs, bmat, cmat = convs[g] if g in convs else conv_item(g)
        acs_last_x = acs_x[(g + 1) * q - 1:(g + 1) * q, :]
        x_dt = xs * dt_x[rs]
        xw = (x_dt * jnp.exp(acs_last_x - acs_x[rs])).astype(_BF)
        x_dt = x_dt.astype(_BF)
        for grp in range(SSM_GROUPS):
            b_g = bmat[:, grp * D_STATE:(grp + 1) * D_STATE]
            c_g = cmat[:, grp * D_STATE:(grp + 1) * D_STATE]
            cb = _dot_nt(c_g, b_g)
            for hd in range(grp * heads_per_group, (grp + 1) * heads_per_group):
                cols = slice(hd * SSM_HEADDIM, (hd + 1) * SSM_HEADDIM)
                acs_h = acs_w[rs, hd * LANES:(hd + 1) * LANES]
                acs_h = acs_h[:, :q] if q <= LANES else jnp.concatenate([acs_h] * (q // LANES), axis=1)
                lmat = jnp.exp(jnp.where(causal, acs_h - acs_t[hd:hd + 1, rs], -jnp.inf))
                h_prev = h_ref[g, hd]
                y = _dot((cb * lmat).astype(_BF), x_dt[:, cols])
                y = y + _dot_nt(c_g, h_prev.astype(_BF)) * e_acs[rs, cols]
                decay = jnp.exp(acs[(g + 1) * q - 1:(g + 1) * q, hd:hd + 1])
                h_ref[g, hd] = decay * h_prev + _dot_tn(xw[:, cols], b_g)
                y_ref[:, cols] = y
        yv = y_ref[...] + dskip_ref[...] * xs
        gated = yv * _silu(z_ref[g * q:(g + 1) * q, :])
        gw = D_SSM // SSM_GROUPS
        parts = []
        for grp in range(SSM_GROUPS):
            gg = gated[:, grp * gw:(grp + 1) * gw]
            parts.append(gg * lax.rsqrt(jnp.mean(gg * gg, axis=-1, keepdims=True) + EPS))
        o_ref[g * q:(g + 1) * q, :] = (jnp.concatenate(parts, axis=1) * nrm_ref[...]).astype(o_ref.dtype)


def _ssd(xbc, z, dtr, conv_buf, h0, p, b, l, q, g_items):
    nc = l // q
    rows = g_items * q
    row_spec = lambda w: pl.BlockSpec((rows, w), lambda bi, c: (bi * nc + c, 0))
    in_specs = [
        row_spec(CONV_DIM), row_spec(D_SSM), row_spec(LANES),
        pl.BlockSpec((g_items, CONV_K - 1, CONV_DIM), lambda bi, c: (bi, 0, 0)),
        pl.BlockSpec((g_items, SSM_HEADS, SSM_HEADDIM, D_STATE), lambda bi, c: (bi, 0, 0, 0)),
        _const_spec((CONV_K, CONV_DIM)), _const_spec((1, CONV_DIM)), _const_spec((1, LANES)), _const_spec((1, LANES)),
        _const_spec((1, D_SSM)), _const_spec((1, D_SSM)), _const_spec((SSM_HEADS, 3 * LANES)),
        _const_spec((3 * LANES, D_SSM)), _const_spec((3 * LANES, SSM_HEADS * LANES)),
    ]
    out_specs = (
        row_spec(D_SSM),
        pl.BlockSpec((g_items, SSM_HEADS, SSM_HEADDIM, D_STATE), lambda bi, c: (bi, 0, 0, 0)),
        pl.BlockSpec((g_items, CONV_K - 1, CONV_DIM), lambda bi, c: (bi, 0, 0)),
    )
    out_shape = (
        jax.ShapeDtypeStruct((b * l, D_SSM), _BF),
        jax.ShapeDtypeStruct((b, SSM_HEADS, SSM_HEADDIM, D_STATE), _F32),
        jax.ShapeDtypeStruct((b, CONV_K - 1, CONV_DIM), _F32),
    )
    return pl.pallas_call(
        functools.partial(_ssd_kernel, g_items=g_items, q=q),
        grid=(b // g_items, nc), in_specs=in_specs, out_specs=out_specs, out_shape=out_shape,
        scratch_shapes=[pltpu.VMEM((g_items, _XP_OFF + q, CONV_DIM), _F32), pltpu.VMEM((q, D_SSM), _F32)],
        compiler_params=pltpu.CompilerParams(dimension_semantics=("arbitrary", "arbitrary"),
                                             vmem_limit_bytes=VMEM_LIMIT),
        name="ssd",
    )(xbc, z, dtr, conv_buf, h0, p["conv_w"], p["conv_b"], p["dt_bias"], p["a_log"], p["d_skip"], p["ssm_norm"],
      p["sel"], p["head_to_cols"], p["head_to_lanes"])


def _out_mlp_kernel(x_ref, oa_ref, os_ref, wo_ref, gm_ref, wu_ref, wd_ref, gf_ref, y_ref, *, f_chunk):
    mix = jnp.concatenate([oa_ref[...], os_ref[...]], axis=1)
    x1 = x_ref[...] + _dot(mix, wo_ref[...])
    h2 = _rms(x1, gm_ref[...]).astype(_BF)
    acc = x1
    for f0 in range(0, D_FF, f_chunk):
        u = jnp.maximum(_dot(h2, wu_ref[:, f0:f0 + f_chunk]), 0.0)
        acc = acc + _dot((u * u).astype(_BF), wd_ref[f0:f0 + f_chunk, :])
    y_ref[...] = _rms(acc, gf_ref[...])


def _out_mlp(x2d, oa, osm, p, tm):
    t = x2d.shape[0]
    row_spec = lambda w: pl.BlockSpec((tm, w), lambda i: (i, 0))
    once = lambda shape: pl.BlockSpec(shape, lambda i: (0,) * len(shape), pipeline_mode=pl.Buffered(1))
    return pl.pallas_call(
        functools.partial(_out_mlp_kernel, f_chunk=1024),
        grid=(t // tm,),
        in_specs=[row_spec(D_MODEL), row_spec(D_ATTN), row_spec(D_SSM), once((D_MODEL, D_MODEL)),
                  once((1, D_MODEL)), once((D_MODEL, D_FF)), once((D_FF, D_MODEL)), once((1, D_MODEL))],
        out_specs=row_spec(D_MODEL),
        out_shape=jax.ShapeDtypeStruct((t, D_MODEL), _F32),
        compiler_params=pltpu.CompilerParams(dimension_semantics=("arbitrary",), vmem_limit_bytes=VMEM_LIMIT),
        name="out_mlp",
    )(x2d, oa, osm, p["w_out"], p["g_mlp"], p["w_up"], p["w_down"], p["g_final"])


def _prep_params(norm_attn, w_in, q_norm, kv_norm, w_uq, w_uk, w_uv, conv_w, conv_b, dt_bias, a_log, d_skip,
                 ssm_norm, w_out, norm_mlp, w_up, w_down, norm_final):
    half = QK_ROPE // 2

    def swap(w):
        return jnp.concatenate([w[..., half:], w[..., :half]], axis=-1)

    def place(w, lead, n):
        return jnp.pad(w, ((0, 0), (lead, n - lead - w.shape[1])))

    o1 = Q_LORA
    o2 = o1 + KV_LORA
    o3 = o2 + QK_ROPE
    o5 = o3 + D_SSM + CONV_DIM
    w_kpe = w_in[:, o2:o3]

    def w_in_padded(lead):
        return jnp.concatenate([
            w_in[:, :o2], place(w_kpe, lead, LANES), place(swap(w_kpe), lead, LANES), w_in[:, o3:o5],
            place(w_in[:, o5:], 0, LANES)], axis=1).astype(_BF)

    uq = w_uq.reshape(Q_LORA, MLA_HEADS, QK_NOPE + QK_ROPE)
    uq_nope = uq[:, :, :QK_NOPE]
    uq_rope = uq[:, :, QK_NOPE:]
    w_uq_p = jnp.concatenate([
        jnp.pad(uq_nope, ((0, 0), (0, 0), (0, LANES - QK_NOPE))).reshape(Q_LORA, MLA_HEADS * LANES),
        uq_rope.reshape(Q_LORA, -1), swap(uq_rope).reshape(Q_LORA, -1)], axis=1).astype(_BF)
    w_uk_p = jnp.pad(w_uk, ((0, 0), (0, LANES - QK_NOPE), (0, 0))).astype(_BF)
    tail = LANES - QK_NOPE - QK_ROPE
    plain = jnp.pad(uq, ((0, 0), (0, 0), (0, tail)))
    swapped = jnp.pad(swap(uq_rope), ((0, 0), (0, 0), (QK_NOPE, tail)))
    w_uq_t = jnp.concatenate([plain.reshape(Q_LORA, -1), swapped.reshape(Q_LORA, -1)], axis=1).T.astype(_BF)
    w_k = jnp.pad(jnp.transpose(w_uk, (2, 0, 1)), ((0, 0), (0, 0), (0, LANES - QK_NOPE)))
    w_k = w_k.reshape(KV_LORA, MLA_HEADS * LANES).astype(_BF)
    w_uv_t = jnp.transpose(w_uv, (0, 2, 1)).reshape(D_ATTN, KV_LORA).astype(_BF)
    pad_heads = lambda v: jnp.pad(v.reshape(1, SSM_HEADS), ((0, 0), (0, LANES - SSM_HEADS)))
    return dict(
        g_attn=norm_attn.reshape(1, D_MODEL), w_in=w_in_padded(0), w_in_prompt=w_in_padded(QK_NOPE),
        q_norm=q_norm.reshape(1, Q_LORA), kv_norm=kv_norm.reshape(1, KV_LORA), w_uq=w_uq_p, w_uk=w_uk_p,
        w_uq_t=w_uq_t, w_k=w_k, w_uv_t=w_uv_t, w_uv=w_uv.astype(_BF),
        conv_w=conv_w, conv_b=conv_b.reshape(1, CONV_DIM), dt_bias=pad_heads(dt_bias), a_log=pad_heads(a_log),
        d_skip=jnp.repeat(d_skip, SSM_HEADDIM).reshape(1, D_SSM), ssm_norm=ssm_norm.reshape(1, D_SSM),
        sel=jnp.tile(jnp.eye(SSM_HEADS, LANES, dtype=_BF), (1, 3)),
        head_to_cols=jnp.tile(jnp.repeat(jnp.eye(LANES, SSM_HEADS, dtype=_BF), SSM_HEADDIM, axis=1), (3, 1)),
        head_to_lanes=jnp.tile(jnp.repeat(jnp.eye(LANES, SSM_HEADS, dtype=_BF), LANES, axis=1), (3, 1)),
        w_out=w_out.astype(_BF), g_mlp=norm_mlp.reshape(1, D_MODEL), w_up=w_up.astype(_BF),
        w_down=w_down.astype(_BF), g_final=norm_final.reshape(1, D_MODEL),
    )


def _largest_divisor(n, cap):
    d = min(n, cap)
    while n % d:
        d -= 1
    return d


def kernel(x_prompt, x_sample, cache_kv_latent, cache_k_rope, state_ssm, state_conv, page_table, norm_attn, w_in, q_norm, kv_norm, w_uq, w_uk, w_uv, conv_w, conv_b, dt_bias, a_log, d_skip, ssm_norm, w_out, norm_mlp, w_up, w_down, norm_final):
    assert norm_attn.shape[0] == 1, "single-layer model"
    p = _prep_params(norm_attn[0], w_in[0], q_norm[0], kv_norm[0], w_uq[0], w_uk[0], w_uv[0], conv_w[0], conv_b[0],
                     dt_bias[0], a_log[0], d_skip[0], ssm_norm[0], w_out[0], norm_mlp[0], w_up[0], w_down[0],
                     norm_final)
    bp, lp, _ = x_prompt.shape
    bs, ls, _ = x_sample.shape
    n_pages = page_table.shape[1]
    page = cache_kv_latent.shape[2]
    past_len = n_pages * page

    xp2 = x_prompt.reshape(bp * lp, D_MODEL)
    qt, kh, vt, c_new, kr_new, z, xbc, dtr = _in_proj_prompt(xp2, bp, lp, p, _largest_divisor(lp, 512))
    o_attn = _attn_prompt(qt, kh, vt, bp, lp, _largest_divisor(lp, 512))
    chunk = _largest_divisor(lp, CHUNK)
    o_ssm, h_p, conv_p = _ssd(xbc, z, dtr, jnp.zeros((bp, CONV_K - 1, CONV_DIM), _F32),
                              jnp.zeros((bp, SSM_HEADS, SSM_HEADDIM, D_STATE), _F32), p, bp, lp, chunk, 1)
    y_p = _out_mlp(xp2, o_attn, o_ssm, p, _largest_divisor(bp * lp, 512))
    outs_p = (y_p.reshape(bp, lp, D_MODEL), c_new.reshape(1, bp, lp, KV_LORA), kr_new.reshape(1, bp, lp, QK_ROPE),
              h_p[None], conv_p[None])

    xs2 = x_sample.reshape(bs * ls, D_MODEL)
    tb = _largest_divisor(bs, 512 // ls)
    q, kv, c_new, kr_new, z, xbc, dtr = _in_proj(xs2, bs, ls, past_len, p, tb, ls)
    cache_krt = jnp.swapaxes(cache_k_rope[0], 1, 2)
    o_lat = _attn_sample(q.reshape(bs, MLA_HEADS * ls, QK_PAD), kv.reshape(bs, ls, QK_PAD), cache_kv_latent[0],
                         cache_krt, page_table)
    o_attn = _uv_proj(o_lat.reshape(bs, MLA_HEADS, ls, KV_LORA), p["w_uv"], tb)
    g_items = _largest_divisor(bs, 8)
    o_ssm, h_s, conv_s = _ssd(xbc, z, dtr, state_conv[0], state_ssm[0], p, bs, ls, ls, g_items)
    y_s = _out_mlp(xs2, o_attn, o_ssm, p, _largest_divisor(bs * ls, 512))
    outs_s = (y_s.reshape(bs, ls, D_MODEL), c_new.reshape(1, bs, ls, KV_LORA), kr_new.reshape(1, bs, ls, QK_ROPE),
              h_s[None], conv_s[None])

    return (outs_p[0], outs_s[0], outs_p[1], outs_p[2], outs_p[3], outs_p[4],
            outs_s[1], outs_s[2], outs_s[3], outs_s[4])
```

```python
import functools

import jax
import jax.numpy as jnp
from jax import lax
from jax.experimental import pallas as pl
from jax.experimental.pallas import tpu as pltpu

D_MODEL = 1024
D_ATTN = 512
D_SSM = 512
V_HEAD = 64
MLA_HEADS = 8
QK_NOPE = 64
QK_ROPE = 32
KV_LORA = 256
Q_LORA = 384
ROPE_THETA = 10000.0
SSM_HEADDIM = 64
SSM_HEADS = 8
SSM_GROUPS = 2
D_STATE = 128
CONV_K = 4
CHUNK = 256
CONV_DIM = D_SSM + 2 * SSM_GROUPS * D_STATE
D_FF = 4096
EPS = 1e-6

LANES = 128
SUBLANES = 8
QK_AHEAD = 2
SPLITS_AHEAD = 4
QK_PAD = KV_LORA + LANES
VMEM_LIMIT = 56 * 1024 * 1024
SCALE_LOG2 = (QK_NOPE + QK_ROPE) ** -0.5 * 1.4426950408889634

_C_QLAT = 0
_C_KV = _C_QLAT + Q_LORA
_C_KPE = _C_KV + KV_LORA
_C_KPES = _C_KPE + LANES
_C_Z = _C_KPES + LANES
_C_XBC = _C_Z + D_SSM
_C_DT = _C_XBC + CONV_DIM
_N_IN_PAD = _C_DT + LANES
_Q_NOPE = 0
_Q_ROPE = MLA_HEADS * LANES
_Q_ROPES = _Q_ROPE + MLA_HEADS * QK_ROPE
_N_UQ_PAD = _Q_ROPES + MLA_HEADS * QK_ROPE

_BF = jnp.bfloat16
_F32 = jnp.float32


def _dot(a, b):
    return jnp.dot(a, b, preferred_element_type=_F32)


def _dot_nt(a, b):
    return lax.dot_general(a, b, (((1,), (1,)), ((), ())), preferred_element_type=_F32)


def _dot_tn(a, b):
    return lax.dot_general(a, b, (((0,), (0,)), ((), ())), preferred_element_type=_F32)


def _split3(a):
    hi = a.astype(_BF)
    r1 = a - hi.astype(_F32)
    mid = r1.astype(_BF)
    lo = (r1 - mid.astype(_F32)).astype(_BF)
    return jnp.concatenate([hi, mid, lo], axis=1)


def _rms(x, g):
    return x * lax.rsqrt(jnp.mean(x * x, axis=-1, keepdims=True) + EPS) * g


def _silu(x):
    return x * (1.0 / (1.0 + jnp.exp(-x)))


def _sublane_all(op, x):
    for shift in (4, 2, 1):
        x = op(x, pltpu.roll(x, shift, 0))
    return x


def _const_spec(shape):
    nd = len(shape)
    return pl.BlockSpec(shape, lambda *_: (0,) * nd)


def _rope_tables(l, past_len):
    pos = (past_len + jnp.arange(l, dtype=jnp.int32)).astype(_F32)
    inv = ROPE_THETA ** (-(jnp.arange(0, QK_ROPE, 2, dtype=_F32) / QK_ROPE))
    ang = pos[:, None] * inv[None, :]
    cos, sin = jnp.cos(ang), jnp.sin(ang)
    return jnp.concatenate([cos, cos], axis=1), jnp.concatenate([-sin, sin], axis=1)


def _in_proj_kernel(x_ref, g_ref, w_ref, qn_ref, kvn_ref, wuq_ref, wuk_ref, cosq_ref, sinq_ref, cosk_ref, sink_ref,
                    q_ref, kv_ref, c_ref, kr_ref, z_ref, xbc_ref, dt_ref, *, tb, tl):
    tm = tb * tl

    def rows(tab_ref):
        t = tab_ref[...]
        if tb == 1:
            return t
        return jnp.broadcast_to(t[None], (tb, tl, t.shape[-1])).reshape(tm, t.shape[-1])

    h = _rms(x_ref[...], g_ref[...]).astype(_BF)
    proj = _dot(h, w_ref[...])
    z_ref[...] = proj[:, _C_Z:_C_XBC]
    xbc_ref[...] = proj[:, _C_XBC:_C_DT]
    dt_ref[...] = proj[:, _C_DT:_N_IN_PAD]

    c = _rms(proj[:, _C_KV:_C_KPE], kvn_ref[...])
    kr = proj[:, _C_KPE:_C_KPES] * rows(cosk_ref) + proj[:, _C_KPES:_C_Z] * rows(sink_ref)
    c_ref[...] = c
    kr_ref[...] = kr[:, :QK_ROPE]
    kv_ref[:, :KV_LORA] = c.astype(_BF)
    kv_ref[:, KV_LORA:] = kr.astype(_BF)

    qn = _rms(proj[:, _C_QLAT:_C_KV], qn_ref[...]).astype(_BF)
    qq = _dot(qn, wuq_ref[...])
    q_pe = qq[:, _Q_ROPE:_Q_ROPES] * rows(cosq_ref) + qq[:, _Q_ROPES:_N_UQ_PAD] * rows(sinq_ref)
    lane = lax.broadcasted_iota(jnp.int32, (tm, LANES), 1)
    heads_per_group = LANES // QK_ROPE
    for hd in range(MLA_HEADS):
        q_abs = _dot(qq[:, hd * LANES:(hd + 1) * LANES].astype(_BF), wuk_ref[hd]) * SCALE_LOG2
        grp = q_pe[:, (hd // heads_per_group) * LANES:(hd // heads_per_group + 1) * LANES]
        shift = (LANES - QK_ROPE * (hd % heads_per_group)) % LANES
        if shift:
            grp = pltpu.roll(grp, shift, 1)
        pe = jnp.where(lane < QK_ROPE, grp * SCALE_LOG2, 0.0)
        q_ref[:, hd, :, :KV_LORA] = q_abs.astype(_BF).reshape(tb, tl, KV_LORA)
        q_ref[:, hd, :, KV_LORA:] = pe.astype(_BF).reshape(tb, tl, LANES)


def _in_proj(x2d, b, l, past_len, p, tb, tl):
    t = b * l
    tm = tb * tl
    n_l = l // tl
    cos32, sin32 = _rope_tables(l, past_len)
    padk = jnp.zeros((l, LANES - QK_ROPE), _F32)
    cosk = jnp.concatenate([cos32, padk], axis=1)
    sink = jnp.concatenate([sin32, padk], axis=1)
    cosq = jnp.tile(cos32, (1, MLA_HEADS))
    sinq = jnp.tile(sin32, (1, MLA_HEADS))

    row_spec = lambda w: pl.BlockSpec((tm, w), lambda i: (i, 0))
    tab_spec = lambda w: pl.BlockSpec((tl, w), lambda i: (i % n_l, 0))
    out_shape = (
        jax.ShapeDtypeStruct((b, MLA_HEADS, l, QK_PAD), _BF),
        jax.ShapeDtypeStruct((t, QK_PAD), _BF),
        jax.ShapeDtypeStruct((t, KV_LORA), _F32),
        jax.ShapeDtypeStruct((t, QK_ROPE), _F32),
        jax.ShapeDtypeStruct((t, D_SSM), _F32),
        jax.ShapeDtypeStruct((t, CONV_DIM), _F32),
        jax.ShapeDtypeStruct((t, LANES), _F32),
    )
    out_specs = (
        pl.BlockSpec((tb, MLA_HEADS, tl, QK_PAD), lambda i: (i // n_l, 0, i % n_l, 0)),
        row_spec(QK_PAD), row_spec(KV_LORA), row_spec(QK_ROPE), row_spec(D_SSM), row_spec(CONV_DIM), row_spec(LANES),
    )
    in_specs = [
        row_spec(D_MODEL), _const_spec((1, D_MODEL)), _const_spec((D_MODEL, _N_IN_PAD)),
        _const_spec((1, Q_LORA)), _const_spec((1, KV_LORA)), _const_spec((Q_LORA, _N_UQ_PAD)),
        _const_spec((MLA_HEADS, LANES, KV_LORA)),
        tab_spec(MLA_HEADS * QK_ROPE), tab_spec(MLA_HEADS * QK_ROPE), tab_spec(LANES), tab_spec(LANES),
    ]
    return pl.pallas_call(
        functools.partial(_in_proj_kernel, tb=tb, tl=tl),
        grid=(t // tm,), in_specs=in_specs, out_specs=out_specs, out_shape=out_shape,
        compiler_params=pltpu.CompilerParams(dimension_semantics=("arbitrary",), vmem_limit_bytes=VMEM_LIMIT),
        name="in_proj",
    )(x2d, p["g_attn"], p["w_in"], p["q_norm"], p["kv_norm"], p["w_uq"], p["w_uk"], cosq, sinq, cosk, sink)


def _in_proj_prompt_kernel(x_ref, g_ref, w_ref, qn_ref, kvn_ref, wuqt_ref, wk_ref, wuvt_ref, cost_ref, sint_ref,
                           cosk_ref, sink_ref, qt_ref, kh_ref, vt_ref, c_ref, kr_ref, z_ref, xbc_ref, dt_ref):
    h = _rms(x_ref[...], g_ref[...]).astype(_BF)
    proj = _dot(h, w_ref[...])
    z_ref[...] = proj[:, _C_Z:_C_XBC]
    xbc_ref[...] = proj[:, _C_XBC:_C_DT]
    dt_ref[...] = proj[:, _C_DT:_N_IN_PAD]

    c = _rms(proj[:, _C_KV:_C_KPE], kvn_ref[...])
    kr = proj[:, _C_KPE:_C_KPES] * cosk_ref[...] + proj[:, _C_KPES:_C_Z] * sink_ref[...]
    c_ref[...] = c
    kr_ref[...] = kr[:, QK_NOPE:QK_NOPE + QK_ROPE]
    c_bf = c.astype(_BF)
    kn = _dot(c_bf, wk_ref[...])
    for hd in range(MLA_HEADS):
        kh_ref[:, hd * LANES:(hd + 1) * LANES] = (kn[:, hd * LANES:(hd + 1) * LANES] + kr).astype(_BF)
    vt_ref[0] = _dot_nt(wuvt_ref[...], c_bf).astype(_BF)

    qn = _rms(proj[:, _C_QLAT:_C_KV], qn_ref[...]).astype(_BF)
    qqt = _dot_nt(wuqt_ref[...], qn)
    half = MLA_HEADS * LANES
    for hd in range(MLA_HEADS):
        qt = (qqt[hd * LANES:(hd + 1) * LANES] * cost_ref[...]
              + qqt[half + hd * LANES:half + (hd + 1) * LANES] * sint_ref[...])
        qt_ref[0, hd] = qt.astype(_BF)


def _in_proj_prompt(x2d, b, l, p, tl):
    t = b * l
    n_l = l // tl
    cos32, sin32 = _rope_tables(l, 0)
    lead = jnp.zeros((l, QK_NOPE), _F32)
    trail = jnp.zeros((l, LANES - QK_NOPE - QK_ROPE), _F32)
    cosk = jnp.concatenate([lead, cos32, trail], axis=1)
    sink = jnp.concatenate([lead, sin32, trail], axis=1)
    cost = (jnp.concatenate([lead + 1.0, cos32, trail], axis=1) * SCALE_LOG2).T
    sint = (sink * SCALE_LOG2).T

    row_spec = lambda w: pl.BlockSpec((tl, w), lambda i: (i, 0))
    out_shape = (
        jax.ShapeDtypeStruct((b, MLA_HEADS, LANES, l), _BF),
        jax.ShapeDtypeStruct((t, MLA_HEADS * LANES), _BF),
        jax.ShapeDtypeStruct((b, D_ATTN, l), _BF),
        jax.ShapeDtypeStruct((t, KV_LORA), _F32),
        jax.ShapeDtypeStruct((t, QK_ROPE), _F32),
        jax.ShapeDtypeStruct((t, D_SSM), _F32),
        jax.ShapeDtypeStruct((t, CONV_DIM), _F32),
        jax.ShapeDtypeStruct((t, LANES), _F32),
    )
    out_specs = (
        pl.BlockSpec((1, MLA_HEADS, LANES, tl), lambda i: (i // n_l, 0, 0, i % n_l)),
        row_spec(MLA_HEADS * LANES),
        pl.BlockSpec((1, D_ATTN, tl), lambda i: (i // n_l, 0, i % n_l)),
        row_spec(KV_LORA), row_spec(QK_ROPE), row_spec(D_SSM), row_spec(CONV_DIM), row_spec(LANES),
    )
    in_specs = [
        row_spec(D_MODEL), _const_spec((1, D_MODEL)), _const_spec((D_MODEL, _N_IN_PAD)),
        _const_spec((1, Q_LORA)), _const_spec((1, KV_LORA)), _const_spec((2 * MLA_HEADS * LANES, Q_LORA)),
        _const_spec((KV_LORA, MLA_HEADS * LANES)), _const_spec((D_ATTN, KV_LORA)),
        pl.BlockSpec((LANES, tl), lambda i: (0, i % n_l)), pl.BlockSpec((LANES, tl), lambda i: (0, i % n_l)),
        pl.BlockSpec((tl, LANES), lambda i: (i % n_l, 0)), pl.BlockSpec((tl, LANES), lambda i: (i % n_l, 0)),
    ]
    return pl.pallas_call(
        _in_proj_prompt_kernel,
        grid=(t // tl,), in_specs=in_specs, out_specs=out_specs, out_shape=out_shape,
        compiler_params=pltpu.CompilerParams(dimension_semantics=("arbitrary",), vmem_limit_bytes=VMEM_LIMIT),
        name="in_proj_prompt",
    )(x2d, p["g_attn"], p["w_in_prompt"], p["q_norm"], p["kv_norm"], p["w_uq_t"], p["w_k"], p["w_uv_t"],
      cost, sint, cosk, sink)


def _attn_prompt_kernel(it_ref, jt_ref, qt_ref, kh_ref, vt_ref, o_ref, m_ref, l_ref, acc_ref, *, tq):
    s_idx = pl.program_id(1)
    i = it_ref[s_idx]
    j = jt_ref[s_idx]

    @pl.when(j == 0)
    def _():
        m_ref[...] = jnp.full(m_ref.shape, -jnp.inf, _F32)
        l_ref[...] = jnp.zeros(l_ref.shape, _F32)
        acc_ref[...] = jnp.zeros(acc_ref.shape, _F32)

    def step(masked):
        if masked:
            krow = lax.broadcasted_iota(jnp.int32, (tq, tq), 0)
            qcol = lax.broadcasted_iota(jnp.int32, (tq, tq), 1)
            keep = krow <= qcol

        def scores(hd):
            st = _dot(kh_ref[:, hd * LANES:(hd + 1) * LANES], qt_ref[0, hd])
            if masked:
                st = jnp.where(keep, st, -jnp.inf)
            return st.reshape(tq // SUBLANES, SUBLANES, tq)

        ahead = {hd: scores(hd) for hd in range(QK_AHEAD)}
        for hd in range(MLA_HEADS):
            if hd + QK_AHEAD < MLA_HEADS:
                ahead[hd + QK_AHEAD] = scores(hd + QK_AHEAD)
            st = ahead.pop(hd)
            m_prev = m_ref[hd]
            m_new = jnp.maximum(m_prev, _sublane_all(jnp.maximum, jnp.max(st, axis=0)))
            alpha = jnp.exp2(m_prev - m_new)
            pr = jnp.exp2(st - m_new[None])
            l_ref[hd] = alpha * l_ref[hd] + jnp.sum(pr, axis=0)
            rows = slice(hd * V_HEAD, (hd + 1) * V_HEAD)
            acc = acc_ref[rows, :].reshape(V_HEAD // SUBLANES, SUBLANES, tq) * alpha[None]
            acc_ref[rows, :] = acc.reshape(V_HEAD, tq) + _dot(vt_ref[0, rows, :], pr.reshape(tq, tq).astype(_BF))
            m_ref[hd] = m_new

    @pl.when(j < i)
    def _():
        step(False)

    @pl.when(j == i)
    def _():
        step(True)
        for hd in range(MLA_HEADS):
            rows = slice(hd * V_HEAD, (hd + 1) * V_HEAD)
            den = _sublane_all(jnp.add, l_ref[hd])
            acc = acc_ref[rows, :].reshape(V_HEAD // SUBLANES, SUBLANES, tq) / den[None]
            acc_ref[rows, :] = acc.reshape(V_HEAD, tq)
        o_ref[...] = acc_ref[...].T.astype(o_ref.dtype)


def _attn_prompt(qt, kh, vt, b, l, tq):
    nq = l // tq
    pairs = [(i, j) for i in range(nq) for j in range(i + 1)]
    it = jnp.asarray([pr[0] for pr in pairs], jnp.int32)
    jt = jnp.asarray([pr[1] for pr in pairs], jnp.int32)
    grid_spec = pltpu.PrefetchScalarGridSpec(
        num_scalar_prefetch=2,
        grid=(b, len(pairs)),
        in_specs=[
            pl.BlockSpec((1, MLA_HEADS, LANES, tq), lambda bi, s, it, jt: (bi, 0, 0, it[s])),
            pl.BlockSpec((tq, MLA_HEADS * LANES), lambda bi, s, it, jt: (bi * nq + jt[s], 0)),
            pl.BlockSpec((1, D_ATTN, tq), lambda bi, s, it, jt: (bi, 0, jt[s])),
        ],
        out_specs=pl.BlockSpec((tq, D_ATTN), lambda bi, s, it, jt: (bi * nq + it[s], 0)),
        scratch_shapes=[pltpu.VMEM((MLA_HEADS, SUBLANES, tq), _F32), pltpu.VMEM((MLA_HEADS, SUBLANES, tq), _F32),
                        pltpu.VMEM((D_ATTN, tq), _F32)],
    )
    return pl.pallas_call(
        functools.partial(_attn_prompt_kernel, tq=tq),
        grid_spec=grid_spec,
        out_shape=jax.ShapeDtypeStruct((b * l, D_ATTN), _BF),
        compiler_params=pltpu.CompilerParams(dimension_semantics=("arbitrary", "arbitrary"),
                                             vmem_limit_bytes=VMEM_LIMIT),
        name="attn_prompt",
    )(it, jt, qt, kh, vt)


def _page_copies(pt_ref, cache_c_ref, cache_krt_ref, cbuf_ref, krbuf_ref, sem_ref, item, slot, pg, page):
    pid = pt_ref[item, pg]
    dst = pl.ds(pl.multiple_of(pg * page, page), page)
    return (pltpu.make_async_copy(cache_c_ref.at[pid], cbuf_ref.at[slot, dst, :], sem_ref.at[slot, 0]),
            pltpu.make_async_copy(cache_krt_ref.at[pid], krbuf_ref.at[slot, :, dst], sem_ref.at[slot, 1]))


def _wait_item(cbuf_ref, krbuf_ref, sem_ref, slot):
    pltpu.make_async_copy(cbuf_ref.at[slot], cbuf_ref.at[slot], sem_ref.at[slot, 0]).wait()
    pltpu.make_async_copy(krbuf_ref.at[slot], krbuf_ref.at[slot], sem_ref.at[slot, 1]).wait()


def _attend_item(q, kvn, slot, cbuf_ref, krbuf_ref, kc_ref, *, n_pages, page, l_new, n_split):
    m_rows = MLA_HEADS * l_new
    q_abs = q[:, :KV_LORA]
    q_pe = q[:, KV_LORA:KV_LORA + QK_ROPE]
    ks = (n_pages // n_split) * page

    def scores(sp):
        rows = pl.ds(sp * ks, ks)
        kc_ref[rows, :] = cbuf_ref[slot, rows, :].astype(_BF)
        return _dot_nt(q_abs, kc_ref[rows, :]) + _dot(q_pe, krbuf_ref[slot, :, rows].astype(_BF))

    def partial(s, values):
        m = jnp.max(s, axis=1, keepdims=True)
        pr = jnp.exp2(s - m)
        return m, jnp.sum(pr, axis=1, keepdims=True), _dot(pr.astype(_BF), values)

    def merge(run, new):
        m = jnp.maximum(run[0], new[0])
        a, c = jnp.exp2(run[0] - m), jnp.exp2(new[0] - m)
        return m, a * run[1] + c * new[1], a * run[2] + c * new[2]

    ahead = {sp: scores(sp) for sp in range(min(SPLITS_AHEAD, n_split))}
    sn = _dot_nt(q, kvn)
    qpos = lax.broadcasted_iota(jnp.int32, (MLA_HEADS, l_new, l_new), 1).reshape(m_rows, l_new)
    kpos = lax.broadcasted_iota(jnp.int32, (m_rows, l_new), 1)
    run = partial(jnp.where(kpos <= qpos, sn, -jnp.inf), kvn[:, :KV_LORA])
    for sp in range(n_split):
        if sp + SPLITS_AHEAD < n_split:
            ahead[sp + SPLITS_AHEAD] = scores(sp + SPLITS_AHEAD)
        run = merge(run, partial(ahead.pop(sp), kc_ref[pl.ds(sp * ks, ks), :]))
    return run[2] / run[1]


def _uv_proj_kernel(o_ref, wuv_ref, out_ref, *, tb, l_new):
    for hd in range(MLA_HEADS):
        o = o_ref[:, hd].reshape(tb * l_new, KV_LORA)
        out_ref[:, hd * V_HEAD:(hd + 1) * V_HEAD] = _dot(o, wuv_ref[hd]).astype(out_ref.dtype)


def _uv_proj(o_lat, wuv, tb):
    b, _, l_new, _ = o_lat.shape
    return pl.pallas_call(
        functools.partial(_uv_proj_kernel, tb=tb, l_new=l_new),
        grid=(b // tb,),
        in_specs=[pl.BlockSpec((tb, MLA_HEADS, l_new, KV_LORA), lambda i: (i, 0, 0, 0)),
                  _const_spec((MLA_HEADS, KV_LORA, V_HEAD))],
        out_specs=pl.BlockSpec((tb * l_new, D_ATTN), lambda i: (i, 0)),
        out_shape=jax.ShapeDtypeStruct((b * l_new, D_ATTN), _BF),
        compiler_params=pltpu.CompilerParams(dimension_semantics=("arbitrary",), vmem_limit_bytes=VMEM_LIMIT),
        name="uv_proj",
    )(o_lat, wuv)


_XP_OFF = 8


def _ssd_kernel(xbc_ref, z_ref, dtr_ref, cbuf_ref, h0_ref, cw_ref, cb_ref, dtb_ref, alog_ref, dskip_ref, nrm_ref,
                sel_ref, ex_ref, exw_ref, o_ref, h_ref, cnew_ref, xp_ref, y_ref, *, g_items, q):
    c_idx = pl.program_id(1)
    last = pl.num_programs(1) - 1
    heads_per_group = SSM_HEADS // SSM_GROUPS
    rows = g_items * q
    row = lax.broadcasted_iota(jnp.int32, (q, q), 0)
    col = lax.broadcasted_iota(jnp.int32, (q, q), 1)
    causal = row >= col

    def conv_item(g):
        @pl.when(c_idx == 0)
        def _():
            xp_ref[g, _XP_OFF - (CONV_K - 1):_XP_OFF, :] = cbuf_ref[g]
            h_ref[g] = h0_ref[g]

        xp_ref[g, _XP_OFF:_XP_OFF + q, :] = xbc_ref[g * q:(g + 1) * q, :]
        conv = cb_ref[...]
        for k in range(CONV_K):
            conv = conv + xp_ref[g, _XP_OFF - (CONV_K - 1) + k:_XP_OFF - (CONV_K - 1) + k + q, :] * cw_ref[k:k + 1, :]
        conv = _silu(conv)
        tail = xp_ref[g, _XP_OFF + q - (CONV_K - 1):_XP_OFF + q, :]
        xp_ref[g, _XP_OFF - (CONV_K - 1):_XP_OFF, :] = tail

        @pl.when(c_idx == last)
        def _():
            cnew_ref[g] = tail

        return (conv[:, :D_SSM], conv[:, D_SSM:D_SSM + SSM_GROUPS * D_STATE].astype(_BF),
                conv[:, D_SSM + SSM_GROUPS * D_STATE:].astype(_BF))

    convs = {0: conv_item(0)} if g_items == 1 else {}

    if g_items == 1:
        tril = jnp.where(causal, 1.0, 0.0).astype(_BF)
    else:
        r_all = lax.broadcasted_iota(jnp.int32, (rows, rows), 0)
        c_all = lax.broadcasted_iota(jnp.int32, (rows, rows), 1)
        same_item = (r_all // q) == (c_all // q)
        tril = jnp.where(r_all >= c_all, jnp.where(same_item, 1.0, 0.0), 0.0).astype(_BF)
    a_neg = -jnp.exp(alog_ref[...])
    dtv = dtr_ref[...] + dtb_ref[...]
    dt = jnp.maximum(dtv, 0.0) + jnp.log1p(jnp.exp(-jnp.abs(dtv)))
    cs = _dot(tril, _split3(dt * a_neg))
    acs = cs[:, :LANES] + cs[:, LANES:2 * LANES] + cs[:, 2 * LANES:]
    acs3 = _split3(acs)
    acs_t = _dot_nt(sel_ref[...], acs3)
    dt_x = _dot(_split3(dt), ex_ref[...])
    acs_x = _dot(acs3, ex_ref[...])
    acs_w = _dot(acs3, exw_ref[...])
    e_acs = jnp.exp(acs_x)

    for g in range(g_items):
        rs = slice(g * q, (g + 1) * q)
        xs, bmat, cmat = convs[g] if g in convs else conv_item(g)
        acs_last_x = acs_x[(g + 1) * q - 1:(g + 1) * q, :]
        x_dt = xs * dt_x[rs]
        xw = (x_dt * jnp.exp(acs_last_x - acs_x[rs])).astype(_BF)
        x_dt = x_dt.astype(_BF)
        for grp in range(SSM_GROUPS):
            b_g = bmat[:, grp * D_STATE:(grp + 1) * D_STATE]
            c_g = cmat[:, grp * D_STATE:(grp + 1) * D_STATE]
            cb = _dot_nt(c_g, b_g)
            for hd in range(grp * heads_per_group, (grp + 1) * heads_per_group):
                cols = slice(hd * SSM_HEADDIM, (hd + 1) * SSM_HEADDIM)
                acs_h = acs_w[rs, hd * LANES:(hd + 1) * LANES]
                acs_h = acs_h[:, :q] if q <= LANES else jnp.concatenate([acs_h] * (q // LANES), axis=1)
                lmat = jnp.exp(jnp.where(causal, acs_h - acs_t[hd:hd + 1, rs], -jnp.inf))
                h_prev = h_ref[g, hd]
                y = _dot((cb * lmat).astype(_BF), x_dt[:, cols])
                y = y + _dot_nt(c_g, h_prev.astype(_BF)) * e_acs[rs, cols]
                decay = jnp.exp(acs[(g + 1) * q - 1:(g + 1) * q, hd:hd + 1])
                h_ref[g, hd] = decay * h_prev + _dot_tn(xw[:, cols], b_g)
                y_ref[:, cols] = y
        yv = y_ref[...] + dskip_ref[...] * xs
        gated = yv * _silu(z_ref[g * q:(g + 1) * q, :])
        gw = D_SSM // SSM_GROUPS
        parts = []
        for grp in range(SSM_GROUPS):
            gg = gated[:, grp * gw:(grp + 1) * gw]
            parts.append(gg * lax.rsqrt(jnp.mean(gg * gg, axis=-1, keepdims=True) + EPS))
        o_ref[g * q:(g + 1) * q, :] = (jnp.concatenate(parts, axis=1) * nrm_ref[...]).astype(o_ref.dtype)


def _ssd(xbc, z, dtr, conv_buf, h0, p, b, l, q, g_items):
    nc = l // q
    rows = g_items * q
    row_spec = lambda w: pl.BlockSpec((rows, w), lambda bi, c: (bi * nc + c, 0))
    in_specs = [
        row_spec(CONV_DIM), row_spec(D_SSM), row_spec(LANES),
        pl.BlockSpec((g_items, CONV_K - 1, CONV_DIM), lambda bi, c: (bi, 0, 0)),
        pl.BlockSpec((g_items, SSM_HEADS, SSM_HEADDIM, D_STATE), lambda bi, c: (bi, 0, 0, 0)),
        _const_spec((CONV_K, CONV_DIM)), _const_spec((1, CONV_DIM)), _const_spec((1, LANES)), _const_spec((1, LANES)),
        _const_spec((1, D_SSM)), _const_spec((1, D_SSM)), _const_spec((SSM_HEADS, 3 * LANES)),
        _const_spec((3 * LANES, D_SSM)), _const_spec((3 * LANES, SSM_HEADS * LANES)),
    ]
    out_specs = (
        row_spec(D_SSM),
        pl.BlockSpec((g_items, SSM_HEADS, SSM_HEADDIM, D_STATE), lambda bi, c: (bi, 0, 0, 0)),
        pl.BlockSpec((g_items, CONV_K - 1, CONV_DIM), lambda bi, c: (bi, 0, 0)),
    )
    out_shape = (
        jax.ShapeDtypeStruct((b * l, D_SSM), _BF),
        jax.ShapeDtypeStruct((b, SSM_HEADS, SSM_HEADDIM, D_STATE), _F32),
        jax.ShapeDtypeStruct((b, CONV_K - 1, CONV_DIM), _F32),
    )
    return pl.pallas_call(
        functools.partial(_ssd_kernel, g_items=g_items, q=q),
        grid=(b // g_items, nc), in_specs=in_specs, out_specs=out_specs, out_shape=out_shape,
        scratch_shapes=[pltpu.VMEM((g_items, _XP_OFF + q, CONV_DIM), _F32), pltpu.VMEM((q, D_SSM), _F32)],
        compiler_params=pltpu.CompilerParams(dimension_semantics=("arbitrary", "arbitrary"),
                                             vmem_limit_bytes=VMEM_LIMIT),
        name="ssd",
    )(xbc, z, dtr, conv_buf, h0, p["conv_w"], p["conv_b"], p["dt_bias"], p["a_log"], p["d_skip"], p["ssm_norm"],
      p["sel"], p["head_to_cols"], p["head_to_lanes"])


def _mlp_chunk(h2, acc, wu_ref, wd_ref, f0, f_chunk):
    u = jnp.maximum(_dot(h2, wu_ref[:, f0:f0 + f_chunk]), 0.0)
    return acc + _dot((u * u).astype(_BF), wd_ref[f0:f0 + f_chunk, :])


def _out_mlp_kernel(x_ref, oa_ref, os_ref, wo_ref, gm_ref, wu_ref, wd_ref, gf_ref, y_ref, *, f_chunk):
    mix = jnp.concatenate([oa_ref[...], os_ref[...]], axis=1)
    x1 = x_ref[...] + _dot(mix, wo_ref[...])
    h2 = _rms(x1, gm_ref[...]).astype(_BF)
    acc = x1
    for f0 in range(0, D_FF, f_chunk):
        acc = _mlp_chunk(h2, acc, wu_ref, wd_ref, f0, f_chunk)
    y_ref[...] = _rms(acc, gf_ref[...])


def _mlp_attn_kernel(pt_ref, x_ref, oa_ref, os_ref, wo_ref, gm_ref, wu_ref, wd_ref, gf_ref, q_ref, kvn_ref,
                     cache_c_ref, cache_krt_ref, y_ref, o_ref, cbuf_ref, krbuf_ref, kc_ref, sem_ref, *,
                     f_chunk, ipb, n_items, n_pages, page, l_new, n_split):
    i = pl.program_id(0)
    copies = functools.partial(_page_copies, pt_ref, cache_c_ref, cache_krt_ref, cbuf_ref, krbuf_ref, sem_ref)

    @pl.when(i == 0)
    def _():
        def body(pg, carry):
            for cp in copies(0, 0, pg, page):
                cp.start()
            return carry
        lax.fori_loop(0, n_pages, body, 0)

    mix = jnp.concatenate([oa_ref[...], os_ref[...]], axis=1)
    x1 = x_ref[...] + _dot(mix, wo_ref[...])
    h2 = _rms(x1, gm_ref[...]).astype(_BF)
    acc = x1
    f_starts = list(range(0, D_FF, f_chunk))
    for k in range(max(ipb, len(f_starts))):
        if k < ipb:
            item = i * ipb + k
            slot = item % 2
            _wait_item(cbuf_ref, krbuf_ref, sem_ref, slot)
            nxt = jnp.minimum(item + 1, n_items - 1)
            for pg in range(n_pages):
                for cp in copies(nxt, 1 - slot, pg, page):
                    cp.start()
            o = _attend_item(q_ref[k], kvn_ref[k], slot, cbuf_ref, krbuf_ref, kc_ref, n_pages=n_pages, page=page,
                             l_new=l_new, n_split=n_split)
            o_ref[k] = o.astype(o_ref.dtype)
        if k < len(f_starts):
            acc = _mlp_chunk(h2, acc, wu_ref, wd_ref, f_starts[k], f_chunk)
    y_ref[...] = _rms(acc, gf_ref[...])

    @pl.when(i == pl.num_programs(0) - 1)
    def _():
        _wait_item(cbuf_ref, krbuf_ref, sem_ref, n_items % 2)


def _mlp_attn(x2d, oa, osm, q, kvn, cache_c, cache_krt, page_table, p, tm):
    t = x2d.shape[0]
    n_steps = t // tm
    b, m_rows, _ = q.shape
    assert b % n_steps == 0, "sample batch must split evenly over the MLP token tiles"
    ipb = b // n_steps
    l_new = m_rows // MLA_HEADS
    n_pages = page_table.shape[1]
    page = cache_c.shape[1]
    keys = n_pages * page
    row_spec = lambda w: pl.BlockSpec((tm, w), lambda i, pt: (i, 0))
    once = lambda shape: pl.BlockSpec(shape, lambda i, pt: (0,) * len(shape), pipeline_mode=pl.Buffered(1))
    item_spec = lambda r, w: pl.BlockSpec((ipb, r, w), lambda i, pt: (i, 0, 0))
    grid_spec = pltpu.PrefetchScalarGridSpec(
        num_scalar_prefetch=1,
        grid=(n_steps,),
        in_specs=[row_spec(D_MODEL), row_spec(D_ATTN), row_spec(D_SSM), once((D_MODEL, D_MODEL)),
                  once((1, D_MODEL)), once((D_MODEL, D_FF)), once((D_FF, D_MODEL)), once((1, D_MODEL)),
                  item_spec(m_rows, QK_PAD), item_spec(l_new, QK_PAD),
                  pl.BlockSpec(memory_space=pl.ANY), pl.BlockSpec(memory_space=pl.ANY)],
        out_specs=(row_spec(D_MODEL), item_spec(m_rows, KV_LORA)),
        scratch_shapes=[pltpu.VMEM((2, keys, KV_LORA), _F32), pltpu.VMEM((2, QK_ROPE, keys), _F32),
                        pltpu.VMEM((keys, KV_LORA), _BF), pltpu.SemaphoreType.DMA((2, 2))],
    )
    return pl.pallas_call(
        functools.partial(_mlp_attn_kernel, f_chunk=1024, ipb=ipb, n_items=b, n_pages=n_pages, page=page,
                          l_new=l_new, n_split=_largest_divisor(n_pages, 8)),
        grid_spec=grid_spec,
        out_shape=(jax.ShapeDtypeStruct((t, D_MODEL), _F32), jax.ShapeDtypeStruct((b, m_rows, KV_LORA), _BF)),
        compiler_params=pltpu.CompilerParams(dimension_semantics=("arbitrary",), vmem_limit_bytes=VMEM_LIMIT),
        name="mlp_attn",
    )(page_table, x2d, oa, osm, p["w_out"], p["g_mlp"], p["w_up"], p["w_down"], p["g_final"], q, kvn, cache_c,
      cache_krt)


def _out_mlp(x2d, oa, osm, p, tm):
    t = x2d.shape[0]
    row_spec = lambda w: pl.BlockSpec((tm, w), lambda i: (i, 0))
    once = lambda shape: pl.BlockSpec(shape, lambda i: (0,) * len(shape), pipeline_mode=pl.Buffered(1))
    return pl.pallas_call(
        functools.partial(_out_mlp_kernel, f_chunk=1024),
        grid=(t // tm,),
        in_specs=[row_spec(D_MODEL), row_spec(D_ATTN), row_spec(D_SSM), once((D_MODEL, D_MODEL)),
                  once((1, D_MODEL)), once((D_MODEL, D_FF)), once((D_FF, D_MODEL)), once((1, D_MODEL))],
        out_specs=row_spec(D_MODEL),
        out_shape=jax.ShapeDtypeStruct((t, D_MODEL), _F32),
        compiler_params=pltpu.CompilerParams(dimension_semantics=("arbitrary",), vmem_limit_bytes=VMEM_LIMIT),
        name="out_mlp",
    )(x2d, oa, osm, p["w_out"], p["g_mlp"], p["w_up"], p["w_down"], p["g_final"])


def _prep_params(norm_attn, w_in, q_norm, kv_norm, w_uq, w_uk, w_uv, conv_w, conv_b, dt_bias, a_log, d_skip,
                 ssm_norm, w_out, norm_mlp, w_up, w_down, norm_final):
    half = QK_ROPE // 2

    def swap(w):
        return jnp.concatenate([w[..., half:], w[..., :half]], axis=-1)

    def place(w, lead, n):
        return jnp.pad(w, ((0, 0), (lead, n - lead - w.shape[1])))

    o1 = Q_LORA
    o2 = o1 + KV_LORA
    o3 = o2 + QK_ROPE
    o5 = o3 + D_SSM + CONV_DIM
    w_kpe = w_in[:, o2:o3]

    def w_in_padded(lead):
        return jnp.concatenate([
            w_in[:, :o2], place(w_kpe, lead, LANES), place(swap(w_kpe), lead, LANES), w_in[:, o3:o5],
            place(w_in[:, o5:], 0, LANES)], axis=1).astype(_BF)

    uq = w_uq.reshape(Q_LORA, MLA_HEADS, QK_NOPE + QK_ROPE)
    uq_nope = uq[:, :, :QK_NOPE]
    uq_rope = uq[:, :, QK_NOPE:]
    w_uq_p = jnp.concatenate([
        jnp.pad(uq_nope, ((0, 0), (0, 0), (0, LANES - QK_NOPE))).reshape(Q_LORA, MLA_HEADS * LANES),
        uq_rope.reshape(Q_LORA, -1), swap(uq_rope).reshape(Q_LORA, -1)], axis=1).astype(_BF)
    w_uk_p = jnp.pad(w_uk, ((0, 0), (0, LANES - QK_NOPE), (0, 0))).astype(_BF)
    tail = LANES - QK_NOPE - QK_ROPE
    plain = jnp.pad(uq, ((0, 0), (0, 0), (0, tail)))
    swapped = jnp.pad(swap(uq_rope), ((0, 0), (0, 0), (QK_NOPE, tail)))
    w_uq_t = jnp.concatenate([plain.reshape(Q_LORA, -1), swapped.reshape(Q_LORA, -1)], axis=1).T.astype(_BF)
    w_k = jnp.pad(jnp.transpose(w_uk, (2, 0, 1)), ((0, 0), (0, 0), (0, LANES - QK_NOPE)))
    w_k = w_k.reshape(KV_LORA, MLA_HEADS * LANES).astype(_BF)
    w_uv_t = jnp.transpose(w_uv, (0, 2, 1)).reshape(D_ATTN, KV_LORA).astype(_BF)
    pad_heads = lambda v: jnp.pad(v.reshape(1, SSM_HEADS), ((0, 0), (0, LANES - SSM_HEADS)))
    return dict(
        g_attn=norm_attn.reshape(1, D_MODEL), w_in=w_in_padded(0), w_in_prompt=w_in_padded(QK_NOPE),
        q_norm=q_norm.reshape(1, Q_LORA), kv_norm=kv_norm.reshape(1, KV_LORA), w_uq=w_uq_p, w_uk=w_uk_p,
        w_uq_t=w_uq_t, w_k=w_k, w_uv_t=w_uv_t, w_uv=w_uv.astype(_BF),
        conv_w=conv_w, conv_b=conv_b.reshape(1, CONV_DIM), dt_bias=pad_heads(dt_bias), a_log=pad_heads(a_log),
        d_skip=jnp.repeat(d_skip, SSM_HEADDIM).reshape(1, D_SSM), ssm_norm=ssm_norm.reshape(1, D_SSM),
        sel=jnp.tile(jnp.eye(SSM_HEADS, LANES, dtype=_BF), (1, 3)),
        head_to_cols=jnp.tile(jnp.repeat(jnp.eye(LANES, SSM_HEADS, dtype=_BF), SSM_HEADDIM, axis=1), (3, 1)),
        head_to_lanes=jnp.tile(jnp.repeat(jnp.eye(LANES, SSM_HEADS, dtype=_BF), LANES, axis=1), (3, 1)),
        w_out=w_out.astype(_BF), g_mlp=norm_mlp.reshape(1, D_MODEL), w_up=w_up.astype(_BF),
        w_down=w_down.astype(_BF), g_final=norm_final.reshape(1, D_MODEL),
    )


def _largest_divisor(n, cap):
    d = min(n, cap)
    while n % d:
        d -= 1
    return d


def kernel(x_prompt, x_sample, cache_kv_latent, cache_k_rope, state_ssm, state_conv, page_table, norm_attn, w_in, q_norm, kv_norm, w_uq, w_uk, w_uv, conv_w, conv_b, dt_bias, a_log, d_skip, ssm_norm, w_out, norm_mlp, w_up, w_down, norm_final):
    assert norm_attn.shape[0] == 1, "single-layer model"
    p = _prep_params(norm_attn[0], w_in[0], q_norm[0], kv_norm[0], w_uq[0], w_uk[0], w_uv[0], conv_w[0], conv_b[0],
                     dt_bias[0], a_log[0], d_skip[0], ssm_norm[0], w_out[0], norm_mlp[0], w_up[0], w_down[0],
                     norm_final)
    bp, lp, _ = x_prompt.shape
    bs, ls, _ = x_sample.shape
    n_pages = page_table.shape[1]
    page = cache_kv_latent.shape[2]
    past_len = n_pages * page

    xp2 = x_prompt.reshape(bp * lp, D_MODEL)
    xs2 = x_sample.reshape(bs * ls, D_MODEL)
    qt, kh, vt, c_p, kr_p, z_p, xbc_p, dtr_p = _in_proj_prompt(xp2, bp, lp, p, _largest_divisor(lp, 512))
    tb = _largest_divisor(bs, 512 // ls)
    q_s, kv_s, c_s, kr_s, z_s, xbc_s, dtr_s = _in_proj(xs2, bs, ls, past_len, p, tb, ls)

    o_attn_p = _attn_prompt(qt, kh, vt, bp, lp, _largest_divisor(lp, 512))
    o_ssm_p, h_p, conv_p = _ssd(xbc_p, z_p, dtr_p, jnp.zeros((bp, CONV_K - 1, CONV_DIM), _F32),
                                jnp.zeros((bp, SSM_HEADS, SSM_HEADDIM, D_STATE), _F32), p, bp, lp,
                                _largest_divisor(lp, CHUNK), 1)

    cache_krt = jnp.swapaxes(cache_k_rope[0], 1, 2)
    y_p, o_lat = _mlp_attn(xp2, o_attn_p, o_ssm_p, q_s.reshape(bs, MLA_HEADS * ls, QK_PAD),
                           kv_s.reshape(bs, ls, QK_PAD), cache_kv_latent[0], cache_krt, page_table, p,
                           _largest_divisor(bp * lp, 256))
    outs_p = (y_p.reshape(bp, lp, D_MODEL), c_p.reshape(1, bp, lp, KV_LORA), kr_p.reshape(1, bp, lp, QK_ROPE),
              h_p[None], conv_p[None])

    o_attn_s = _uv_proj(o_lat.reshape(bs, MLA_HEADS, ls, KV_LORA), p["w_uv"], tb)
    o_ssm_s, h_s, conv_s = _ssd(xbc_s, z_s, dtr_s, state_conv[0], state_ssm[0], p, bs, ls, ls,
                                _largest_divisor(bs, 8))
    y_s = _out_mlp(xs2, o_attn_s, o_ssm_s, p, _largest_divisor(bs * ls, 512))
    outs_s = (y_s.reshape(bs, ls, D_MODEL), c_s.reshape(1, bs, ls, KV_LORA), kr_s.reshape(1, bs, ls, QK_ROPE),
              h_s[None], conv_s[None])

    return (outs_p[0], outs_s[0], outs_p[1], outs_p[2], outs_p[3], outs_p[4],
            outs_s[1], outs_s[2], outs_s[3], outs_s[4])
```

```python
import functools

import jax
import jax.numpy as jnp
from jax import lax
from jax.experimental import pallas as pl
from jax.experimental.pallas import tpu as pltpu

D_MODEL = 1024
D_ATTN = 512
D_SSM = 512
V_HEAD = 64
MLA_HEADS = 8
QK_NOPE = 64
QK_ROPE = 32
KV_LORA = 256
Q_LORA = 384
ROPE_THETA = 10000.0
SSM_HEADDIM = 64
SSM_HEADS = 8
SSM_GROUPS = 2
D_STATE = 128
CONV_K = 4
CHUNK = 256
CONV_DIM = D_SSM + 2 * SSM_GROUPS * D_STATE
D_FF = 4096
EPS = 1e-6

LANES = 128
SUBLANES = 8
QK_AHEAD = 2
SPLITS_AHEAD = 4
QK_PAD = KV_LORA + LANES
VMEM_LIMIT = 56 * 1024 * 1024
SCALE_LOG2 = (QK_NOPE + QK_ROPE) ** -0.5 * 1.4426950408889634

_C_QLAT = 0
_C_KV = _C_QLAT + Q_LORA
_C_KPE = _C_KV + KV_LORA
_C_KPES = _C_KPE + LANES
_C_Z = _C_KPES + LANES
_C_XBC = _C_Z + D_SSM
_C_DT = _C_XBC + CONV_DIM
_N_IN_PAD = _C_DT + LANES
_Q_NOPE = 0
_Q_ROPE = MLA_HEADS * LANES
_Q_ROPES = _Q_ROPE + MLA_HEADS * QK_ROPE
_N_UQ_PAD = _Q_ROPES + MLA_HEADS * QK_ROPE

_BF = jnp.bfloat16
_F32 = jnp.float32


def _dot(a, b):
    return jnp.dot(a, b, preferred_element_type=_F32)


def _dot_nt(a, b):
    return lax.dot_general(a, b, (((1,), (1,)), ((), ())), preferred_element_type=_F32)


def _dot_tn(a, b):
    return lax.dot_general(a, b, (((0,), (0,)), ((), ())), preferred_element_type=_F32)


def _split3(a):
    hi = a.astype(_BF)
    r1 = a - hi.astype(_F32)
    mid = r1.astype(_BF)
    lo = (r1 - mid.astype(_F32)).astype(_BF)
    return jnp.concatenate([hi, mid, lo], axis=1)


def _rms(x, g):
    return x * lax.rsqrt(jnp.mean(x * x, axis=-1, keepdims=True) + EPS) * g


def _silu(x):
    return x * (1.0 / (1.0 + jnp.exp(-x)))


def _sublane_all(op, x):
    for shift in (4, 2, 1):
        x = op(x, pltpu.roll(x, shift, 0))
    return x


def _const_spec(shape):
    nd = len(shape)
    return pl.BlockSpec(shape, lambda *_: (0,) * nd)


def _rope_tables(l, past_len):
    pos = (past_len + jnp.arange(l, dtype=jnp.int32)).astype(_F32)
    inv = ROPE_THETA ** (-(jnp.arange(0, QK_ROPE, 2, dtype=_F32) / QK_ROPE))
    ang = pos[:, None] * inv[None, :]
    cos, sin = jnp.cos(ang), jnp.sin(ang)
    return jnp.concatenate([cos, cos], axis=1), jnp.concatenate([-sin, sin], axis=1)


_XP_OFF = 8


def _conv_silu(win_ref, x_new, cw_ref, cb_ref, q):
    win_ref[_XP_OFF:_XP_OFF + q, :] = x_new
    conv = cb_ref[...]
    for k in range(CONV_K):
        conv = conv + win_ref[_XP_OFF - (CONV_K - 1) + k:_XP_OFF - (CONV_K - 1) + k + q, :] * cw_ref[k:k + 1, :]
    tail = win_ref[_XP_OFF + q - (CONV_K - 1):_XP_OFF + q, :]
    win_ref[_XP_OFF - (CONV_K - 1):_XP_OFF, :] = tail
    return _silu(conv), tail


def _in_proj_kernel(x_ref, g_ref, w_ref, qn_ref, kvn_ref, wuq_ref, wuk_ref, cosq_ref, sinq_ref, cosk_ref, sink_ref,
                    q_ref, kv_ref, c_ref, kr_ref, z_ref, xbc_ref, dt_ref, *, tb, tl):
    tm = tb * tl

    def rows(tab_ref):
        t = tab_ref[...]
        if tb == 1:
            return t
        return jnp.broadcast_to(t[None], (tb, tl, t.shape[-1])).reshape(tm, t.shape[-1])

    h = _rms(x_ref[...], g_ref[...]).astype(_BF)
    proj = _dot(h, w_ref[...])
    z_ref[...] = proj[:, _C_Z:_C_XBC]
    xbc_ref[...] = proj[:, _C_XBC:_C_DT]
    dt_ref[...] = proj[:, _C_DT:_N_IN_PAD]

    c = _rms(proj[:, _C_KV:_C_KPE], kvn_ref[...])
    kr = proj[:, _C_KPE:_C_KPES] * rows(cosk_ref) + proj[:, _C_KPES:_C_Z] * rows(sink_ref)
    c_ref[...] = c
    kr_ref[...] = kr[:, :QK_ROPE]
    kv_ref[:, :KV_LORA] = c.astype(_BF)
    kv_ref[:, KV_LORA:] = kr.astype(_BF)

    qn = _rms(proj[:, _C_QLAT:_C_KV], qn_ref[...]).astype(_BF)
    qq = _dot(qn, wuq_ref[...])
    q_pe = qq[:, _Q_ROPE:_Q_ROPES] * rows(cosq_ref) + qq[:, _Q_ROPES:_N_UQ_PAD] * rows(sinq_ref)
    lane = lax.broadcasted_iota(jnp.int32, (tm, LANES), 1)
    heads_per_group = LANES // QK_ROPE
    for hd in range(MLA_HEADS):
        q_abs = _dot(qq[:, hd * LANES:(hd + 1) * LANES].astype(_BF), wuk_ref[hd]) * SCALE_LOG2
        grp = q_pe[:, (hd // heads_per_group) * LANES:(hd // heads_per_group + 1) * LANES]
        shift = (LANES - QK_ROPE * (hd % heads_per_group)) % LANES
        if shift:
            grp = pltpu.roll(grp, shift, 1)
        pe = jnp.where(lane < QK_ROPE, grp * SCALE_LOG2, 0.0)
        q_ref[:, hd, :, :KV_LORA] = q_abs.astype(_BF).reshape(tb, tl, KV_LORA)
        q_ref[:, hd, :, KV_LORA:] = pe.astype(_BF).reshape(tb, tl, LANES)


def _in_proj(x2d, b, l, past_len, p, tb, tl):
    t = b * l
    tm = tb * tl
    n_l = l // tl
    cos32, sin32 = _rope_tables(l, past_len)
    padk = jnp.zeros((l, LANES - QK_ROPE), _F32)
    cosk = jnp.concatenate([cos32, padk], axis=1)
    sink = jnp.concatenate([sin32, padk], axis=1)
    cosq = jnp.tile(cos32, (1, MLA_HEADS))
    sinq = jnp.tile(sin32, (1, MLA_HEADS))

    row_spec = lambda w: pl.BlockSpec((tm, w), lambda i: (i, 0))
    tab_spec = lambda w: pl.BlockSpec((tl, w), lambda i: (i % n_l, 0))
    out_shape = (
        jax.ShapeDtypeStruct((b, MLA_HEADS, l, QK_PAD), _BF),
        jax.ShapeDtypeStruct((t, QK_PAD), _BF),
        jax.ShapeDtypeStruct((t, KV_LORA), _F32),
        jax.ShapeDtypeStruct((t, QK_ROPE), _F32),
        jax.ShapeDtypeStruct((t, D_SSM), _F32),
        jax.ShapeDtypeStruct((t, CONV_DIM), _F32),
        jax.ShapeDtypeStruct((t, LANES), _F32),
    )
    out_specs = (
        pl.BlockSpec((tb, MLA_HEADS, tl, QK_PAD), lambda i: (i // n_l, 0, i % n_l, 0)),
        row_spec(QK_PAD), row_spec(KV_LORA), row_spec(QK_ROPE), row_spec(D_SSM), row_spec(CONV_DIM), row_spec(LANES),
    )
    in_specs = [
        row_spec(D_MODEL), _const_spec((1, D_MODEL)), _const_spec((D_MODEL, _N_IN_PAD)),
        _const_spec((1, Q_LORA)), _const_spec((1, KV_LORA)), _const_spec((Q_LORA, _N_UQ_PAD)),
        _const_spec((MLA_HEADS, LANES, KV_LORA)),
        tab_spec(MLA_HEADS * QK_ROPE), tab_spec(MLA_HEADS * QK_ROPE), tab_spec(LANES), tab_spec(LANES),
    ]
    return pl.pallas_call(
        functools.partial(_in_proj_kernel, tb=tb, tl=tl),
        grid=(t // tm,), in_specs=in_specs, out_specs=out_specs, out_shape=out_shape,
        compiler_params=pltpu.CompilerParams(dimension_semantics=("arbitrary",), vmem_limit_bytes=VMEM_LIMIT),
        name="in_proj",
    )(x2d, p["g_attn"], p["w_in"], p["q_norm"], p["kv_norm"], p["w_uq"], p["w_uk"], cosq, sinq, cosk, sink)


def _in_proj_prompt_kernel(x_ref, g_ref, w_ref, qn_ref, kvn_ref, wuqt_ref, wk_ref, wuvt_ref, cost_ref, sint_ref,
                           cosk_ref, sink_ref, qt_ref, kh_ref, vt_ref, c_ref, kr_ref, z_ref, xbc_ref, dt_ref):
    h = _rms(x_ref[...], g_ref[...]).astype(_BF)
    proj = _dot(h, w_ref[...])
    z_ref[...] = proj[:, _C_Z:_C_XBC]
    xbc_ref[...] = proj[:, _C_XBC:_C_DT]
    dt_ref[...] = proj[:, _C_DT:_N_IN_PAD]

    c = _rms(proj[:, _C_KV:_C_KPE], kvn_ref[...])
    kr = proj[:, _C_KPE:_C_KPES] * cosk_ref[...] + proj[:, _C_KPES:_C_Z] * sink_ref[...]
    c_ref[...] = c
    kr_ref[...] = kr[:, QK_NOPE:QK_NOPE + QK_ROPE]
    c_bf = c.astype(_BF)
    kn = _dot(c_bf, wk_ref[...])
    for hd in range(MLA_HEADS):
        kh_ref[:, hd * LANES:(hd + 1) * LANES] = (kn[:, hd * LANES:(hd + 1) * LANES] + kr).astype(_BF)
    vt_ref[0] = _dot_nt(wuvt_ref[...], c_bf).astype(_BF)

    qn = _rms(proj[:, _C_QLAT:_C_KV], qn_ref[...]).astype(_BF)
    qqt = _dot_nt(wuqt_ref[...], qn)
    half = MLA_HEADS * LANES
    for hd in range(MLA_HEADS):
        qt = (qqt[hd * LANES:(hd + 1) * LANES] * cost_ref[...]
              + qqt[half + hd * LANES:half + (hd + 1) * LANES] * sint_ref[...])
        qt_ref[0, hd] = qt.astype(_BF)


def _in_proj_prompt(x2d, b, l, p, tl):
    t = b * l
    n_l = l // tl
    cos32, sin32 = _rope_tables(l, 0)
    lead = jnp.zeros((l, QK_NOPE), _F32)
    trail = jnp.zeros((l, LANES - QK_NOPE - QK_ROPE), _F32)
    cosk = jnp.concatenate([lead, cos32, trail], axis=1)
    sink = jnp.concatenate([lead, sin32, trail], axis=1)
    cost = (jnp.concatenate([lead + 1.0, cos32, trail], axis=1) * SCALE_LOG2).T
    sint = (sink * SCALE_LOG2).T

    row_spec = lambda w: pl.BlockSpec((tl, w), lambda i: (i, 0))
    out_shape = (
        jax.ShapeDtypeStruct((b, MLA_HEADS, LANES, l), _BF),
        jax.ShapeDtypeStruct((t, MLA_HEADS * LANES), _BF),
        jax.ShapeDtypeStruct((b, D_ATTN, l), _BF),
        jax.ShapeDtypeStruct((t, KV_LORA), _F32),
        jax.ShapeDtypeStruct((t, QK_ROPE), _F32),
        jax.ShapeDtypeStruct((t, D_SSM), _F32),
        jax.ShapeDtypeStruct((t, CONV_DIM), _F32),
        jax.ShapeDtypeStruct((t, LANES), _F32),
    )
    out_specs = (
        pl.BlockSpec((1, MLA_HEADS, LANES, tl), lambda i: (i // n_l, 0, 0, i % n_l)),
        row_spec(MLA_HEADS * LANES),
        pl.BlockSpec((1, D_ATTN, tl), lambda i: (i // n_l, 0, i % n_l)),
        row_spec(KV_LORA), row_spec(QK_ROPE), row_spec(D_SSM), row_spec(CONV_DIM), row_spec(LANES),
    )
    in_specs = [
        row_spec(D_MODEL), _const_spec((1, D_MODEL)), _const_spec((D_MODEL, _N_IN_PAD)),
        _const_spec((1, Q_LORA)), _const_spec((1, KV_LORA)), _const_spec((2 * MLA_HEADS * LANES, Q_LORA)),
        _const_spec((KV_LORA, MLA_HEADS * LANES)), _const_spec((D_ATTN, KV_LORA)),
        pl.BlockSpec((LANES, tl), lambda i: (0, i % n_l)), pl.BlockSpec((LANES, tl), lambda i: (0, i % n_l)),
        pl.BlockSpec((tl, LANES), lambda i: (i % n_l, 0)), pl.BlockSpec((tl, LANES), lambda i: (i % n_l, 0)),
    ]
    return pl.pallas_call(
        _in_proj_prompt_kernel,
        grid=(t // tl,), in_specs=in_specs, out_specs=out_specs, out_shape=out_shape,
        compiler_params=pltpu.CompilerParams(dimension_semantics=("arbitrary",), vmem_limit_bytes=VMEM_LIMIT),
        name="in_proj_prompt",
    )(x2d, p["g_attn"], p["w_in_prompt"], p["q_norm"], p["kv_norm"], p["w_uq_t"], p["w_k"], p["w_uv_t"],
      cost, sint, cosk, sink)


def _attn_prompt_kernel(it_ref, jt_ref, qt_ref, kh_ref, vt_ref, o_ref, m_ref, l_ref, acc_ref, *, tq):
    s_idx = pl.program_id(1)
    i = it_ref[s_idx]
    j = jt_ref[s_idx]

    @pl.when(j == 0)
    def _():
        m_ref[...] = jnp.full(m_ref.shape, -jnp.inf, _F32)
        l_ref[...] = jnp.zeros(l_ref.shape, _F32)
        acc_ref[...] = jnp.zeros(acc_ref.shape, _F32)

    def step(masked):
        if masked:
            krow = lax.broadcasted_iota(jnp.int32, (tq, tq), 0)
            qcol = lax.broadcasted_iota(jnp.int32, (tq, tq), 1)
            keep = krow <= qcol

        def scores(hd):
            st = _dot(kh_ref[:, hd * LANES:(hd + 1) * LANES], qt_ref[0, hd])
            if masked:
                st = jnp.where(keep, st, -jnp.inf)
            return st.reshape(tq // SUBLANES, SUBLANES, tq)

        ahead = {hd: scores(hd) for hd in range(QK_AHEAD)}
        for hd in range(MLA_HEADS):
            if hd + QK_AHEAD < MLA_HEADS:
                ahead[hd + QK_AHEAD] = scores(hd + QK_AHEAD)
            st = ahead.pop(hd)
            m_prev = m_ref[hd]
            m_new = jnp.maximum(m_prev, _sublane_all(jnp.maximum, jnp.max(st, axis=0)))
            alpha = jnp.exp2(m_prev - m_new)
            pr = jnp.exp2(st - m_new[None])
            l_ref[hd] = alpha * l_ref[hd] + jnp.sum(pr, axis=0)
            rows = slice(hd * V_HEAD, (hd + 1) * V_HEAD)
            acc = acc_ref[rows, :].reshape(V_HEAD // SUBLANES, SUBLANES, tq) * alpha[None]
            acc_ref[rows, :] = acc.reshape(V_HEAD, tq) + _dot(vt_ref[0, rows, :], pr.reshape(tq, tq).astype(_BF))
            m_ref[hd] = m_new

    @pl.when(j < i)
    def _():
        step(False)

    @pl.when(j == i)
    def _():
        step(True)
        for hd in range(MLA_HEADS):
            rows = slice(hd * V_HEAD, (hd + 1) * V_HEAD)
            den = _sublane_all(jnp.add, l_ref[hd])
            acc = acc_ref[rows, :].reshape(V_HEAD // SUBLANES, SUBLANES, tq) / den[None]
            acc_ref[rows, :] = acc.reshape(V_HEAD, tq)
        o_ref[...] = acc_ref[...].T.astype(o_ref.dtype)


def _attn_prompt(qt, kh, vt, b, l, tq):
    nq = l // tq
    pairs = [(i, j) for i in range(nq) for j in range(i + 1)]
    it = jnp.asarray([pr[0] for pr in pairs], jnp.int32)
    jt = jnp.asarray([pr[1] for pr in pairs], jnp.int32)
    grid_spec = pltpu.PrefetchScalarGridSpec(
        num_scalar_prefetch=2,
        grid=(b, len(pairs)),
        in_specs=[
            pl.BlockSpec((1, MLA_HEADS, LANES, tq), lambda bi, s, it, jt: (bi, 0, 0, it[s])),
            pl.BlockSpec((tq, MLA_HEADS * LANES), lambda bi, s, it, jt: (bi * nq + jt[s], 0)),
            pl.BlockSpec((1, D_ATTN, tq), lambda bi, s, it, jt: (bi, 0, jt[s])),
        ],
        out_specs=pl.BlockSpec((tq, D_ATTN), lambda bi, s, it, jt: (bi * nq + it[s], 0)),
        scratch_shapes=[pltpu.VMEM((MLA_HEADS, SUBLANES, tq), _F32), pltpu.VMEM((MLA_HEADS, SUBLANES, tq), _F32),
                        pltpu.VMEM((D_ATTN, tq), _F32)],
    )
    return pl.pallas_call(
        functools.partial(_attn_prompt_kernel, tq=tq),
        grid_spec=grid_spec,
        out_shape=jax.ShapeDtypeStruct((b * l, D_ATTN), _BF),
        compiler_params=pltpu.CompilerParams(dimension_semantics=("arbitrary", "arbitrary"),
                                             vmem_limit_bytes=VMEM_LIMIT),
        name="attn_prompt",
    )(it, jt, qt, kh, vt)


def _page_copies(pt_ref, cache_c_ref, cache_krt_ref, cbuf_ref, krbuf_ref, sem_ref, item, slot, pg, page):
    pid = pt_ref[item, pg]
    dst = pl.ds(pl.multiple_of(pg * page, page), page)
    return (pltpu.make_async_copy(cache_c_ref.at[pid], cbuf_ref.at[slot, dst, :], sem_ref.at[slot, 0]),
            pltpu.make_async_copy(cache_krt_ref.at[pid], krbuf_ref.at[slot, :, dst], sem_ref.at[slot, 1]))


def _wait_item(cbuf_ref, krbuf_ref, sem_ref, slot):
    pltpu.make_async_copy(cbuf_ref.at[slot], cbuf_ref.at[slot], sem_ref.at[slot, 0]).wait()
    pltpu.make_async_copy(krbuf_ref.at[slot], krbuf_ref.at[slot], sem_ref.at[slot, 1]).wait()


def _attend_item(q, kvn, slot, cbuf_ref, krbuf_ref, kc_ref, *, n_pages, page, l_new, n_split):
    m_rows = MLA_HEADS * l_new
    q_abs = q[:, :KV_LORA]
    q_pe = q[:, KV_LORA:KV_LORA + QK_ROPE]
    ks = (n_pages // n_split) * page

    def scores(sp):
        rows = pl.ds(sp * ks, ks)
        kc_ref[rows, :] = cbuf_ref[slot, rows, :].astype(_BF)
        return _dot_nt(q_abs, kc_ref[rows, :]) + _dot(q_pe, krbuf_ref[slot, :, rows].astype(_BF))

    def partial(s, values):
        m = jnp.max(s, axis=1, keepdims=True)
        pr = jnp.exp2(s - m)
        return m, jnp.sum(pr, axis=1, keepdims=True), _dot(pr.astype(_BF), values)

    def merge(run, new):
        m = jnp.maximum(run[0], new[0])
        a, c = jnp.exp2(run[0] - m), jnp.exp2(new[0] - m)
        return m, a * run[1] + c * new[1], a * run[2] + c * new[2]

    ahead = {sp: scores(sp) for sp in range(min(SPLITS_AHEAD, n_split))}
    sn = _dot_nt(q, kvn)
    qpos = lax.broadcasted_iota(jnp.int32, (MLA_HEADS, l_new, l_new), 1).reshape(m_rows, l_new)
    kpos = lax.broadcasted_iota(jnp.int32, (m_rows, l_new), 1)
    run = partial(jnp.where(kpos <= qpos, sn, -jnp.inf), kvn[:, :KV_LORA])
    for sp in range(n_split):
        if sp + SPLITS_AHEAD < n_split:
            ahead[sp + SPLITS_AHEAD] = scores(sp + SPLITS_AHEAD)
        run = merge(run, partial(ahead.pop(sp), kc_ref[pl.ds(sp * ks, ks), :]))
    return run[2] / run[1]


def _uv_proj_kernel(o_ref, wuv_ref, out_ref, *, tb, l_new):
    for hd in range(MLA_HEADS):
        o = o_ref[:, hd].reshape(tb * l_new, KV_LORA)
        out_ref[:, hd * V_HEAD:(hd + 1) * V_HEAD] = _dot(o, wuv_ref[hd]).astype(out_ref.dtype)


def _uv_proj(o_lat, wuv, tb):
    b, _, l_new, _ = o_lat.shape
    return pl.pallas_call(
        functools.partial(_uv_proj_kernel, tb=tb, l_new=l_new),
        grid=(b // tb,),
        in_specs=[pl.BlockSpec((tb, MLA_HEADS, l_new, KV_LORA), lambda i: (i, 0, 0, 0)),
                  _const_spec((MLA_HEADS, KV_LORA, V_HEAD))],
        out_specs=pl.BlockSpec((tb * l_new, D_ATTN), lambda i: (i, 0)),
        out_shape=jax.ShapeDtypeStruct((b * l_new, D_ATTN), _BF),
        compiler_params=pltpu.CompilerParams(dimension_semantics=("arbitrary",), vmem_limit_bytes=VMEM_LIMIT),
        name="uv_proj",
    )(o_lat, wuv)


def _ssd_kernel(xbc_ref, z_ref, dtr_ref, cbuf_ref, h0_ref, cw_ref, cb_ref, dtb_ref, alog_ref, dskip_ref, nrm_ref,
                sel_ref, ex_ref, exw_ref, o_ref, h_ref, cnew_ref, xp_ref, y_ref, *, g_items, q):
    c_idx = pl.program_id(1)
    last = pl.num_programs(1) - 1
    heads_per_group = SSM_HEADS // SSM_GROUPS
    rows = g_items * q
    row = lax.broadcasted_iota(jnp.int32, (q, q), 0)
    col = lax.broadcasted_iota(jnp.int32, (q, q), 1)
    causal = row >= col
    n_bc = SSM_GROUPS * D_STATE

    def conv_item(g):
        rs = slice(g * q, (g + 1) * q)

        @pl.when(c_idx == 0)
        def _():
            h_ref[g] = h0_ref[g]
            xp_ref[g, _XP_OFF - (CONV_K - 1):_XP_OFF, :] = cbuf_ref[g]

        act, tail = _conv_silu(xp_ref.at[g], xbc_ref[rs, :], cw_ref, cb_ref, q)

        @pl.when(c_idx == last)
        def _():
            cnew_ref[g] = tail

        return act[:, :D_SSM], act[:, D_SSM:D_SSM + n_bc].astype(_BF), act[:, D_SSM + n_bc:].astype(_BF)

    convs = {0: conv_item(0)} if g_items == 1 else {}

    if g_items == 1:
        tril = jnp.where(causal, 1.0, 0.0).astype(_BF)
    else:
        r_all = lax.broadcasted_iota(jnp.int32, (rows, rows), 0)
        c_all = lax.broadcasted_iota(jnp.int32, (rows, rows), 1)
        same_item = (r_all // q) == (c_all // q)
        tril = jnp.where(r_all >= c_all, jnp.where(same_item, 1.0, 0.0), 0.0).astype(_BF)
    a_neg = -jnp.exp(alog_ref[...])
    dtv = dtr_ref[...] + dtb_ref[...]
    dt = jnp.maximum(dtv, 0.0) + jnp.log1p(jnp.exp(-jnp.abs(dtv)))
    cs = _dot(tril, _split3(dt * a_neg))
    acs = cs[:, :LANES] + cs[:, LANES:2 * LANES] + cs[:, 2 * LANES:]
    acs3 = _split3(acs)
    acs_t = _dot_nt(sel_ref[...], acs3)
    dt_x = _dot(_split3(dt), ex_ref[...])
    acs_x = _dot(acs3, ex_ref[...])
    acs_w = _dot(acs3, exw_ref[...])
    e_acs = jnp.exp(acs_x)

    for g in range(g_items):
        rs = slice(g * q, (g + 1) * q)
        xs, bmat, cmat = convs[g] if g in convs else conv_item(g)
        acs_last_x = acs_x[(g + 1) * q - 1:(g + 1) * q, :]
        x_dt = xs * dt_x[rs]
        xw = (x_dt * jnp.exp(acs_last_x - acs_x[rs])).astype(_BF)
        x_dt = x_dt.astype(_BF)
        for grp in range(SSM_GROUPS):
            b_g = bmat[:, grp * D_STATE:(grp + 1) * D_STATE]
            c_g = cmat[:, grp * D_STATE:(grp + 1) * D_STATE]
            cb = _dot_nt(c_g, b_g)
            for hd in range(grp * heads_per_group, (grp + 1) * heads_per_group):
                cols = slice(hd * SSM_HEADDIM, (hd + 1) * SSM_HEADDIM)
                acs_h = acs_w[rs, hd * LANES:(hd + 1) * LANES]
                acs_h = acs_h[:, :q] if q <= LANES else jnp.concatenate([acs_h] * (q // LANES), axis=1)
                lmat = jnp.exp(jnp.where(causal, acs_h - acs_t[hd:hd + 1, rs], -jnp.inf))
                h_prev = h_ref[g, hd]
                y = _dot((cb * lmat).astype(_BF), x_dt[:, cols])
                y = y + _dot_nt(c_g, h_prev.astype(_BF)) * e_acs[rs, cols]
                decay = jnp.exp(acs[(g + 1) * q - 1:(g + 1) * q, hd:hd + 1])
                h_ref[g, hd] = decay * h_prev + _dot_tn(xw[:, cols], b_g)
                y_ref[:, cols] = y
        yv = y_ref[...] + dskip_ref[...] * xs
        gated = yv * _silu(z_ref[g * q:(g + 1) * q, :])
        gw = D_SSM // SSM_GROUPS
        parts = []
        for grp in range(SSM_GROUPS):
            gg = gated[:, grp * gw:(grp + 1) * gw]
            parts.append(gg * lax.rsqrt(jnp.mean(gg * gg, axis=-1, keepdims=True) + EPS))
        o_ref[g * q:(g + 1) * q, :] = (jnp.concatenate(parts, axis=1) * nrm_ref[...]).astype(o_ref.dtype)


def _ssd(xbc, z, dtr, conv_buf, h0, p, b, l, q, g_items):
    nc = l // q
    rows = g_items * q
    row_spec = lambda w: pl.BlockSpec((rows, w), lambda bi, c: (bi * nc + c, 0))
    state_spec = pl.BlockSpec((g_items, SSM_HEADS, SSM_HEADDIM, D_STATE), lambda bi, c: (bi, 0, 0, 0))
    hist_spec = pl.BlockSpec((g_items, CONV_K - 1, CONV_DIM), lambda bi, c: (bi, 0, 0))
    consts = [p["dt_bias"], p["a_log"], p["d_skip"], p["ssm_norm"], p["sel"], p["head_to_cols"], p["head_to_lanes"]]
    const_specs = [_const_spec((1, LANES)), _const_spec((1, LANES)), _const_spec((1, D_SSM)), _const_spec((1, D_SSM)),
                   _const_spec((SSM_HEADS, 3 * LANES)), _const_spec((3 * LANES, D_SSM)),
                   _const_spec((3 * LANES, SSM_HEADS * LANES))]
    in_specs = [row_spec(CONV_DIM), row_spec(D_SSM), row_spec(LANES), hist_spec, state_spec,
                _const_spec((CONV_K, CONV_DIM)), _const_spec((1, CONV_DIM)), *const_specs]
    out_shape = (jax.ShapeDtypeStruct((b * l, D_SSM), _BF),
                 jax.ShapeDtypeStruct((b, SSM_HEADS, SSM_HEADDIM, D_STATE), _F32),
                 jax.ShapeDtypeStruct((b, CONV_K - 1, CONV_DIM), _F32))
    return pl.pallas_call(
        functools.partial(_ssd_kernel, g_items=g_items, q=q),
        grid=(b // g_items, nc), in_specs=in_specs, out_specs=(row_spec(D_SSM), state_spec, hist_spec),
        out_shape=out_shape,
        scratch_shapes=[pltpu.VMEM((g_items, _XP_OFF + q, CONV_DIM), _F32), pltpu.VMEM((q, D_SSM), _F32)],
        compiler_params=pltpu.CompilerParams(dimension_semantics=("arbitrary", "arbitrary"),
                                             vmem_limit_bytes=VMEM_LIMIT),
        name="ssd",
    )(xbc, z, dtr, conv_buf, h0, p["conv_w"], p["conv_b"], *consts)


def _mlp_chunk(h2, acc, wu_ref, wd_ref, f0, f_chunk):
    u = jnp.maximum(_dot(h2, wu_ref[:, f0:f0 + f_chunk]), 0.0)
    return acc + _dot((u * u).astype(_BF), wd_ref[f0:f0 + f_chunk, :])


def _out_mlp_kernel(x_ref, oa_ref, os_ref, wo_ref, gm_ref, wu_ref, wd_ref, gf_ref, y_ref, *, f_chunk):
    mix = jnp.concatenate([oa_ref[...], os_ref[...]], axis=1)
    x1 = x_ref[...] + _dot(mix, wo_ref[...])
    h2 = _rms(x1, gm_ref[...]).astype(_BF)
    acc = x1
    for f0 in range(0, D_FF, f_chunk):
        acc = _mlp_chunk(h2, acc, wu_ref, wd_ref, f0, f_chunk)
    y_ref[...] = _rms(acc, gf_ref[...])


def _mlp_attn_kernel(pt_ref, x_ref, oa_ref, os_ref, wo_ref, gm_ref, wu_ref, wd_ref, gf_ref, q_ref, kvn_ref,
                     cache_c_ref, cache_krt_ref, y_ref, o_ref, cbuf_ref, krbuf_ref, kc_ref, sem_ref, *,
                     f_chunk, ipb, n_items, n_pages, page, l_new, n_split):
    i = pl.program_id(0)
    copies = functools.partial(_page_copies, pt_ref, cache_c_ref, cache_krt_ref, cbuf_ref, krbuf_ref, sem_ref)

    @pl.when(i == 0)
    def _():
        def body(pg, carry):
            for cp in copies(0, 0, pg, page):
                cp.start()
            return carry
        lax.fori_loop(0, n_pages, body, 0)

    mix = jnp.concatenate([oa_ref[...], os_ref[...]], axis=1)
    x1 = x_ref[...] + _dot(mix, wo_ref[...])
    h2 = _rms(x1, gm_ref[...]).astype(_BF)
    acc = x1
    f_starts = list(range(0, D_FF, f_chunk))
    per_item = -(-len(f_starts) // ipb)
    for k in range(ipb):
        item = i * ipb + k
        slot = k % 2 if ipb % 2 == 0 else item % 2
        _wait_item(cbuf_ref, krbuf_ref, sem_ref, slot)
        nxt = jnp.minimum(item + 1, n_items - 1)
        for pg in range(n_pages):
            for cp in copies(nxt, 1 - slot, pg, page):
                cp.start()
        o = _attend_item(q_ref[k], kvn_ref[k], slot, cbuf_ref, krbuf_ref, kc_ref, n_pages=n_pages, page=page,
                         l_new=l_new, n_split=n_split)
        o_ref[k] = o.astype(o_ref.dtype)
        for f0 in f_starts[k * per_item:(k + 1) * per_item]:
            acc = _mlp_chunk(h2, acc, wu_ref, wd_ref, f0, f_chunk)
    for f0 in f_starts[ipb * per_item:]:
        acc = _mlp_chunk(h2, acc, wu_ref, wd_ref, f0, f_chunk)
    y_ref[...] = _rms(acc, gf_ref[...])

    @pl.when(i == pl.num_programs(0) - 1)
    def _():
        _wait_item(cbuf_ref, krbuf_ref, sem_ref, n_items % 2)


def _mlp_attn(x2d, oa, osm, q, kvn, cache_c, cache_krt, page_table, p, tm):
    t = x2d.shape[0]
    n_steps = t // tm
    b, m_rows, _ = q.shape
    assert b % n_steps == 0, "sample batch must split evenly over the MLP token tiles"
    ipb = b // n_steps
    l_new = m_rows // MLA_HEADS
    n_pages = page_table.shape[1]
    page = cache_c.shape[1]
    keys = n_pages * page
    row_spec = lambda w: pl.BlockSpec((tm, w), lambda i, pt: (i, 0))
    once = lambda shape: pl.BlockSpec(shape, lambda i, pt: (0,) * len(shape), pipeline_mode=pl.Buffered(1))
    item_spec = lambda r, w: pl.BlockSpec((ipb, r, w), lambda i, pt: (i, 0, 0))
    grid_spec = pltpu.PrefetchScalarGridSpec(
        num_scalar_prefetch=1,
        grid=(n_steps,),
        in_specs=[row_spec(D_MODEL), row_spec(D_ATTN), row_spec(D_SSM), once((D_MODEL, D_MODEL)),
                  once((1, D_MODEL)), once((D_MODEL, D_FF)), once((D_FF, D_MODEL)), once((1, D_MODEL)),
                  item_spec(m_rows, QK_PAD), item_spec(l_new, QK_PAD),
                  pl.BlockSpec(memory_space=pl.ANY), pl.BlockSpec(memory_space=pl.ANY)],
        out_specs=(row_spec(D_MODEL), item_spec(m_rows, KV_LORA)),
        scratch_shapes=[pltpu.VMEM((2, keys, KV_LORA), _F32), pltpu.VMEM((2, QK_ROPE, keys), _F32),
                        pltpu.VMEM((keys, KV_LORA), _BF), pltpu.SemaphoreType.DMA((2, 2))],
    )
    return pl.pallas_call(
        functools.partial(_mlp_attn_kernel, f_chunk=1024, ipb=ipb, n_items=b, n_pages=n_pages, page=page,
                          l_new=l_new, n_split=_largest_divisor(n_pages, 8)),
        grid_spec=grid_spec,
        out_shape=(jax.ShapeDtypeStruct((t, D_MODEL), _F32), jax.ShapeDtypeStruct((b, m_rows, KV_LORA), _BF)),
        compiler_params=pltpu.CompilerParams(dimension_semantics=("arbitrary",), vmem_limit_bytes=VMEM_LIMIT),
        name="mlp_attn",
    )(page_table, x2d, oa, osm, p["w_out"], p["g_mlp"], p["w_up"], p["w_down"], p["g_final"], q, kvn, cache_c,
      cache_krt)


def _out_mlp(x2d, oa, osm, p, tm):
    t = x2d.shape[0]
    row_spec = lambda w: pl.BlockSpec((tm, w), lambda i: (i, 0))
    once = lambda shape: pl.BlockSpec(shape, lambda i: (0,) * len(shape), pipeline_mode=pl.Buffered(1))
    return pl.pallas_call(
        functools.partial(_out_mlp_kernel, f_chunk=1024),
        grid=(t // tm,),
        in_specs=[row_spec(D_MODEL), row_spec(D_ATTN), row_spec(D_SSM), once((D_MODEL, D_MODEL)),
                  once((1, D_MODEL)), once((D_MODEL, D_FF)), once((D_FF, D_MODEL)), once((1, D_MODEL))],
        out_specs=row_spec(D_MODEL),
        out_shape=jax.ShapeDtypeStruct((t, D_MODEL), _F32),
        compiler_params=pltpu.CompilerParams(dimension_semantics=("arbitrary",), vmem_limit_bytes=VMEM_LIMIT),
        name="out_mlp",
    )(x2d, oa, osm, p["w_out"], p["g_mlp"], p["w_up"], p["w_down"], p["g_final"])


def _prep_params(norm_attn, w_in, q_norm, kv_norm, w_uq, w_uk, w_uv, conv_w, conv_b, dt_bias, a_log, d_skip,
                 ssm_norm, w_out, norm_mlp, w_up, w_down, norm_final):
    half = QK_ROPE // 2

    def swap(w):
        return jnp.concatenate([w[..., half:], w[..., :half]], axis=-1)

    def place(w, lead, n):
        return jnp.pad(w, ((0, 0), (lead, n - lead - w.shape[1])))

    o1 = Q_LORA
    o2 = o1 + KV_LORA
    o3 = o2 + QK_ROPE
    o5 = o3 + D_SSM + CONV_DIM
    w_kpe = w_in[:, o2:o3]

    def w_in_padded(lead):
        return jnp.concatenate([
            w_in[:, :o2], place(w_kpe, lead, LANES), place(swap(w_kpe), lead, LANES), w_in[:, o3:o5],
            place(w_in[:, o5:], 0, LANES)], axis=1).astype(_BF)

    uq = w_uq.reshape(Q_LORA, MLA_HEADS, QK_NOPE + QK_ROPE)
    uq_nope = uq[:, :, :QK_NOPE]
    uq_rope = uq[:, :, QK_NOPE:]
    w_uq_p = jnp.concatenate([
        jnp.pad(uq_nope, ((0, 0), (0, 0), (0, LANES - QK_NOPE))).reshape(Q_LORA, MLA_HEADS * LANES),
        uq_rope.reshape(Q_LORA, -1), swap(uq_rope).reshape(Q_LORA, -1)], axis=1).astype(_BF)
    w_uk_p = jnp.pad(w_uk, ((0, 0), (0, LANES - QK_NOPE), (0, 0))).astype(_BF)
    tail = LANES - QK_NOPE - QK_ROPE
    plain = jnp.pad(uq, ((0, 0), (0, 0), (0, tail)))
    swapped = jnp.pad(swap(uq_rope), ((0, 0), (0, 0), (QK_NOPE, tail)))
    w_uq_t = jnp.concatenate([plain.reshape(Q_LORA, -1), swapped.reshape(Q_LORA, -1)], axis=1).T.astype(_BF)
    w_k = jnp.pad(jnp.transpose(w_uk, (2, 0, 1)), ((0, 0), (0, 0), (0, LANES - QK_NOPE)))
    w_k = w_k.reshape(KV_LORA, MLA_HEADS * LANES).astype(_BF)
    w_uv_t = jnp.transpose(w_uv, (0, 2, 1)).reshape(D_ATTN, KV_LORA).astype(_BF)
    pad_heads = lambda v: jnp.pad(v.reshape(1, SSM_HEADS), ((0, 0), (0, LANES - SSM_HEADS)))
    return dict(
        g_attn=norm_attn.reshape(1, D_MODEL), w_in=w_in_padded(0), w_in_prompt=w_in_padded(QK_NOPE),
        q_norm=q_norm.reshape(1, Q_LORA), kv_norm=kv_norm.reshape(1, KV_LORA), w_uq=w_uq_p, w_uk=w_uk_p,
        w_uq_t=w_uq_t, w_k=w_k, w_uv_t=w_uv_t, w_uv=w_uv.astype(_BF),
        conv_w=conv_w, conv_b=conv_b.reshape(1, CONV_DIM), dt_bias=pad_heads(dt_bias), a_log=pad_heads(a_log),
        d_skip=jnp.repeat(d_skip, SSM_HEADDIM).reshape(1, D_SSM), ssm_norm=ssm_norm.reshape(1, D_SSM),
        sel=jnp.tile(jnp.eye(SSM_HEADS, LANES, dtype=_BF), (1, 3)),
        head_to_cols=jnp.tile(jnp.repeat(jnp.eye(LANES, SSM_HEADS, dtype=_BF), SSM_HEADDIM, axis=1), (3, 1)),
        head_to_lanes=jnp.tile(jnp.repeat(jnp.eye(LANES, SSM_HEADS, dtype=_BF), LANES, axis=1), (3, 1)),
        w_out=w_out.astype(_BF), g_mlp=norm_mlp.reshape(1, D_MODEL), w_up=w_up.astype(_BF),
        w_down=w_down.astype(_BF), g_final=norm_final.reshape(1, D_MODEL),
    )


def _largest_divisor(n, cap):
    d = min(n, cap)
    while n % d:
        d -= 1
    return d


def kernel(x_prompt, x_sample, cache_kv_latent, cache_k_rope, state_ssm, state_conv, page_table, norm_attn, w_in, q_norm, kv_norm, w_uq, w_uk, w_uv, conv_w, conv_b, dt_bias, a_log, d_skip, ssm_norm, w_out, norm_mlp, w_up, w_down, norm_final):
    assert norm_attn.shape[0] == 1, "single-layer model"
    p = _prep_params(norm_attn[0], w_in[0], q_norm[0], kv_norm[0], w_uq[0], w_uk[0], w_uv[0], conv_w[0], conv_b[0],
                     dt_bias[0], a_log[0], d_skip[0], ssm_norm[0], w_out[0], norm_mlp[0], w_up[0], w_down[0],
                     norm_final)
    bp, lp, _ = x_prompt.shape
    bs, ls, _ = x_sample.shape
    n_pages = page_table.shape[1]
    page = cache_kv_latent.shape[2]
    past_len = n_pages * page

    xp2 = x_prompt.reshape(bp * lp, D_MODEL)
    xs2 = x_sample.reshape(bs * ls, D_MODEL)
    qt, kh, vt, c_p, kr_p, z_p, xbc_p, dtr_p = _in_proj_prompt(xp2, bp, lp, p, _largest_divisor(lp, 512))
    tb = _largest_divisor(bs, 512 // ls)
    q_s, kv_s, c_s, kr_s, z_s, xbc_s, dtr_s = _in_proj(xs2, bs, ls, past_len, p, tb, ls)

    o_attn_p = _attn_prompt(qt, kh, vt, bp, lp, _largest_divisor(lp, 512))
    o_ssm_p, h_p, conv_p = _ssd(xbc_p, z_p, dtr_p, jnp.zeros((bp, CONV_K - 1, CONV_DIM), _F32),
                                jnp.zeros((bp, SSM_HEADS, SSM_HEADDIM, D_STATE), _F32), p, bp, lp,
                                _largest_divisor(lp, CHUNK), 1)

    cache_krt = jnp.swapaxes(cache_k_rope[0], 1, 2)
    y_p, o_lat = _mlp_attn(xp2, o_attn_p, o_ssm_p, q_s.reshape(bs, MLA_HEADS * ls, QK_PAD),
                           kv_s.reshape(bs, ls, QK_PAD), cache_kv_latent[0], cache_krt, page_table, p,
                           _largest_divisor(bp * lp, 256))
    outs_p = (y_p.reshape(bp, lp, D_MODEL), c_p.reshape(1, bp, lp, KV_LORA), kr_p.reshape(1, bp, lp, QK_ROPE),
              h_p[None], conv_p[None])

    o_attn_s = _uv_proj(o_lat.reshape(bs, MLA_HEADS, ls, KV_LORA), p["w_uv"], tb)
    o_ssm_s, h_s, conv_s = _ssd(xbc_s, z_s, dtr_s, state_conv[0], state_ssm[0], p, bs, ls, ls,
                                _largest_divisor(bs, 8))
    y_s = _out_mlp(xs2, o_attn_s, o_ssm_s, p, _largest_divisor(bs * ls, 512))
    outs_s = (y_s.reshape(bs, ls, D_MODEL), c_s.reshape(1, bs, ls, KV_LORA), kr_s.reshape(1, bs, ls, QK_ROPE),
              h_s[None], conv_s[None])

    return (outs_p[0], outs_s[0], outs_p[1], outs_p[2], outs_p[3], outs_p[4],
            outs_s[1], outs_s[2], outs_s[3], outs_s[4])
```

```python
import functools

import jax
import jax.numpy as jnp
import numpy as np
from jax import lax
from jax.experimental import pallas as pl
from jax.experimental.pallas import tpu as pltpu

D_MODEL = 1024
D_ATTN = 512
D_SSM = 512
V_HEAD = 64
MLA_HEADS = 8
QK_NOPE = 64
QK_ROPE = 32
KV_LORA = 256
Q_LORA = 384
ROPE_THETA = 10000.0
SSM_HEADDIM = 64
SSM_HEADS = 8
SSM_GROUPS = 2
D_STATE = 128
CONV_K = 4
CHUNK = 256
CONV_DIM = D_SSM + 2 * SSM_GROUPS * D_STATE
D_FF = 4096
EPS = 1e-6

LANES = 128
SUBLANES = 8
QK_AHEAD = 2
SPLITS_AHEAD = 4
QK_PAD = KV_LORA + LANES
VMEM_LIMIT = 56 * 1024 * 1024
SCALE_LOG2 = (QK_NOPE + QK_ROPE) ** -0.5 * 1.4426950408889634

_C_QLAT = 0
_C_KV = _C_QLAT + Q_LORA
_C_KPE = _C_KV + KV_LORA
_C_KPES = _C_KPE + LANES
_C_Z = _C_KPES + LANES
_C_XBC = _C_Z + D_SSM
_C_DT = _C_XBC + CONV_DIM
_N_IN_PAD = _C_DT + LANES
_Q_NOPE = 0
_Q_ROPE = MLA_HEADS * LANES
_Q_ROPES = _Q_ROPE + MLA_HEADS * QK_ROPE
_N_UQ_PAD = _Q_ROPES + MLA_HEADS * QK_ROPE

_BF = jnp.bfloat16
_F32 = jnp.float32


def _dot(a, b):
    return jnp.dot(a, b, preferred_element_type=_F32)


def _dot_nt(a, b):
    return lax.dot_general(a, b, (((1,), (1,)), ((), ())), preferred_element_type=_F32)


def _dot_tn(a, b):
    return lax.dot_general(a, b, (((0,), (0,)), ((), ())), preferred_element_type=_F32)


def _split3(a):
    hi = a.astype(_BF)
    r1 = a - hi.astype(_F32)
    mid = r1.astype(_BF)
    lo = (r1 - mid.astype(_F32)).astype(_BF)
    return jnp.concatenate([hi, mid, lo], axis=1)


def _rms(x, g):
    return x * lax.rsqrt(jnp.mean(x * x, axis=-1, keepdims=True) + EPS) * g


def _silu(x):
    return x * (1.0 / (1.0 + jnp.exp(-x)))


def _sublane_all(op, x):
    for shift in (4, 2, 1):
        x = op(x, pltpu.roll(x, shift, 0))
    return x


def _const_spec(shape):
    nd = len(shape)
    return pl.BlockSpec(shape, lambda *_: (0,) * nd)


def _rope_tables(l, past_len):
    pos = past_len + np.arange(l, dtype=np.float64)
    inv = ROPE_THETA ** (-(np.arange(0, QK_ROPE, 2, dtype=np.float64) / QK_ROPE))
    ang = pos[:, None] * inv[None, :]
    cos, sin = np.cos(ang).astype(np.float32), np.sin(ang).astype(np.float32)
    return np.concatenate([cos, cos], axis=1), np.concatenate([-sin, sin], axis=1)


_XP_OFF = 8


def _conv_silu(win_ref, x_new, cw_ref, cb_ref, q):
    win_ref[_XP_OFF:_XP_OFF + q, :] = x_new
    conv = cb_ref[...]
    for k in range(CONV_K):
        conv = conv + win_ref[_XP_OFF - (CONV_K - 1) + k:_XP_OFF - (CONV_K - 1) + k + q, :] * cw_ref[k:k + 1, :]
    tail = win_ref[_XP_OFF + q - (CONV_K - 1):_XP_OFF + q, :]
    win_ref[_XP_OFF - (CONV_K - 1):_XP_OFF, :] = tail
    return _silu(conv), tail


def _in_proj_kernel(x_ref, g_ref, w_ref, qn_ref, kvn_ref, wuq_ref, wuk_ref, cosq_ref, sinq_ref, cosk_ref, sink_ref,
                    q_ref, kv_ref, c_ref, kr_ref, z_ref, xbc_ref, dt_ref, *, tb, tl):
    tm = tb * tl

    def rows(tab_ref):
        t = tab_ref[...]
        if tb == 1:
            return t
        return jnp.broadcast_to(t[None], (tb, tl, t.shape[-1])).reshape(tm, t.shape[-1])

    h = _rms(x_ref[...], g_ref[...]).astype(_BF)
    proj = _dot(h, w_ref[...])
    z_ref[...] = proj[:, _C_Z:_C_XBC]
    xbc_ref[...] = proj[:, _C_XBC:_C_DT]
    dt_ref[...] = proj[:, _C_DT:_N_IN_PAD]

    c = _rms(proj[:, _C_KV:_C_KPE], kvn_ref[...])
    kr = proj[:, _C_KPE:_C_KPES] * rows(cosk_ref) + proj[:, _C_KPES:_C_Z] * rows(sink_ref)
    c_ref[...] = c
    kr_ref[...] = kr[:, :QK_ROPE]
    kv_ref[:, :KV_LORA] = c.astype(_BF)
    kv_ref[:, KV_LORA:] = kr.astype(_BF)

    qn = _rms(proj[:, _C_QLAT:_C_KV], qn_ref[...]).astype(_BF)
    qq = _dot(qn, wuq_ref[...])
    q_pe = qq[:, _Q_ROPE:_Q_ROPES] * rows(cosq_ref) + qq[:, _Q_ROPES:_N_UQ_PAD] * rows(sinq_ref)
    lane = lax.broadcasted_iota(jnp.int32, (tm, LANES), 1)
    heads_per_group = LANES // QK_ROPE
    for hd in range(MLA_HEADS):
        q_abs = _dot(qq[:, hd * LANES:(hd + 1) * LANES].astype(_BF), wuk_ref[hd]) * SCALE_LOG2
        grp = q_pe[:, (hd // heads_per_group) * LANES:(hd // heads_per_group + 1) * LANES]
        shift = (LANES - QK_ROPE * (hd % heads_per_group)) % LANES
        if shift:
            grp = pltpu.roll(grp, shift, 1)
        pe = jnp.where(lane < QK_ROPE, grp * SCALE_LOG2, 0.0)
        q_ref[:, hd, :, :KV_LORA] = q_abs.astype(_BF).reshape(tb, tl, KV_LORA)
        q_ref[:, hd, :, KV_LORA:] = pe.astype(_BF).reshape(tb, tl, LANES)


def _in_proj(x2d, b, l, past_len, p, tb, tl):
    t = b * l
    tm = tb * tl
    n_l = l // tl
    cos32, sin32 = _rope_tables(l, past_len)
    padk = np.zeros((l, LANES - QK_ROPE), np.float32)
    cosk = np.concatenate([cos32, padk], axis=1)
    sink = np.concatenate([sin32, padk], axis=1)
    cosq = np.tile(cos32, (1, MLA_HEADS))
    sinq = np.tile(sin32, (1, MLA_HEADS))

    row_spec = lambda w: pl.BlockSpec((tm, w), lambda i: (i, 0))
    tab_spec = lambda w: pl.BlockSpec((tl, w), lambda i: (i % n_l, 0))
    out_shape = (
        jax.ShapeDtypeStruct((b, MLA_HEADS, l, QK_PAD), _BF),
        jax.ShapeDtypeStruct((t, QK_PAD), _BF),
        jax.ShapeDtypeStruct((t, KV_LORA), _F32),
        jax.ShapeDtypeStruct((t, QK_ROPE), _F32),
        jax.ShapeDtypeStruct((t, D_SSM), _F32),
        jax.ShapeDtypeStruct((t, CONV_DIM), _F32),
        jax.ShapeDtypeStruct((t, LANES), _F32),
    )
    out_specs = (
        pl.BlockSpec((tb, MLA_HEADS, tl, QK_PAD), lambda i: (i // n_l, 0, i % n_l, 0)),
        row_spec(QK_PAD), row_spec(KV_LORA), row_spec(QK_ROPE), row_spec(D_SSM), row_spec(CONV_DIM), row_spec(LANES),
    )
    in_specs = [
        row_spec(D_MODEL), _const_spec((1, D_MODEL)), _const_spec((D_MODEL, _N_IN_PAD)),
        _const_spec((1, Q_LORA)), _const_spec((1, KV_LORA)), _const_spec((Q_LORA, _N_UQ_PAD)),
        _const_spec((MLA_HEADS, LANES, KV_LORA)),
        tab_spec(MLA_HEADS * QK_ROPE), tab_spec(MLA_HEADS * QK_ROPE), tab_spec(LANES), tab_spec(LANES),
    ]
    return pl.pallas_call(
        functools.partial(_in_proj_kernel, tb=tb, tl=tl),
        grid=(t // tm,), in_specs=in_specs, out_specs=out_specs, out_shape=out_shape,
        compiler_params=pltpu.CompilerParams(dimension_semantics=("arbitrary",), vmem_limit_bytes=VMEM_LIMIT),
        name="in_proj",
    )(x2d, p["g_attn"], p["w_in"], p["q_norm"], p["kv_norm"], p["w_uq"], p["w_uk"], cosq, sinq, cosk, sink)


def _in_proj_prompt_kernel(x_ref, g_ref, w_ref, qn_ref, kvn_ref, wuqt_ref, wk_ref, wuvt_ref, cost_ref, sint_ref,
                           cosk_ref, sink_ref, qt_ref, kh_ref, vt_ref, c_ref, kr_ref, z_ref, xbc_ref, dt_ref):
    h = _rms(x_ref[...], g_ref[...]).astype(_BF)
    proj = _dot(h, w_ref[...])
    z_ref[...] = proj[:, _C_Z:_C_XBC]
    xbc_ref[...] = proj[:, _C_XBC:_C_DT]
    dt_ref[...] = proj[:, _C_DT:_N_IN_PAD]

    c = _rms(proj[:, _C_KV:_C_KPE], kvn_ref[...])
    kr = proj[:, _C_KPE:_C_KPES] * cosk_ref[...] + proj[:, _C_KPES:_C_Z] * sink_ref[...]
    c_ref[...] = c
    kr_ref[...] = kr[:, QK_NOPE:QK_NOPE + QK_ROPE]
    c_bf = c.astype(_BF)
    kn = _dot(c_bf, wk_ref[...])
    for hd in range(MLA_HEADS):
        kh_ref[:, hd * LANES:(hd + 1) * LANES] = (kn[:, hd * LANES:(hd + 1) * LANES] + kr).astype(_BF)
    vt_ref[0] = _dot_nt(wuvt_ref[...], c_bf).astype(_BF)

    qn = _rms(proj[:, _C_QLAT:_C_KV], qn_ref[...]).astype(_BF)
    qqt = _dot_nt(wuqt_ref[...], qn)
    half = MLA_HEADS * LANES
    for hd in range(MLA_HEADS):
        qt = (qqt[hd * LANES:(hd + 1) * LANES] * cost_ref[...]
              + qqt[half + hd * LANES:half + (hd + 1) * LANES] * sint_ref[...])
        qt_ref[0, hd] = qt.astype(_BF)


def _in_proj_prompt(x2d, b, l, p, tl):
    t = b * l
    n_l = l // tl
    cos32, sin32 = _rope_tables(l, 0)
    lead = np.zeros((l, QK_NOPE), np.float32)
    trail = np.zeros((l, LANES - QK_NOPE - QK_ROPE), np.float32)
    cosk = np.concatenate([lead, cos32, trail], axis=1)
    sink = np.concatenate([lead, sin32, trail], axis=1)
    cost = np.ascontiguousarray((np.concatenate([lead + 1.0, cos32, trail], axis=1) * np.float32(SCALE_LOG2)).T)
    sint = np.ascontiguousarray((sink * np.float32(SCALE_LOG2)).T)

    row_spec = lambda w: pl.BlockSpec((tl, w), lambda i: (i, 0))
    out_shape = (
        jax.ShapeDtypeStruct((b, MLA_HEADS, LANES, l), _BF),
        jax.ShapeDtypeStruct((t, MLA_HEADS * LANES), _BF),
        jax.ShapeDtypeStruct((b, D_ATTN, l), _BF),
        jax.ShapeDtypeStruct((t, KV_LORA), _F32),
        jax.ShapeDtypeStruct((t, QK_ROPE), _F32),
        jax.ShapeDtypeStruct((t, D_SSM), _F32),
        jax.ShapeDtypeStruct((t, CONV_DIM), _F32),
        jax.ShapeDtypeStruct((t, LANES), _F32),
    )
    out_specs = (
        pl.BlockSpec((1, MLA_HEADS, LANES, tl), lambda i: (i // n_l, 0, 0, i % n_l)),
        row_spec(MLA_HEADS * LANES),
        pl.BlockSpec((1, D_ATTN, tl), lambda i: (i // n_l, 0, i % n_l)),
        row_spec(KV_LORA), row_spec(QK_ROPE), row_spec(D_SSM), row_spec(CONV_DIM), row_spec(LANES),
    )
    in_specs = [
        row_spec(D_MODEL), _const_spec((1, D_MODEL)), _const_spec((D_MODEL, _N_IN_PAD)),
        _const_spec((1, Q_LORA)), _const_spec((1, KV_LORA)), _const_spec((2 * MLA_HEADS * LANES, Q_LORA)),
        _const_spec((KV_LORA, MLA_HEADS * LANES)), _const_spec((D_ATTN, KV_LORA)),
        pl.BlockSpec((LANES, tl), lambda i: (0, i % n_l)), pl.BlockSpec((LANES, tl), lambda i: (0, i % n_l)),
        pl.BlockSpec((tl, LANES), lambda i: (i % n_l, 0)), pl.BlockSpec((tl, LANES), lambda i: (i % n_l, 0)),
    ]
    return pl.pallas_call(
        _in_proj_prompt_kernel,
        grid=(t // tl,), in_specs=in_specs, out_specs=out_specs, out_shape=out_shape,
        compiler_params=pltpu.CompilerParams(dimension_semantics=("arbitrary",), vmem_limit_bytes=VMEM_LIMIT),
        name="in_proj_prompt",
    )(x2d, p["g_attn"], p["w_in_prompt"], p["q_norm"], p["kv_norm"], p["w_uq_t"], p["w_k"], p["w_uv_t"],
      cost, sint, cosk, sink)


def _attn_prompt_kernel(it_ref, jt_ref, qt_ref, kh_ref, vt_ref, o_ref, m_ref, l_ref, acc_ref, *, tq):
    s_idx = pl.program_id(1)
    i = it_ref[s_idx]
    j = jt_ref[s_idx]

    @pl.when(j == 0)
    def _():
        m_ref[...] = jnp.full(m_ref.shape, -jnp.inf, _F32)
        l_ref[...] = jnp.zeros(l_ref.shape, _F32)
        acc_ref[...] = jnp.zeros(acc_ref.shape, _F32)

    def step(masked):
        if masked:
            krow = lax.broadcasted_iota(jnp.int32, (tq, tq), 0)
            qcol = lax.broadcasted_iota(jnp.int32, (tq, tq), 1)
            keep = krow <= qcol

        def scores(hd):
            st = _dot(kh_ref[:, hd * LANES:(hd + 1) * LANES], qt_ref[0, hd])
            if masked:
                st = jnp.where(keep, st, -jnp.inf)
            return st.reshape(tq // SUBLANES, SUBLANES, tq)

        ahead = {hd: scores(hd) for hd in range(QK_AHEAD)}
        for hd in range(MLA_HEADS):
            if hd + QK_AHEAD < MLA_HEADS:
                ahead[hd + QK_AHEAD] = scores(hd + QK_AHEAD)
            st = ahead.pop(hd)
            m_prev = m_ref[hd]
            m_new = jnp.maximum(m_prev, _sublane_all(jnp.maximum, jnp.max(st, axis=0)))
            alpha = jnp.exp2(m_prev - m_new)
            pr = jnp.exp2(st - m_new[None])
            l_ref[hd] = alpha * l_ref[hd] + jnp.sum(pr, axis=0)
            rows = slice(hd * V_HEAD, (hd + 1) * V_HEAD)
            acc = acc_ref[rows, :].reshape(V_HEAD // SUBLANES, SUBLANES, tq) * alpha[None]
            acc_ref[rows, :] = acc.reshape(V_HEAD, tq) + _dot(vt_ref[0, rows, :], pr.reshape(tq, tq).astype(_BF))
            m_ref[hd] = m_new

    @pl.when(j < i)
    def _():
        step(False)

    @pl.when(j == i)
    def _():
        step(True)
        for hd in range(MLA_HEADS):
            rows = slice(hd * V_HEAD, (hd + 1) * V_HEAD)
            den = _sublane_all(jnp.add, l_ref[hd])
            acc = acc_ref[rows, :].reshape(V_HEAD // SUBLANES, SUBLANES, tq) / den[None]
            acc_ref[rows, :] = acc.reshape(V_HEAD, tq)
        o_ref[...] = acc_ref[...].T.astype(o_ref.dtype)


def _attn_prompt(qt, kh, vt, b, l, tq):
    nq = l // tq
    pairs = [(i, j) for i in range(nq) for j in range(i + 1)]
    it = jnp.asarray([pr[0] for pr in pairs], jnp.int32)
    jt = jnp.asarray([pr[1] for pr in pairs], jnp.int32)
    grid_spec = pltpu.PrefetchScalarGridSpec(
        num_scalar_prefetch=2,
        grid=(b, len(pairs)),
        in_specs=[
            pl.BlockSpec((1, MLA_HEADS, LANES, tq), lambda bi, s, it, jt: (bi, 0, 0, it[s])),
            pl.BlockSpec((tq, MLA_HEADS * LANES), lambda bi, s, it, jt: (bi * nq + jt[s], 0)),
            pl.BlockSpec((1, D_ATTN, tq), lambda bi, s, it, jt: (bi, 0, jt[s])),
        ],
        out_specs=pl.BlockSpec((tq, D_ATTN), lambda bi, s, it, jt: (bi * nq + it[s], 0)),
        scratch_shapes=[pltpu.VMEM((MLA_HEADS, SUBLANES, tq), _F32), pltpu.VMEM((MLA_HEADS, SUBLANES, tq), _F32),
                        pltpu.VMEM((D_ATTN, tq), _F32)],
    )
    return pl.pallas_call(
        functools.partial(_attn_prompt_kernel, tq=tq),
        grid_spec=grid_spec,
        out_shape=jax.ShapeDtypeStruct((b * l, D_ATTN), _BF),
        compiler_params=pltpu.CompilerParams(dimension_semantics=("arbitrary", "arbitrary"),
                                             vmem_limit_bytes=VMEM_LIMIT),
        name="attn_prompt",
    )(it, jt, qt, kh, vt)


def _page_copies(pt_ref, cache_c_ref, cache_krt_ref, cbuf_ref, krbuf_ref, sem_ref, item, slot, pg, page):
    pid = pt_ref[item, pg]
    dst = pl.ds(pl.multiple_of(pg * page, page), page)
    return (pltpu.make_async_copy(cache_c_ref.at[pid], cbuf_ref.at[slot, dst, :], sem_ref.at[slot, 0]),
            pltpu.make_async_copy(cache_krt_ref.at[pid], krbuf_ref.at[slot, :, dst], sem_ref.at[slot, 1]))


def _wait_item(cbuf_ref, krbuf_ref, sem_ref, slot):
    pltpu.make_async_copy(cbuf_ref.at[slot], cbuf_ref.at[slot], sem_ref.at[slot, 0]).wait()
    pltpu.make_async_copy(krbuf_ref.at[slot], krbuf_ref.at[slot], sem_ref.at[slot, 1]).wait()


def _attend_item(q, kvn, slot, cbuf_ref, krbuf_ref, kc_ref, *, n_pages, page, l_new, n_split):
    m_rows = MLA_HEADS * l_new
    q_abs = q[:, :KV_LORA]
    q_pe = q[:, KV_LORA:KV_LORA + QK_ROPE]
    ks = (n_pages // n_split) * page

    def scores(sp):
        rows = pl.ds(sp * ks, ks)
        kc_ref[rows, :] = cbuf_ref[slot, rows, :].astype(_BF)
        return _dot_nt(q_abs, kc_ref[rows, :]) + _dot(q_pe, krbuf_ref[slot, :, rows].astype(_BF))

    def partial(s, values):
        m = jnp.max(s, axis=1, keepdims=True)
        pr = jnp.exp2(s - m)
        return m, jnp.sum(pr, axis=1, keepdims=True), _dot(pr.astype(_BF), values)

    def merge(run, new):
        m = jnp.maximum(run[0], new[0])
        a, c = jnp.exp2(run[0] - m), jnp.exp2(new[0] - m)
        return m, a * run[1] + c * new[1], a * run[2] + c * new[2]

    ahead = {sp: scores(sp) for sp in range(min(SPLITS_AHEAD, n_split))}
    sn = _dot_nt(q, kvn)
    qpos = lax.broadcasted_iota(jnp.int32, (MLA_HEADS, l_new, l_new), 1).reshape(m_rows, l_new)
    kpos = lax.broadcasted_iota(jnp.int32, (m_rows, l_new), 1)
    run = partial(jnp.where(kpos <= qpos, sn, -jnp.inf), kvn[:, :KV_LORA])
    for sp in range(n_split):
        if sp + SPLITS_AHEAD < n_split:
            ahead[sp + SPLITS_AHEAD] = scores(sp + SPLITS_AHEAD)
        run = merge(run, partial(ahead.pop(sp), kc_ref[pl.ds(sp * ks, ks), :]))
    return run[2] / run[1]


def _uv_proj_kernel(o_ref, wuv_ref, out_ref, *, tb, l_new):
    for hd in range(MLA_HEADS):
        o = o_ref[:, hd].reshape(tb * l_new, KV_LORA)
        out_ref[:, hd * V_HEAD:(hd + 1) * V_HEAD] = _dot(o, wuv_ref[hd]).astype(out_ref.dtype)


def _uv_proj(o_lat, wuv, tb):
    b, _, l_new, _ = o_lat.shape
    return pl.pallas_call(
        functools.partial(_uv_proj_kernel, tb=tb, l_new=l_new),
        grid=(b // tb,),
        in_specs=[pl.BlockSpec((tb, MLA_HEADS, l_new, KV_LORA), lambda i: (i, 0, 0, 0)),
                  _const_spec((MLA_HEADS, KV_LORA, V_HEAD))],
        out_specs=pl.BlockSpec((tb * l_new, D_ATTN), lambda i: (i, 0)),
        out_shape=jax.ShapeDtypeStruct((b * l_new, D_ATTN), _BF),
        compiler_params=pltpu.CompilerParams(dimension_semantics=("arbitrary",), vmem_limit_bytes=VMEM_LIMIT),
        name="uv_proj",
    )(o_lat, wuv)


def _ssd_kernel(xbc_ref, z_ref, dtr_ref, cbuf_ref, h0_ref, cw_ref, cb_ref, dtb_ref, alog_ref, dskip_ref, nrm_ref,
                sel_ref, ex_ref, exw_ref, o_ref, h_ref, cnew_ref, xp_ref, y_ref, *, n_seq, n_chunk, q):
    assert n_seq == 1 or n_chunk == 1
    g_items = n_seq * n_chunk
    c_idx = pl.program_id(1)
    last = pl.num_programs(1) - 1
    heads_per_group = SSM_HEADS // SSM_GROUPS
    rows = g_items * q
    row = lax.broadcasted_iota(jnp.int32, (q, q), 0)
    col = lax.broadcasted_iota(jnp.int32, (q, q), 1)
    causal = row >= col
    n_bc = SSM_GROUPS * D_STATE

    def conv_item(g):
        sq, ck = divmod(g, n_chunk)
        rs = slice(g * q, (g + 1) * q)

        if ck == 0:
            @pl.when(c_idx == 0)
            def _():
                h_ref[sq] = h0_ref[sq]
                xp_ref[sq, _XP_OFF - (CONV_K - 1):_XP_OFF, :] = cbuf_ref[sq]

        act, tail = _conv_silu(xp_ref.at[sq], xbc_ref[rs, :], cw_ref, cb_ref, q)

        if ck == n_chunk - 1:
            @pl.when(c_idx == last)
            def _():
                cnew_ref[sq] = tail

        return act[:, :D_SSM], act[:, D_SSM:D_SSM + n_bc].astype(_BF), act[:, D_SSM + n_bc:].astype(_BF)

    convs = {g: conv_item(g) for g in range(g_items)} if n_seq == 1 else {}

    if g_items == 1:
        tril = jnp.where(causal, 1.0, 0.0).astype(_BF)
    else:
        r_all = lax.broadcasted_iota(jnp.int32, (rows, rows), 0)
        c_all = lax.broadcasted_iota(jnp.int32, (rows, rows), 1)
        same_item = (r_all // q) == (c_all // q)
        tril = jnp.where(r_all >= c_all, jnp.where(same_item, 1.0, 0.0), 0.0).astype(_BF)
    a_neg = -jnp.exp(alog_ref[...])
    dtv = dtr_ref[...] + dtb_ref[...]
    dt = jnp.maximum(dtv, 0.0) + jnp.log1p(jnp.exp(-jnp.abs(dtv)))
    cs = _dot(tril, _split3(dt * a_neg))
    acs = cs[:, :LANES] + cs[:, LANES:2 * LANES] + cs[:, 2 * LANES:]
    acs3 = _split3(acs)
    acs_t = _dot_nt(sel_ref[...], acs3)
    dt_x = _dot(_split3(dt), ex_ref[...])
    acs_x = _dot(acs3, ex_ref[...])
    acs_w = _dot(acs3, exw_ref[...])
    e_acs = jnp.exp(acs_x)

    for g in range(g_items):
        sq = g // n_chunk
        rs = slice(g * q, (g + 1) * q)
        xs, bmat, cmat = convs[g] if g in convs else conv_item(g)
        acs_last_x = acs_x[(g + 1) * q - 1:(g + 1) * q, :]
        x_dt = xs * dt_x[rs]
        xw = (x_dt * jnp.exp(acs_last_x - acs_x[rs])).astype(_BF)
        x_dt = x_dt.astype(_BF)
        for grp in range(SSM_GROUPS):
            b_g = bmat[:, grp * D_STATE:(grp + 1) * D_STATE]
            c_g = cmat[:, grp * D_STATE:(grp + 1) * D_STATE]
            cb = _dot_nt(c_g, b_g)
            for hd in range(grp * heads_per_group, (grp + 1) * heads_per_group):
                cols = slice(hd * SSM_HEADDIM, (hd + 1) * SSM_HEADDIM)
                acs_h = acs_w[rs, hd * LANES:(hd + 1) * LANES]
                acs_h = acs_h[:, :q] if q <= LANES else jnp.concatenate([acs_h] * (q // LANES), axis=1)
                lmat = jnp.exp(jnp.where(causal, acs_h - acs_t[hd:hd + 1, rs], -jnp.inf))
                h_prev = h_ref[sq, hd]
                y = _dot((cb * lmat).astype(_BF), x_dt[:, cols])
                y = y + _dot_nt(c_g, h_prev.astype(_BF)) * e_acs[rs, cols]
                decay = jnp.exp(acs[(g + 1) * q - 1:(g + 1) * q, hd:hd + 1])
                h_ref[sq, hd] = decay * h_prev + _dot_tn(xw[:, cols], b_g)
                y_ref[:, cols] = y
        yv = y_ref[...] + dskip_ref[...] * xs
        gated = yv * _silu(z_ref[g * q:(g + 1) * q, :])
        gw = D_SSM // SSM_GROUPS
        parts = []
        for grp in range(SSM_GROUPS):
            gg = gated[:, grp * gw:(grp + 1) * gw]
            parts.append(gg * lax.rsqrt(jnp.mean(gg * gg, axis=-1, keepdims=True) + EPS))
        o_ref[g * q:(g + 1) * q, :] = (jnp.concatenate(parts, axis=1) * nrm_ref[...]).astype(o_ref.dtype)


def _ssd(xbc, z, dtr, conv_buf, h0, p, b, l, q, n_seq, n_chunk):
    nc = l // (q * n_chunk)
    rows = n_seq * n_chunk * q
    row_spec = lambda w: pl.BlockSpec((rows, w), lambda bi, c: (bi * nc + c, 0))
    state_spec = pl.BlockSpec((n_seq, SSM_HEADS, SSM_HEADDIM, D_STATE), lambda bi, c: (bi, 0, 0, 0))
    hist_spec = pl.BlockSpec((n_seq, CONV_K - 1, CONV_DIM), lambda bi, c: (bi, 0, 0))
    consts = [p["dt_bias"], p["a_log"], p["d_skip"], p["ssm_norm"], p["sel"], p["head_to_cols"], p["head_to_lanes"]]
    const_specs = [_const_spec((1, LANES)), _const_spec((1, LANES)), _const_spec((1, D_SSM)), _const_spec((1, D_SSM)),
                   _const_spec((SSM_HEADS, 3 * LANES)), _const_spec((3 * LANES, D_SSM)),
                   _const_spec((3 * LANES, SSM_HEADS * LANES))]
    in_specs = [row_spec(CONV_DIM), row_spec(D_SSM), row_spec(LANES), hist_spec, state_spec,
                _const_spec((CONV_K, CONV_DIM)), _const_spec((1, CONV_DIM)), *const_specs]
    out_shape = (jax.ShapeDtypeStruct((b * l, D_SSM), _BF),
                 jax.ShapeDtypeStruct((b, SSM_HEADS, SSM_HEADDIM, D_STATE), _F32),
                 jax.ShapeDtypeStruct((b, CONV_K - 1, CONV_DIM), _F32))
    return pl.pallas_call(
        functools.partial(_ssd_kernel, n_seq=n_seq, n_chunk=n_chunk, q=q),
        grid=(b // n_seq, nc), in_specs=in_specs, out_specs=(row_spec(D_SSM), state_spec, hist_spec),
        out_shape=out_shape,
        scratch_shapes=[pltpu.VMEM((n_seq, _XP_OFF + q, CONV_DIM), _F32), pltpu.VMEM((q, D_SSM), _F32)],
        compiler_params=pltpu.CompilerParams(dimension_semantics=("arbitrary", "arbitrary"),
                                             vmem_limit_bytes=VMEM_LIMIT),
        name="ssd",
    )(xbc, z, dtr, conv_buf, h0, p["conv_w"], p["conv_b"], *consts)


def _mlp_chunk(h2, acc, wu_ref, wd_ref, f0, f_chunk):
    u = jnp.maximum(_dot(h2, wu_ref[:, f0:f0 + f_chunk]), 0.0)
    return acc + _dot((u * u).astype(_BF), wd_ref[f0:f0 + f_chunk, :])


def _out_mlp_kernel(x_ref, oa_ref, os_ref, wo_ref, gm_ref, wu_ref, wd_ref, gf_ref, y_ref, *, f_chunk):
    mix = jnp.concatenate([oa_ref[...], os_ref[...]], axis=1)
    x1 = x_ref[...] + _dot(mix, wo_ref[...])
    h2 = _rms(x1, gm_ref[...]).astype(_BF)
    acc = x1
    for f0 in range(0, D_FF, f_chunk):
        acc = _mlp_chunk(h2, acc, wu_ref, wd_ref, f0, f_chunk)
    y_ref[...] = _rms(acc, gf_ref[...])


def _mlp_attn_kernel(pt_ref, x_ref, oa_ref, os_ref, wo_ref, gm_ref, wu_ref, wd_ref, gf_ref, q_ref, kvn_ref,
                     cache_c_ref, cache_krt_ref, y_ref, o_ref, cbuf_ref, krbuf_ref, kc_ref, sem_ref, *,
                     f_chunk, ipb, n_items, n_pages, page, l_new, n_split):
    i = pl.program_id(0)
    copies = functools.partial(_page_copies, pt_ref, cache_c_ref, cache_krt_ref, cbuf_ref, krbuf_ref, sem_ref)

    @pl.when(i == 0)
    def _():
        def body(pg, carry):
            for cp in copies(0, 0, pg, page):
                cp.start()
            return carry
        lax.fori_loop(0, n_pages, body, 0)

    mix = jnp.concatenate([oa_ref[...], os_ref[...]], axis=1)
    x1 = x_ref[...] + _dot(mix, wo_ref[...])
    h2 = _rms(x1, gm_ref[...]).astype(_BF)
    acc = x1
    f_starts = list(range(0, D_FF, f_chunk))
    per_item = -(-len(f_starts) // ipb)
    for k in range(ipb):
        item = i * ipb + k
        slot = k % 2 if ipb % 2 == 0 else item % 2
        _wait_item(cbuf_ref, krbuf_ref, sem_ref, slot)
        nxt = jnp.minimum(item + 1, n_items - 1)
        for pg in range(n_pages):
            for cp in copies(nxt, 1 - slot, pg, page):
                cp.start()
        o = _attend_item(q_ref[k], kvn_ref[k], slot, cbuf_ref, krbuf_ref, kc_ref, n_pages=n_pages, page=page,
                         l_new=l_new, n_split=n_split)
        o_ref[k] = o.astype(o_ref.dtype)
        for f0 in f_starts[k * per_item:(k + 1) * per_item]:
            acc = _mlp_chunk(h2, acc, wu_ref, wd_ref, f0, f_chunk)
    for f0 in f_starts[ipb * per_item:]:
        acc = _mlp_chunk(h2, acc, wu_ref, wd_ref, f0, f_chunk)
    y_ref[...] = _rms(acc, gf_ref[...])

    @pl.when(i == pl.num_programs(0) - 1)
    def _():
        _wait_item(cbuf_ref, krbuf_ref, sem_ref, n_items % 2)


def _mlp_attn(x2d, oa, osm, q, kvn, cache_c, cache_krt, page_table, p, tm):
    t = x2d.shape[0]
    n_steps = t // tm
    b, m_rows, _ = q.shape
    assert b % n_steps == 0, "sample batch must split evenly over the MLP token tiles"
    ipb = b // n_steps
    l_new = m_rows // MLA_HEADS
    n_pages = page_table.shape[1]
    page = cache_c.shape[1]
    keys = n_pages * page
    row_spec = lambda w: pl.BlockSpec((tm, w), lambda i, pt: (i, 0))
    once = lambda shape: pl.BlockSpec(shape, lambda i, pt: (0,) * len(shape), pipeline_mode=pl.Buffered(1))
    item_spec = lambda r, w: pl.BlockSpec((ipb, r, w), lambda i, pt: (i, 0, 0))
    grid_spec = pltpu.PrefetchScalarGridSpec(
        num_scalar_prefetch=1,
        grid=(n_steps,),
        in_specs=[row_spec(D_MODEL), row_spec(D_ATTN), row_spec(D_SSM), once((D_MODEL, D_MODEL)),
                  once((1, D_MODEL)), once((D_MODEL, D_FF)), once((D_FF, D_MODEL)), once((1, D_MODEL)),
                  item_spec(m_rows, QK_PAD), item_spec(l_new, QK_PAD),
                  pl.BlockSpec(memory_space=pl.ANY), pl.BlockSpec(memory_space=pl.ANY)],
        out_specs=(row_spec(D_MODEL), item_spec(m_rows, KV_LORA)),
        scratch_shapes=[pltpu.VMEM((2, keys, KV_LORA), _F32), pltpu.VMEM((2, QK_ROPE, keys), _F32),
                        pltpu.VMEM((keys, KV_LORA), _BF), pltpu.SemaphoreType.DMA((2, 2))],
    )
    return pl.pallas_call(
        functools.partial(_mlp_attn_kernel, f_chunk=1024, ipb=ipb, n_items=b, n_pages=n_pages, page=page,
                          l_new=l_new, n_split=_largest_divisor(n_pages, 8)),
        grid_spec=grid_spec,
        out_shape=(jax.ShapeDtypeStruct((t, D_MODEL), _F32), jax.ShapeDtypeStruct((b, m_rows, KV_LORA), _BF)),
        compiler_params=pltpu.CompilerParams(dimension_semantics=("arbitrary",), vmem_limit_bytes=VMEM_LIMIT),
        name="mlp_attn",
    )(page_table, x2d, oa, osm, p["w_out"], p["g_mlp"], p["w_up"], p["w_down"], p["g_final"], q, kvn, cache_c,
      cache_krt)


def _out_mlp(x2d, oa, osm, p, tm):
    t = x2d.shape[0]
    row_spec = lambda w: pl.BlockSpec((tm, w), lambda i: (i, 0))
    once = lambda shape: pl.BlockSpec(shape, lambda i: (0,) * len(shape), pipeline_mode=pl.Buffered(1))
    return pl.pallas_call(
        functools.partial(_out_mlp_kernel, f_chunk=1024),
        grid=(t // tm,),
        in_specs=[row_spec(D_MODEL), row_spec(D_ATTN), row_spec(D_SSM), once((D_MODEL, D_MODEL)),
                  once((1, D_MODEL)), once((D_MODEL, D_FF)), once((D_FF, D_MODEL)), once((1, D_MODEL))],
        out_specs=row_spec(D_MODEL),
        out_shape=jax.ShapeDtypeStruct((t, D_MODEL), _F32),
        compiler_params=pltpu.CompilerParams(dimension_semantics=("arbitrary",), vmem_limit_bytes=VMEM_LIMIT),
        name="out_mlp",
    )(x2d, oa, osm, p["w_out"], p["g_mlp"], p["w_up"], p["w_down"], p["g_final"])


def _prep_params(norm_attn, w_in, q_norm, kv_norm, w_uq, w_uk, w_uv, conv_w, conv_b, dt_bias, a_log, d_skip,
                 ssm_norm, w_out, norm_mlp, w_up, w_down, norm_final):
    half = QK_ROPE // 2

    def swap(w):
        return jnp.concatenate([w[..., half:], w[..., :half]], axis=-1)

    def place(w, lead, n):
        return jnp.pad(w, ((0, 0), (lead, n - lead - w.shape[1])))

    o1 = Q_LORA
    o2 = o1 + KV_LORA
    o3 = o2 + QK_ROPE
    o5 = o3 + D_SSM + CONV_DIM
    w_kpe = w_in[:, o2:o3]

    def w_in_padded(lead):
        return jnp.concatenate([
            w_in[:, :o2], place(w_kpe, lead, LANES), place(swap(w_kpe), lead, LANES), w_in[:, o3:o5],
            place(w_in[:, o5:], 0, LANES)], axis=1).astype(_BF)

    uq = w_uq.reshape(Q_LORA, MLA_HEADS, QK_NOPE + QK_ROPE)
    uq_nope = uq[:, :, :QK_NOPE]
    uq_rope = uq[:, :, QK_NOPE:]
    w_uq_p = jnp.concatenate([
        jnp.pad(uq_nope, ((0, 0), (0, 0), (0, LANES - QK_NOPE))).reshape(Q_LORA, MLA_HEADS * LANES),
        uq_rope.reshape(Q_LORA, -1), swap(uq_rope).reshape(Q_LORA, -1)], axis=1).astype(_BF)
    w_uk_p = jnp.pad(w_uk, ((0, 0), (0, LANES - QK_NOPE), (0, 0))).astype(_BF)
    tail = LANES - QK_NOPE - QK_ROPE
    plain = jnp.pad(uq, ((0, 0), (0, 0), (0, tail)))
    swapped = jnp.pad(swap(uq_rope), ((0, 0), (0, 0), (QK_NOPE, tail)))
    w_uq_t = jnp.concatenate([plain.reshape(Q_LORA, -1), swapped.reshape(Q_LORA, -1)], axis=1).T.astype(_BF)
    w_k = jnp.pad(jnp.transpose(w_uk, (2, 0, 1)), ((0, 0), (0, 0), (0, LANES - QK_NOPE)))
    w_k = w_k.reshape(KV_LORA, MLA_HEADS * LANES).astype(_BF)
    w_uv_t = jnp.transpose(w_uv, (0, 2, 1)).reshape(D_ATTN, KV_LORA).astype(_BF)
    pad_heads = lambda v: jnp.pad(v.reshape(1, SSM_HEADS), ((0, 0), (0, LANES - SSM_HEADS)))
    return dict(
        g_attn=norm_attn.reshape(1, D_MODEL), w_in=w_in_padded(0), w_in_prompt=w_in_padded(QK_NOPE),
        q_norm=q_norm.reshape(1, Q_LORA), kv_norm=kv_norm.reshape(1, KV_LORA), w_uq=w_uq_p, w_uk=w_uk_p,
        w_uq_t=w_uq_t, w_k=w_k, w_uv_t=w_uv_t, w_uv=w_uv.astype(_BF),
        conv_w=conv_w, conv_b=conv_b.reshape(1, CONV_DIM), dt_bias=pad_heads(dt_bias), a_log=pad_heads(a_log),
        d_skip=jnp.repeat(d_skip, SSM_HEADDIM).reshape(1, D_SSM), ssm_norm=ssm_norm.reshape(1, D_SSM),
        sel=jnp.asarray(np.tile(np.eye(SSM_HEADS, LANES), (1, 3)), _BF),
        head_to_cols=jnp.asarray(np.tile(np.repeat(np.eye(LANES, SSM_HEADS), SSM_HEADDIM, axis=1), (3, 1)), _BF),
        head_to_lanes=jnp.asarray(np.tile(np.repeat(np.eye(LANES, SSM_HEADS), LANES, axis=1), (3, 1)), _BF),
        w_out=w_out.astype(_BF), g_mlp=norm_mlp.reshape(1, D_MODEL), w_up=w_up.astype(_BF),
        w_down=w_down.astype(_BF), g_final=norm_final.reshape(1, D_MODEL),
    )


def _largest_divisor(n, cap):
    d = min(n, cap)
    while n % d:
        d -= 1
    return d


def kernel(x_prompt, x_sample, cache_kv_latent, cache_k_rope, state_ssm, state_conv, page_table, norm_attn, w_in, q_norm, kv_norm, w_uq, w_uk, w_uv, conv_w, conv_b, dt_bias, a_log, d_skip, ssm_norm, w_out, norm_mlp, w_up, w_down, norm_final):
    assert norm_attn.shape[0] == 1, "single-layer model"
    p = _prep_params(norm_attn[0], w_in[0], q_norm[0], kv_norm[0], w_uq[0], w_uk[0], w_uv[0], conv_w[0], conv_b[0],
                     dt_bias[0], a_log[0], d_skip[0], ssm_norm[0], w_out[0], norm_mlp[0], w_up[0], w_down[0],
                     norm_final)
    bp, lp, _ = x_prompt.shape
    bs, ls, _ = x_sample.shape
    n_pages = page_table.shape[1]
    page = cache_kv_latent.shape[2]
    past_len = n_pages * page

    xp2 = x_prompt.reshape(bp * lp, D_MODEL)
    xs2 = x_sample.reshape(bs * ls, D_MODEL)
    qt, kh, vt, c_p, kr_p, z_p, xbc_p, dtr_p = _in_proj_prompt(xp2, bp, lp, p, _largest_divisor(lp, 512))
    tb = _largest_divisor(bs, 512 // ls)
    q_s, kv_s, c_s, kr_s, z_s, xbc_s, dtr_s = _in_proj(xs2, bs, ls, past_len, p, tb, ls)

    o_attn_p = _attn_prompt(qt, kh, vt, bp, lp, _largest_divisor(lp, 512))
    chunk = _largest_divisor(lp, CHUNK)
    o_ssm_p, h_p, conv_p = _ssd(xbc_p, z_p, dtr_p, jnp.zeros((bp, CONV_K - 1, CONV_DIM), _F32),
                                jnp.zeros((bp, SSM_HEADS, SSM_HEADDIM, D_STATE), _F32), p, bp, lp,
                                chunk, 1, _largest_divisor(lp // chunk, 2))

    cache_krt = jnp.swapaxes(cache_k_rope[0], 1, 2)
    y_p, o_lat = _mlp_attn(xp2, o_attn_p, o_ssm_p, q_s.reshape(bs, MLA_HEADS * ls, QK_PAD),
                           kv_s.reshape(bs, ls, QK_PAD), cache_kv_latent[0], cache_krt, page_table, p,
                           _largest_divisor(bp * lp, 256))
    outs_p = (y_p.reshape(bp, lp, D_MODEL), c_p.reshape(1, bp, lp, KV_LORA), kr_p.reshape(1, bp, lp, QK_ROPE),
              h_p[None], conv_p[None])

    o_attn_s = _uv_proj(o_lat.reshape(bs, MLA_HEADS, ls, KV_LORA), p["w_uv"], tb)
    o_ssm_s, h_s, conv_s = _ssd(xbc_s, z_s, dtr_s, state_conv[0], state_ssm[0], p, bs, ls, ls,
                                _largest_divisor(bs, 16), 1)
    y_s = _out_mlp(xs2, o_attn_s, o_ssm_s, p, _largest_divisor(bs * ls, 512))
    outs_s = (y_s.reshape(bs, ls, D_MODEL), c_s.reshape(1, bs, ls, KV_LORA), kr_s.reshape(1, bs, ls, QK_ROPE),
              h_s[None], conv_s[None])

    return (outs_p[0], outs_s[0], outs_p[1], outs_p[2], outs_p[3], outs_p[4],
            outs_s[1], outs_s[2], outs_s[3], outs_s[4])
```

```python
import functools

import jax
import jax.numpy as jnp
import numpy as np
from jax import lax
from jax.experimental import pallas as pl
from jax.experimental.pallas import tpu as pltpu

D_MODEL = 1024
D_ATTN = 512
D_SSM = 512
V_HEAD = 64
MLA_HEADS = 8
QK_NOPE = 64
QK_ROPE = 32
KV_LORA = 256
Q_LORA = 384
ROPE_THETA = 10000.0
SSM_HEADDIM = 64
SSM_HEADS = 8
SSM_GROUPS = 2
D_STATE = 128
CONV_K = 4
CHUNK = 256
CONV_DIM = D_SSM + 2 * SSM_GROUPS * D_STATE
D_FF = 4096
EPS = 1e-6

LANES = 128
SUBLANES = 8
QK_AHEAD = 2
SPLITS_AHEAD = 8
QK_PAD = KV_LORA + LANES
VMEM_LIMIT = 56 * 1024 * 1024
SCALE_LOG2 = (QK_NOPE + QK_ROPE) ** -0.5 * 1.4426950408889634

_C_QLAT = 0
_C_KV = _C_QLAT + Q_LORA
_C_KPE = _C_KV + KV_LORA
_C_KPES = _C_KPE + LANES
_C_Z = _C_KPES + LANES
_C_XBC = _C_Z + D_SSM
_C_DT = _C_XBC + CONV_DIM
_N_IN_PAD = _C_DT + LANES
_Q_NOPE = 0
_Q_ROPE = MLA_HEADS * LANES
_Q_ROPES = _Q_ROPE + MLA_HEADS * QK_ROPE
_N_UQ_PAD = _Q_ROPES + MLA_HEADS * QK_ROPE

_BF = jnp.bfloat16
_F32 = jnp.float32


def _dot(a, b):
    return jnp.dot(a, b, preferred_element_type=_F32)


def _dot_nt(a, b):
    return lax.dot_general(a, b, (((1,), (1,)), ((), ())), preferred_element_type=_F32)


def _dot_tn(a, b):
    return lax.dot_general(a, b, (((0,), (0,)), ((), ())), preferred_element_type=_F32)


def _split3(a):
    hi = a.astype(_BF)
    r1 = a - hi.astype(_F32)
    mid = r1.astype(_BF)
    lo = (r1 - mid.astype(_F32)).astype(_BF)
    return jnp.concatenate([hi, mid, lo], axis=1)


def _rms(x, g):
    return x * lax.rsqrt(jnp.mean(x * x, axis=-1, keepdims=True) + EPS) * g


def _silu(x):
    return x * (1.0 / (1.0 + jnp.exp(-x)))


def _sublane_all(op, x):
    for shift in (4, 2, 1):
        x = op(x, pltpu.roll(x, shift, 0))
    return x


def _const_spec(shape):
    nd = len(shape)
    return pl.BlockSpec(shape, lambda *_: (0,) * nd)


def _rope_tables(l, past_len):
    pos = past_len + np.arange(l, dtype=np.float64)
    inv = ROPE_THETA ** (-(np.arange(0, QK_ROPE, 2, dtype=np.float64) / QK_ROPE))
    ang = pos[:, None] * inv[None, :]
    cos, sin = np.cos(ang).astype(np.float32), np.sin(ang).astype(np.float32)
    return np.concatenate([cos, cos], axis=1), np.concatenate([-sin, sin], axis=1)


_XP_OFF = 8


def _conv_silu(win_ref, x_new, cw_ref, cb_ref, q):
    win_ref[_XP_OFF:_XP_OFF + q, :] = x_new
    conv = cb_ref[...]
    for k in range(CONV_K):
        conv = conv + win_ref[_XP_OFF - (CONV_K - 1) + k:_XP_OFF - (CONV_K - 1) + k + q, :] * cw_ref[k:k + 1, :]
    tail = win_ref[_XP_OFF + q - (CONV_K - 1):_XP_OFF + q, :]
    win_ref[_XP_OFF - (CONV_K - 1):_XP_OFF, :] = tail
    return _silu(conv), tail


def _in_proj_kernel(x_ref, g_ref, w_ref, qn_ref, kvn_ref, wuq_ref, wuk_ref, cosq_ref, sinq_ref, cosk_ref, sink_ref,
                    q_ref, kv_ref, c_ref, kr_ref, z_ref, xbc_ref, dt_ref, *, tb, tl):
    tm = tb * tl

    def rows(tab_ref):
        t = tab_ref[...]
        if tb == 1:
            return t
        return jnp.broadcast_to(t[None], (tb, tl, t.shape[-1])).reshape(tm, t.shape[-1])

    h = _rms(x_ref[...], g_ref[...]).astype(_BF)
    proj = _dot(h, w_ref[...])
    z_ref[...] = proj[:, _C_Z:_C_XBC]
    xbc_ref[...] = proj[:, _C_XBC:_C_DT]
    dt_ref[...] = proj[:, _C_DT:_N_IN_PAD]

    c = _rms(proj[:, _C_KV:_C_KPE], kvn_ref[...])
    kr = proj[:, _C_KPE:_C_KPES] * rows(cosk_ref) + proj[:, _C_KPES:_C_Z] * rows(sink_ref)
    c_ref[...] = c
    kr_ref[...] = kr[:, :QK_ROPE]
    kv_ref[:, :KV_LORA] = c.astype(_BF)
    kv_ref[:, KV_LORA:] = kr.astype(_BF)

    qn = _rms(proj[:, _C_QLAT:_C_KV], qn_ref[...]).astype(_BF)
    qq = _dot(qn, wuq_ref[...])
    q_pe = qq[:, _Q_ROPE:_Q_ROPES] * rows(cosq_ref) + qq[:, _Q_ROPES:_N_UQ_PAD] * rows(sinq_ref)
    lane = lax.broadcasted_iota(jnp.int32, (tm, LANES), 1)
    heads_per_group = LANES // QK_ROPE
    for hd in range(MLA_HEADS):
        q_abs = _dot(qq[:, hd * LANES:(hd + 1) * LANES].astype(_BF), wuk_ref[hd]) * SCALE_LOG2
        grp = q_pe[:, (hd // heads_per_group) * LANES:(hd // heads_per_group + 1) * LANES]
        shift = (LANES - QK_ROPE * (hd % heads_per_group)) % LANES
        if shift:
            grp = pltpu.roll(grp, shift, 1)
        pe = jnp.where(lane < QK_ROPE, grp * SCALE_LOG2, 0.0)
        q_ref[:, hd, :, :KV_LORA] = q_abs.astype(_BF).reshape(tb, tl, KV_LORA)
        q_ref[:, hd, :, KV_LORA:] = pe.astype(_BF).reshape(tb, tl, LANES)


def _in_proj(x2d, b, l, past_len, p, tb, tl):
    t = b * l
    tm = tb * tl
    n_l = l // tl
    cos32, sin32 = _rope_tables(l, past_len)
    padk = np.zeros((l, LANES - QK_ROPE), np.float32)
    cosk = np.concatenate([cos32, padk], axis=1)
    sink = np.concatenate([sin32, padk], axis=1)
    cosq = np.tile(cos32, (1, MLA_HEADS))
    sinq = np.tile(sin32, (1, MLA_HEADS))

    row_spec = lambda w: pl.BlockSpec((tm, w), lambda i: (i, 0))
    tab_spec = lambda w: pl.BlockSpec((tl, w), lambda i: (i % n_l, 0))
    out_shape = (
        jax.ShapeDtypeStruct((b, MLA_HEADS, l, QK_PAD), _BF),
        jax.ShapeDtypeStruct((t, QK_PAD), _BF),
        jax.ShapeDtypeStruct((t, KV_LORA), _F32),
        jax.ShapeDtypeStruct((t, QK_ROPE), _F32),
        jax.ShapeDtypeStruct((t, D_SSM), _F32),
        jax.ShapeDtypeStruct((t, CONV_DIM), _F32),
        jax.ShapeDtypeStruct((t, LANES), _F32),
    )
    out_specs = (
        pl.BlockSpec((tb, MLA_HEADS, tl, QK_PAD), lambda i: (i // n_l, 0, i % n_l, 0)),
        row_spec(QK_PAD), row_spec(KV_LORA), row_spec(QK_ROPE), row_spec(D_SSM), row_spec(CONV_DIM), row_spec(LANES),
    )
    in_specs = [
        row_spec(D_MODEL), _const_spec((1, D_MODEL)), _const_spec((D_MODEL, _N_IN_PAD)),
        _const_spec((1, Q_LORA)), _const_spec((1, KV_LORA)), _const_spec((Q_LORA, _N_UQ_PAD)),
        _const_spec((MLA_HEADS, LANES, KV_LORA)),
        tab_spec(MLA_HEADS * QK_ROPE), tab_spec(MLA_HEADS * QK_ROPE), tab_spec(LANES), tab_spec(LANES),
    ]
    return pl.pallas_call(
        functools.partial(_in_proj_kernel, tb=tb, tl=tl),
        grid=(t // tm,), in_specs=in_specs, out_specs=out_specs, out_shape=out_shape,
        compiler_params=pltpu.CompilerParams(dimension_semantics=("arbitrary",), vmem_limit_bytes=VMEM_LIMIT),
        name="in_proj",
    )(x2d, p["g_attn"], p["w_in"], p["q_norm"], p["kv_norm"], p["w_uq"], p["w_uk"], cosq, sinq, cosk, sink)


def _in_proj_prompt_kernel(x_ref, g_ref, w_ref, qn_ref, kvn_ref, wuqt_ref, wk_ref, wuvt_ref, cost_ref, sint_ref,
                           cosk_ref, sink_ref, qt_ref, kh_ref, vt_ref, c_ref, kr_ref, z_ref, xbc_ref, dt_ref):
    h = _rms(x_ref[...], g_ref[...]).astype(_BF)
    proj = _dot(h, w_ref[...])
    z_ref[...] = proj[:, _C_Z:_C_XBC]
    xbc_ref[...] = proj[:, _C_XBC:_C_DT]
    dt_ref[...] = proj[:, _C_DT:_N_IN_PAD]

    c = _rms(proj[:, _C_KV:_C_KPE], kvn_ref[...])
    kr = proj[:, _C_KPE:_C_KPES] * cosk_ref[...] + proj[:, _C_KPES:_C_Z] * sink_ref[...]
    c_ref[...] = c
    kr_ref[...] = kr[:, QK_NOPE:QK_NOPE + QK_ROPE]
    c_bf = c.astype(_BF)
    kn = _dot(c_bf, wk_ref[...])
    for hd in range(MLA_HEADS):
        kh_ref[:, hd * LANES:(hd + 1) * LANES] = (kn[:, hd * LANES:(hd + 1) * LANES] + kr).astype(_BF)
    vt_ref[0] = _dot_nt(wuvt_ref[...], c_bf).astype(_BF)

    qn = _rms(proj[:, _C_QLAT:_C_KV], qn_ref[...]).astype(_BF)
    qqt = _dot_nt(wuqt_ref[...], qn)
    half = MLA_HEADS * LANES
    for hd in range(MLA_HEADS):
        qt = (qqt[hd * LANES:(hd + 1) * LANES] * cost_ref[...]
              + qqt[half + hd * LANES:half + (hd + 1) * LANES] * sint_ref[...])
        qt_ref[0, hd] = qt.astype(_BF)


def _in_proj_prompt(x2d, b, l, p, tl):
    t = b * l
    n_l = l // tl
    cos32, sin32 = _rope_tables(l, 0)
    lead = np.zeros((l, QK_NOPE), np.float32)
    trail = np.zeros((l, LANES - QK_NOPE - QK_ROPE), np.float32)
    cosk = np.concatenate([lead, cos32, trail], axis=1)
    sink = np.concatenate([lead, sin32, trail], axis=1)
    cost = np.ascontiguousarray((np.concatenate([lead + 1.0, cos32, trail], axis=1) * np.float32(SCALE_LOG2)).T)
    sint = np.ascontiguousarray((sink * np.float32(SCALE_LOG2)).T)

    row_spec = lambda w: pl.BlockSpec((tl, w), lambda i: (i, 0))
    out_shape = (
        jax.ShapeDtypeStruct((b, MLA_HEADS, LANES, l), _BF),
        jax.ShapeDtypeStruct((t, MLA_HEADS * LANES), _BF),
        jax.ShapeDtypeStruct((b, D_ATTN, l), _BF),
        jax.ShapeDtypeStruct((t, KV_LORA), _F32),
        jax.ShapeDtypeStruct((t, QK_ROPE), _F32),
        jax.ShapeDtypeStruct((t, D_SSM), _F32),
        jax.ShapeDtypeStruct((t, CONV_DIM), _F32),
        jax.ShapeDtypeStruct((t, LANES), _F32),
    )
    out_specs = (
        pl.BlockSpec((1, MLA_HEADS, LANES, tl), lambda i: (i // n_l, 0, 0, i % n_l)),
        row_spec(MLA_HEADS * LANES),
        pl.BlockSpec((1, D_ATTN, tl), lambda i: (i // n_l, 0, i % n_l)),
        row_spec(KV_LORA), row_spec(QK_ROPE), row_spec(D_SSM), row_spec(CONV_DIM), row_spec(LANES),
    )
    in_specs = [
        row_spec(D_MODEL), _const_spec((1, D_MODEL)), _const_spec((D_MODEL, _N_IN_PAD)),
        _const_spec((1, Q_LORA)), _const_spec((1, KV_LORA)), _const_spec((2 * MLA_HEADS * LANES, Q_LORA)),
        _const_spec((KV_LORA, MLA_HEADS * LANES)), _const_spec((D_ATTN, KV_LORA)),
        pl.BlockSpec((LANES, tl), lambda i: (0, i % n_l)), pl.BlockSpec((LANES, tl), lambda i: (0, i % n_l)),
        pl.BlockSpec((tl, LANES), lambda i: (i % n_l, 0)), pl.BlockSpec((tl, LANES), lambda i: (i % n_l, 0)),
    ]
    return pl.pallas_call(
        _in_proj_prompt_kernel,
        grid=(t // tl,), in_specs=in_specs, out_specs=out_specs, out_shape=out_shape,
        compiler_params=pltpu.CompilerParams(dimension_semantics=("arbitrary",), vmem_limit_bytes=VMEM_LIMIT),
        name="in_proj_prompt",
    )(x2d, p["g_attn"], p["w_in_prompt"], p["q_norm"], p["kv_norm"], p["w_uq_t"], p["w_k"], p["w_uv_t"],
      cost, sint, cosk, sink)


def _attn_prompt_kernel(it_ref, jt_ref, qt_ref, kh_ref, vt_ref, o_ref, m_ref, l_ref, acc_ref, *, tq):
    s_idx = pl.program_id(1)
    i = it_ref[s_idx]
    j = jt_ref[s_idx]

    @pl.when(j == 0)
    def _():
        m_ref[...] = jnp.full(m_ref.shape, -jnp.inf, _F32)
        l_ref[...] = jnp.zeros(l_ref.shape, _F32)
        acc_ref[...] = jnp.zeros(acc_ref.shape, _F32)

    def step(masked):
        if masked:
            krow = lax.broadcasted_iota(jnp.int32, (tq, tq), 0)
            qcol = lax.broadcasted_iota(jnp.int32, (tq, tq), 1)
            keep = krow <= qcol

        def scores(hd):
            st = _dot(kh_ref[:, hd * LANES:(hd + 1) * LANES], qt_ref[0, hd])
            if masked:
                st = jnp.where(keep, st, -jnp.inf)
            return st.reshape(tq // SUBLANES, SUBLANES, tq)

        ahead = {hd: scores(hd) for hd in range(QK_AHEAD)}
        for hd in range(MLA_HEADS):
            if hd + QK_AHEAD < MLA_HEADS:
                ahead[hd + QK_AHEAD] = scores(hd + QK_AHEAD)
            st = ahead.pop(hd)
            m_prev = m_ref[hd]
            m_new = jnp.maximum(m_prev, _sublane_all(jnp.maximum, jnp.max(st, axis=0)))
            alpha = jnp.exp2(m_prev - m_new)
            pr = jnp.exp2(st - m_new[None])
            l_ref[hd] = alpha * l_ref[hd] + jnp.sum(pr, axis=0)
            rows = slice(hd * V_HEAD, (hd + 1) * V_HEAD)
            acc = acc_ref[rows, :].reshape(V_HEAD // SUBLANES, SUBLANES, tq) * alpha[None]
            acc_ref[rows, :] = acc.reshape(V_HEAD, tq) + _dot(vt_ref[0, rows, :], pr.reshape(tq, tq).astype(_BF))
            m_ref[hd] = m_new

    @pl.when(j < i)
    def _():
        step(False)

    @pl.when(j == i)
    def _():
        step(True)
        for hd in range(MLA_HEADS):
            rows = slice(hd * V_HEAD, (hd + 1) * V_HEAD)
            den = _sublane_all(jnp.add, l_ref[hd])
            acc = acc_ref[rows, :].reshape(V_HEAD // SUBLANES, SUBLANES, tq) / den[None]
            acc_ref[rows, :] = acc.reshape(V_HEAD, tq)
        o_ref[...] = acc_ref[...].T.astype(o_ref.dtype)


def _attn_prompt(qt, kh, vt, b, l, tq):
    nq = l // tq
    pairs = [(i, j) for i in range(nq) for j in range(i + 1)]
    it = jnp.asarray([pr[0] for pr in pairs], jnp.int32)
    jt = jnp.asarray([pr[1] for pr in pairs], jnp.int32)
    grid_spec = pltpu.PrefetchScalarGridSpec(
        num_scalar_prefetch=2,
        grid=(b, len(pairs)),
        in_specs=[
            pl.BlockSpec((1, MLA_HEADS, LANES, tq), lambda bi, s, it, jt: (bi, 0, 0, it[s])),
            pl.BlockSpec((tq, MLA_HEADS * LANES), lambda bi, s, it, jt: (bi * nq + jt[s], 0)),
            pl.BlockSpec((1, D_ATTN, tq), lambda bi, s, it, jt: (bi, 0, jt[s])),
        ],
        out_specs=pl.BlockSpec((tq, D_ATTN), lambda bi, s, it, jt: (bi * nq + it[s], 0)),
        scratch_shapes=[pltpu.VMEM((MLA_HEADS, SUBLANES, tq), _F32), pltpu.VMEM((MLA_HEADS, SUBLANES, tq), _F32),
                        pltpu.VMEM((D_ATTN, tq), _F32)],
    )
    return pl.pallas_call(
        functools.partial(_attn_prompt_kernel, tq=tq),
        grid_spec=grid_spec,
        out_shape=jax.ShapeDtypeStruct((b * l, D_ATTN), _BF),
        compiler_params=pltpu.CompilerParams(dimension_semantics=("arbitrary", "arbitrary"),
                                             vmem_limit_bytes=VMEM_LIMIT),
        name="attn_prompt",
    )(it, jt, qt, kh, vt)


def _page_copies(pt_ref, cache_c_ref, cache_krt_ref, cbuf_ref, krbuf_ref, sem_ref, item, slot, pg, page):
    pid = pt_ref[item, pg]
    dst = pl.ds(pl.multiple_of(pg * page, page), page)
    return (pltpu.make_async_copy(cache_c_ref.at[pid], cbuf_ref.at[slot, dst, :], sem_ref.at[slot, 0]),
            pltpu.make_async_copy(cache_krt_ref.at[pid], krbuf_ref.at[slot, :, dst], sem_ref.at[slot, 1]))


def _wait_item(cbuf_ref, krbuf_ref, sem_ref, slot):
    pltpu.make_async_copy(cbuf_ref.at[slot], cbuf_ref.at[slot], sem_ref.at[slot, 0]).wait()
    pltpu.make_async_copy(krbuf_ref.at[slot], krbuf_ref.at[slot], sem_ref.at[slot, 1]).wait()


def _attend_item(q, kvn, slot, cbuf_ref, krbuf_ref, kc_ref, *, n_pages, page, l_new, n_split):
    m_rows = MLA_HEADS * l_new
    q_abs = q[:, :KV_LORA]
    q_pe = q[:, KV_LORA:KV_LORA + QK_ROPE]
    ks = (n_pages // n_split) * page

    def scores(sp):
        rows = pl.ds(sp * ks, ks)
        kc_ref[rows, :] = cbuf_ref[slot, rows, :].astype(_BF)
        return _dot_nt(q_abs, kc_ref[rows, :]) + _dot(q_pe, krbuf_ref[slot, :, rows].astype(_BF))

    def partial(s, values):
        m = jnp.max(s, axis=1, keepdims=True)
        pr = jnp.exp2(s - m)
        return m, jnp.sum(pr, axis=1, keepdims=True), _dot(pr.astype(_BF), values)

    def merge(run, new):
        m = jnp.maximum(run[0], new[0])
        a, c = jnp.exp2(run[0] - m), jnp.exp2(new[0] - m)
        return m, a * run[1] + c * new[1], a * run[2] + c * new[2]

    ahead = {sp: scores(sp) for sp in range(min(SPLITS_AHEAD, n_split))}
    sn = _dot_nt(q, kvn)
    qpos = lax.broadcasted_iota(jnp.int32, (MLA_HEADS, l_new, l_new), 1).reshape(m_rows, l_new)
    kpos = lax.broadcasted_iota(jnp.int32, (m_rows, l_new), 1)
    run = partial(jnp.where(kpos <= qpos, sn, -jnp.inf), kvn[:, :KV_LORA])
    for sp in range(n_split):
        if sp + SPLITS_AHEAD < n_split:
            ahead[sp + SPLITS_AHEAD] = scores(sp + SPLITS_AHEAD)
        run = merge(run, partial(ahead.pop(sp), kc_ref[pl.ds(sp * ks, ks), :]))
    return run[2] / run[1]


def _uv_proj_kernel(o_ref, wuv_ref, out_ref, *, tb, l_new):
    for hd in range(MLA_HEADS):
        o = o_ref[:, hd].reshape(tb * l_new, KV_LORA)
        out_ref[:, hd * V_HEAD:(hd + 1) * V_HEAD] = _dot(o, wuv_ref[hd]).astype(out_ref.dtype)


def _uv_proj(o_lat, wuv, tb):
    b, _, l_new, _ = o_lat.shape
    return pl.pallas_call(
        functools.partial(_uv_proj_kernel, tb=tb, l_new=l_new),
        grid=(b // tb,),
        in_specs=[pl.BlockSpec((tb, MLA_HEADS, l_new, KV_LORA), lambda i: (i, 0, 0, 0)),
                  _const_spec((MLA_HEADS, KV_LORA, V_HEAD))],
        out_specs=pl.BlockSpec((tb * l_new, D_ATTN), lambda i: (i, 0)),
        out_shape=jax.ShapeDtypeStruct((b * l_new, D_ATTN), _BF),
        compiler_params=pltpu.CompilerParams(dimension_semantics=("arbitrary",), vmem_limit_bytes=VMEM_LIMIT),
        name="uv_proj",
    )(o_lat, wuv)


def _ssd_kernel(xbc_ref, z_ref, dtr_ref, cbuf_ref, h0_ref, cw_ref, cb_ref, dtb_ref, alog_ref, dskip_ref, nrm_ref,
                sel_ref, ex_ref, exw_ref, o_ref, h_ref, cnew_ref, xp_ref, y_ref, *, n_seq, n_chunk, q):
    assert n_seq == 1 or n_chunk == 1
    g_items = n_seq * n_chunk
    c_idx = pl.program_id(1)
    last = pl.num_programs(1) - 1
    heads_per_group = SSM_HEADS // SSM_GROUPS
    rows = g_items * q
    row = lax.broadcasted_iota(jnp.int32, (q, q), 0)
    col = lax.broadcasted_iota(jnp.int32, (q, q), 1)
    causal = row >= col
    n_bc = SSM_GROUPS * D_STATE

    def conv_item(g):
        sq, ck = divmod(g, n_chunk)
        rs = slice(g * q, (g + 1) * q)

        if ck == 0:
            @pl.when(c_idx == 0)
            def _():
                h_ref[sq] = h0_ref[sq]
                xp_ref[sq, _XP_OFF - (CONV_K - 1):_XP_OFF, :] = cbuf_ref[sq]

        act, tail = _conv_silu(xp_ref.at[sq], xbc_ref[rs, :], cw_ref, cb_ref, q)

        if ck == n_chunk - 1:
            @pl.when(c_idx == last)
            def _():
                cnew_ref[sq] = tail

        return act[:, :D_SSM], act[:, D_SSM:D_SSM + n_bc].astype(_BF), act[:, D_SSM + n_bc:].astype(_BF)

    convs = {g: conv_item(g) for g in range(g_items)} if n_seq == 1 else {}

    if g_items == 1:
        tril = jnp.where(causal, 1.0, 0.0).astype(_BF)
    else:
        r_all = lax.broadcasted_iota(jnp.int32, (rows, rows), 0)
        c_all = lax.broadcasted_iota(jnp.int32, (rows, rows), 1)
        same_item = (r_all // q) == (c_all // q)
        tril = jnp.where(r_all >= c_all, jnp.where(same_item, 1.0, 0.0), 0.0).astype(_BF)
    a_neg = -jnp.exp(alog_ref[...])
    dtv = dtr_ref[...] + dtb_ref[...]
    dt = jnp.maximum(dtv, 0.0) + jnp.log1p(jnp.exp(-jnp.abs(dtv)))
    cs = _dot(tril, _split3(dt * a_neg))
    acs = cs[:, :LANES] + cs[:, LANES:2 * LANES] + cs[:, 2 * LANES:]
    acs3 = _split3(acs)
    acs_t = _dot_nt(sel_ref[...], acs3)
    dt_x = _dot(_split3(dt), ex_ref[...])
    acs_x = _dot(acs3, ex_ref[...])
    acs_w = _dot(acs3, exw_ref[...])
    e_acs = jnp.exp(acs_x)

    for g in range(g_items):
        sq = g // n_chunk
        rs = slice(g * q, (g + 1) * q)
        xs, bmat, cmat = convs[g] if g in convs else conv_item(g)
        acs_last_x = acs_x[(g + 1) * q - 1:(g + 1) * q, :]
        x_dt = xs * dt_x[rs]
        xw = (x_dt * jnp.exp(acs_last_x - acs_x[rs])).astype(_BF)
        x_dt = x_dt.astype(_BF)
        for grp in range(SSM_GROUPS):
            b_g = bmat[:, grp * D_STATE:(grp + 1) * D_STATE]
            c_g = cmat[:, grp * D_STATE:(grp + 1) * D_STATE]
            cb = _dot_nt(c_g, b_g)
            for hd in range(grp * heads_per_group, (grp + 1) * heads_per_group):
                cols = slice(hd * SSM_HEADDIM, (hd + 1) * SSM_HEADDIM)
                acs_h = acs_w[rs, hd * LANES:(hd + 1) * LANES]
                acs_h = acs_h[:, :q] if q <= LANES else jnp.concatenate([acs_h] * (q // LANES), axis=1)
                lmat = jnp.exp(jnp.where(causal, acs_h - acs_t[hd:hd + 1, rs], -jnp.inf))
                h_prev = h_ref[sq, hd]
                y = _dot((cb * lmat).astype(_BF), x_dt[:, cols])
                y = y + _dot_nt(c_g, h_prev.astype(_BF)) * e_acs[rs, cols]
                decay = jnp.exp(acs[(g + 1) * q - 1:(g + 1) * q, hd:hd + 1])
                h_ref[sq, hd] = decay * h_prev + _dot_tn(xw[:, cols], b_g)
                y_ref[:, cols] = y
        yv = y_ref[...] + dskip_ref[...] * xs
        gated = yv * _silu(z_ref[g * q:(g + 1) * q, :])
        gw = D_SSM // SSM_GROUPS
        parts = []
        for grp in range(SSM_GROUPS):
            gg = gated[:, grp * gw:(grp + 1) * gw]
            parts.append(gg * lax.rsqrt(jnp.mean(gg * gg, axis=-1, keepdims=True) + EPS))
        o_ref[g * q:(g + 1) * q, :] = (jnp.concatenate(parts, axis=1) * nrm_ref[...]).astype(o_ref.dtype)


def _ssd(xbc, z, dtr, conv_buf, h0, p, b, l, q, n_seq, n_chunk):
    nc = l // (q * n_chunk)
    rows = n_seq * n_chunk * q
    row_spec = lambda w: pl.BlockSpec((rows, w), lambda bi, c: (bi * nc + c, 0))
    state_spec = pl.BlockSpec((n_seq, SSM_HEADS, SSM_HEADDIM, D_STATE), lambda bi, c: (bi, 0, 0, 0))
    hist_spec = pl.BlockSpec((n_seq, CONV_K - 1, CONV_DIM), lambda bi, c: (bi, 0, 0))
    consts = [p["dt_bias"], p["a_log"], p["d_skip"], p["ssm_norm"], p["sel"], p["head_to_cols"], p["head_to_lanes"]]
    const_specs = [_const_spec((1, LANES)), _const_spec((1, LANES)), _const_spec((1, D_SSM)), _const_spec((1, D_SSM)),
                   _const_spec((SSM_HEADS, 3 * LANES)), _const_spec((3 * LANES, D_SSM)),
                   _const_spec((3 * LANES, SSM_HEADS * LANES))]
    in_specs = [row_spec(CONV_DIM), row_spec(D_SSM), row_spec(LANES), hist_spec, state_spec,
                _const_spec((CONV_K, CONV_DIM)), _const_spec((1, CONV_DIM)), *const_specs]
    out_shape = (jax.ShapeDtypeStruct((b * l, D_SSM), _BF),
                 jax.ShapeDtypeStruct((b, SSM_HEADS, SSM_HEADDIM, D_STATE), _F32),
                 jax.ShapeDtypeStruct((b, CONV_K - 1, CONV_DIM), _F32))
    return pl.pallas_call(
        functools.partial(_ssd_kernel, n_seq=n_seq, n_chunk=n_chunk, q=q),
        grid=(b // n_seq, nc), in_specs=in_specs, out_specs=(row_spec(D_SSM), state_spec, hist_spec),
        out_shape=out_shape,
        scratch_shapes=[pltpu.VMEM((n_seq, _XP_OFF + q, CONV_DIM), _F32), pltpu.VMEM((q, D_SSM), _F32)],
        compiler_params=pltpu.CompilerParams(dimension_semantics=("arbitrary", "arbitrary"),
                                             vmem_limit_bytes=VMEM_LIMIT),
        name="ssd",
    )(xbc, z, dtr, conv_buf, h0, p["conv_w"], p["conv_b"], *consts)


def _mlp_chunk(h2, acc, wu_ref, wd_ref, f0, f_chunk):
    u = jnp.maximum(_dot(h2, wu_ref[:, f0:f0 + f_chunk]), 0.0)
    return acc + _dot((u * u).astype(_BF), wd_ref[f0:f0 + f_chunk, :])


def _out_mlp_kernel(x_ref, oa_ref, os_ref, wo_ref, gm_ref, wu_ref, wd_ref, gf_ref, y_ref, *, f_chunk):
    mix = jnp.concatenate([oa_ref[...], os_ref[...]], axis=1)
    x1 = x_ref[...] + _dot(mix, wo_ref[...])
    h2 = _rms(x1, gm_ref[...]).astype(_BF)
    acc = x1
    for f0 in range(0, D_FF, f_chunk):
        acc = _mlp_chunk(h2, acc, wu_ref, wd_ref, f0, f_chunk)
    y_ref[...] = _rms(acc, gf_ref[...])


def _mlp_attn_kernel(pt_ref, x_ref, oa_ref, os_ref, wo_ref, gm_ref, wu_ref, wd_ref, gf_ref, q_ref, kvn_ref,
                     cache_c_ref, cache_krt_ref, y_ref, o_ref, cbuf_ref, krbuf_ref, kc_ref, sem_ref, *,
                     f_chunk, ipb, n_items, n_pages, page, l_new, n_split):
    i = pl.program_id(0)
    copies = functools.partial(_page_copies, pt_ref, cache_c_ref, cache_krt_ref, cbuf_ref, krbuf_ref, sem_ref)

    @pl.when(i == 0)
    def _():
        def body(pg, carry):
            for cp in copies(0, 0, pg, page):
                cp.start()
            return carry
        lax.fori_loop(0, n_pages, body, 0)

    mix = jnp.concatenate([oa_ref[...], os_ref[...]], axis=1)
    x1 = x_ref[...] + _dot(mix, wo_ref[...])
    h2 = _rms(x1, gm_ref[...]).astype(_BF)
    acc = x1
    f_starts = list(range(0, D_FF, f_chunk))
    per_item = -(-len(f_starts) // ipb)
    for k in range(ipb):
        item = i * ipb + k
        slot = k % 2 if ipb % 2 == 0 else item % 2
        _wait_item(cbuf_ref, krbuf_ref, sem_ref, slot)
        nxt = jnp.minimum(item + 1, n_items - 1)
        for pg in range(n_pages):
            for cp in copies(nxt, 1 - slot, pg, page):
                cp.start()
        o = _attend_item(q_ref[k], kvn_ref[k], slot, cbuf_ref, krbuf_ref, kc_ref, n_pages=n_pages, page=page,
                         l_new=l_new, n_split=n_split)
        o_ref[k] = o.astype(o_ref.dtype)
        for f0 in f_starts[k * per_item:(k + 1) * per_item]:
            acc = _mlp_chunk(h2, acc, wu_ref, wd_ref, f0, f_chunk)
    for f0 in f_starts[ipb * per_item:]:
        acc = _mlp_chunk(h2, acc, wu_ref, wd_ref, f0, f_chunk)
    y_ref[...] = _rms(acc, gf_ref[...])

    @pl.when(i == pl.num_programs(0) - 1)
    def _():
        _wait_item(cbuf_ref, krbuf_ref, sem_ref, n_items % 2)


def _mlp_attn(x2d, oa, osm, q, kvn, cache_c, cache_krt, page_table, p, tm):
    t = x2d.shape[0]
    n_steps = t // tm
    b, m_rows, _ = q.shape
    assert b % n_steps == 0, "sample batch must split evenly over the MLP token tiles"
    ipb = b // n_steps
    l_new = m_rows // MLA_HEADS
    n_pages = page_table.shape[1]
    page = cache_c.shape[1]
    keys = n_pages * page
    row_spec = lambda w: pl.BlockSpec((tm, w), lambda i, pt: (i, 0))
    once = lambda shape: pl.BlockSpec(shape, lambda i, pt: (0,) * len(shape), pipeline_mode=pl.Buffered(1))
    item_spec = lambda r, w: pl.BlockSpec((ipb, r, w), lambda i, pt: (i, 0, 0))
    grid_spec = pltpu.PrefetchScalarGridSpec(
        num_scalar_prefetch=1,
        grid=(n_steps,),
        in_specs=[row_spec(D_MODEL), row_spec(D_ATTN), row_spec(D_SSM), once((D_MODEL, D_MODEL)),
                  once((1, D_MODEL)), once((D_MODEL, D_FF)), once((D_FF, D_MODEL)), once((1, D_MODEL)),
                  item_spec(m_rows, QK_PAD), item_spec(l_new, QK_PAD),
                  pl.BlockSpec(memory_space=pl.ANY), pl.BlockSpec(memory_space=pl.ANY)],
        out_specs=(row_spec(D_MODEL), item_spec(m_rows, KV_LORA)),
        scratch_shapes=[pltpu.VMEM((2, keys, KV_LORA), _F32), pltpu.VMEM((2, QK_ROPE, keys), _F32),
                        pltpu.VMEM((keys, KV_LORA), _BF), pltpu.SemaphoreType.DMA((2, 2))],
    )
    return pl.pallas_call(
        functools.partial(_mlp_attn_kernel, f_chunk=1024, ipb=ipb, n_items=b, n_pages=n_pages, page=page,
                          l_new=l_new, n_split=_largest_divisor(n_pages, 8)),
        grid_spec=grid_spec,
        out_shape=(jax.ShapeDtypeStruct((t, D_MODEL), _F32), jax.ShapeDtypeStruct((b, m_rows, KV_LORA), _BF)),
        compiler_params=pltpu.CompilerParams(dimension_semantics=("arbitrary",), vmem_limit_bytes=VMEM_LIMIT),
        name="mlp_attn",
    )(page_table, x2d, oa, osm, p["w_out"], p["g_mlp"], p["w_up"], p["w_down"], p["g_final"], q, kvn, cache_c,
      cache_krt)


def _out_mlp(x2d, oa, osm, p, tm):
    t = x2d.shape[0]
    row_spec = lambda w: pl.BlockSpec((tm, w), lambda i: (i, 0))
    once = lambda shape: pl.BlockSpec(shape, lambda i: (0,) * len(shape), pipeline_mode=pl.Buffered(1))
    return pl.pallas_call(
        functools.partial(_out_mlp_kernel, f_chunk=1024),
        grid=(t // tm,),
        in_specs=[row_spec(D_MODEL), row_spec(D_ATTN), row_spec(D_SSM), once((D_MODEL, D_MODEL)),
                  once((1, D_MODEL)), once((D_MODEL, D_FF)), once((D_FF, D_MODEL)), once((1, D_MODEL))],
        out_specs=row_spec(D_MODEL),
        out_shape=jax.ShapeDtypeStruct((t, D_MODEL), _F32),
        compiler_params=pltpu.CompilerParams(dimension_semantics=("arbitrary",), vmem_limit_bytes=VMEM_LIMIT),
        name="out_mlp",
    )(x2d, oa, osm, p["w_out"], p["g_mlp"], p["w_up"], p["w_down"], p["g_final"])


def _prep_params(norm_attn, w_in, q_norm, kv_norm, w_uq, w_uk, w_uv, conv_w, conv_b, dt_bias, a_log, d_skip,
                 ssm_norm, w_out, norm_mlp, w_up, w_down, norm_final):
    half = QK_ROPE // 2

    def swap(w):
        return jnp.concatenate([w[..., half:], w[..., :half]], axis=-1)

    def place(w, lead, n):
        return jnp.pad(w, ((0, 0), (lead, n - lead - w.shape[1])))

    o1 = Q_LORA
    o2 = o1 + KV_LORA
    o3 = o2 + QK_ROPE
    o5 = o3 + D_SSM + CONV_DIM
    w_kpe = w_in[:, o2:o3]

    def w_in_padded(lead):
        return jnp.concatenate([
            w_in[:, :o2], place(w_kpe, lead, LANES), place(swap(w_kpe), lead, LANES), w_in[:, o3:o5],
            place(w_in[:, o5:], 0, LANES)], axis=1).astype(_BF)

    uq = w_uq.reshape(Q_LORA, MLA_HEADS, QK_NOPE + QK_ROPE)
    uq_nope = uq[:, :, :QK_NOPE]
    uq_rope = uq[:, :, QK_NOPE:]
    w_uq_p = jnp.concatenate([
        jnp.pad(uq_nope, ((0, 0), (0, 0), (0, LANES - QK_NOPE))).reshape(Q_LORA, MLA_HEADS * LANES),
        uq_rope.reshape(Q_LORA, -1), swap(uq_rope).reshape(Q_LORA, -1)], axis=1).astype(_BF)
    w_uk_p = jnp.pad(w_uk, ((0, 0), (0, LANES - QK_NOPE), (0, 0))).astype(_BF)
    tail = LANES - QK_NOPE - QK_ROPE
    plain = jnp.pad(uq, ((0, 0), (0, 0), (0, tail)))
    swapped = jnp.pad(swap(uq_rope), ((0, 0), (0, 0), (QK_NOPE, tail)))
    w_uq_t = jnp.concatenate([plain.reshape(Q_LORA, -1), swapped.reshape(Q_LORA, -1)], axis=1).T.astype(_BF)
    w_k = jnp.pad(jnp.transpose(w_uk, (2, 0, 1)), ((0, 0), (0, 0), (0, LANES - QK_NOPE)))
    w_k = w_k.reshape(KV_LORA, MLA_HEADS * LANES).astype(_BF)
    w_uv_t = jnp.transpose(w_uv, (0, 2, 1)).reshape(D_ATTN, KV_LORA).astype(_BF)
    pad_heads = lambda v: jnp.pad(v.reshape(1, SSM_HEADS), ((0, 0), (0, LANES - SSM_HEADS)))
    return dict(
        g_attn=norm_attn.reshape(1, D_MODEL), w_in=w_in_padded(0), w_in_prompt=w_in_padded(QK_NOPE),
        q_norm=q_norm.reshape(1, Q_LORA), kv_norm=kv_norm.reshape(1, KV_LORA), w_uq=w_uq_p, w_uk=w_uk_p,
        w_uq_t=w_uq_t, w_k=w_k, w_uv_t=w_uv_t, w_uv=w_uv.astype(_BF),
        conv_w=conv_w, conv_b=conv_b.reshape(1, CONV_DIM), dt_bias=pad_heads(dt_bias), a_log=pad_heads(a_log),
        d_skip=jnp.repeat(d_skip, SSM_HEADDIM).reshape(1, D_SSM), ssm_norm=ssm_norm.reshape(1, D_SSM),
        sel=jnp.asarray(np.tile(np.eye(SSM_HEADS, LANES), (1, 3)), _BF),
        head_to_cols=jnp.asarray(np.tile(np.repeat(np.eye(LANES, SSM_HEADS), SSM_HEADDIM, axis=1), (3, 1)), _BF),
        head_to_lanes=jnp.asarray(np.tile(np.repeat(np.eye(LANES, SSM_HEADS), LANES, axis=1), (3, 1)), _BF),
        w_out=w_out.astype(_BF), g_mlp=norm_mlp.reshape(1, D_MODEL), w_up=w_up.astype(_BF),
        w_down=w_down.astype(_BF), g_final=norm_final.reshape(1, D_MODEL),
    )


def _largest_divisor(n, cap):
    d = min(n, cap)
    while n % d:
        d -= 1
    return d


def kernel(x_prompt, x_sample, cache_kv_latent, cache_k_rope, state_ssm, state_conv, page_table, norm_attn, w_in, q_norm, kv_norm, w_uq, w_uk, w_uv, conv_w, conv_b, dt_bias, a_log, d_skip, ssm_norm, w_out, norm_mlp, w_up, w_down, norm_final):
    assert norm_attn.shape[0] == 1, "single-layer model"
    p = _prep_params(norm_attn[0], w_in[0], q_norm[0], kv_norm[0], w_uq[0], w_uk[0], w_uv[0], conv_w[0], conv_b[0],
                     dt_bias[0], a_log[0], d_skip[0], ssm_norm[0], w_out[0], norm_mlp[0], w_up[0], w_down[0],
                     norm_final)
    bp, lp, _ = x_prompt.shape
    bs, ls, _ = x_sample.shape
    n_pages = page_table.shape[1]
    page = cache_kv_latent.shape[2]
    past_len = n_pages * page

    xp2 = x_prompt.reshape(bp * lp, D_MODEL)
    xs2 = x_sample.reshape(bs * ls, D_MODEL)
    qt, kh, vt, c_p, kr_p, z_p, xbc_p, dtr_p = _in_proj_prompt(xp2, bp, lp, p, _largest_divisor(lp, 512))
    tb = _largest_divisor(bs, 512 // ls)
    q_s, kv_s, c_s, kr_s, z_s, xbc_s, dtr_s = _in_proj(xs2, bs, ls, past_len, p, tb, ls)

    o_attn_p = _attn_prompt(qt, kh, vt, bp, lp, _largest_divisor(lp, 512))
    chunk = _largest_divisor(lp, CHUNK)
    o_ssm_p, h_p, conv_p = _ssd(xbc_p, z_p, dtr_p, jnp.zeros((bp, CONV_K - 1, CONV_DIM), _F32),
                                jnp.zeros((bp, SSM_HEADS, SSM_HEADDIM, D_STATE), _F32), p, bp, lp,
                                chunk, 1, _largest_divisor(lp // chunk, 2))

    cache_krt = jnp.swapaxes(cache_k_rope[0], 1, 2)
    y_p, o_lat = _mlp_attn(xp2, o_attn_p, o_ssm_p, q_s.reshape(bs, MLA_HEADS * ls, QK_PAD),
                           kv_s.reshape(bs, ls, QK_PAD), cache_kv_latent[0], cache_krt, page_table, p,
                           _largest_divisor(bp * lp, 256))
    outs_p = (y_p.reshape(bp, lp, D_MODEL), c_p.reshape(1, bp, lp, KV_LORA), kr_p.reshape(1, bp, lp, QK_ROPE),
              h_p[None], conv_p[None])

    o_attn_s = _uv_proj(o_lat.reshape(bs, MLA_HEADS, ls, KV_LORA), p["w_uv"], tb)
    o_ssm_s, h_s, conv_s = _ssd(xbc_s, z_s, dtr_s, state_conv[0], state_ssm[0], p, bs, ls, ls,
                                _largest_divisor(bs, 16), 1)
    y_s = _out_mlp(xs2, o_attn_s, o_ssm_s, p, _largest_divisor(bs * ls, 512))
    outs_s = (y_s.reshape(bs, ls, D_MODEL), c_s.reshape(1, bs, ls, KV_LORA), kr_s.reshape(1, bs, ls, QK_ROPE),
              h_s[None], conv_s[None])

    return (outs_p[0], outs_s[0], outs_p[1], outs_p[2], outs_p[3], outs_p[4],
            outs_s[1], outs_s[2], outs_s[3], outs_s[4])
```

```python
import functools

import jax
import jax.numpy as jnp
import numpy as np
from jax import lax
from jax.experimental import pallas as pl
from jax.experimental.pallas import tpu as pltpu

D_MODEL = 1024
D_ATTN = 512
D_SSM = 512
V_HEAD = 64
MLA_HEADS = 8
QK_NOPE = 64
QK_ROPE = 32
KV_LORA = 256
Q_LORA = 384
ROPE_THETA = 10000.0
SSM_HEADDIM = 64
SSM_HEADS = 8
SSM_GROUPS = 2
D_STATE = 128
CONV_K = 4
CHUNK = 256
CONV_DIM = D_SSM + 2 * SSM_GROUPS * D_STATE
D_FF = 4096
EPS = 1e-6

LANES = 128
SUBLANES = 8
QK_AHEAD = 2
SPLITS_AHEAD = 4
QK_PAD = KV_LORA + LANES
VMEM_LIMIT = 56 * 1024 * 1024
VMEM_LIMIT_FUSED = 60 * 1024 * 1024
SCALE_LOG2 = (QK_NOPE + QK_ROPE) ** -0.5 * 1.4426950408889634

_C_QLAT = 0
_C_KV = _C_QLAT + Q_LORA
_C_KPE = _C_KV + KV_LORA
_C_KPES = _C_KPE + LANES
_C_Z = _C_KPES + LANES
_C_XBC = _C_Z + D_SSM
_C_DT = _C_XBC + CONV_DIM
_N_IN_PAD = _C_DT + LANES
_Q_NOPE = 0
_Q_ROPE = MLA_HEADS * LANES
_Q_ROPES = _Q_ROPE + MLA_HEADS * QK_ROPE
_N_UQ_PAD = _Q_ROPES + MLA_HEADS * QK_ROPE

_BF = jnp.bfloat16
_F32 = jnp.float32


def _dot(a, b):
    return jnp.dot(a, b, preferred_element_type=_F32)


def _dot_nt(a, b):
    return lax.dot_general(a, b, (((1,), (1,)), ((), ())), preferred_element_type=_F32)


def _dot_tn(a, b):
    return lax.dot_general(a, b, (((0,), (0,)), ((), ())), preferred_element_type=_F32)


def _split3(a):
    hi = a.astype(_BF)
    r1 = a - hi.astype(_F32)
    mid = r1.astype(_BF)
    lo = (r1 - mid.astype(_F32)).astype(_BF)
    return jnp.concatenate([hi, mid, lo], axis=1)


def _rms(x, g):
    return x * lax.rsqrt(jnp.mean(x * x, axis=-1, keepdims=True) + EPS) * g


def _silu(x):
    return x * (1.0 / (1.0 + jnp.exp(-x)))


def _sublane_all(op, x):
    for shift in (4, 2, 1):
        x = op(x, pltpu.roll(x, shift, 0))
    return x


def _const_spec(shape):
    nd = len(shape)
    return pl.BlockSpec(shape, lambda *_: (0,) * nd)


def _rope_tables(l, past_len):
    pos = past_len + np.arange(l, dtype=np.float64)
    inv = ROPE_THETA ** (-(np.arange(0, QK_ROPE, 2, dtype=np.float64) / QK_ROPE))
    ang = pos[:, None] * inv[None, :]
    cos, sin = np.cos(ang).astype(np.float32), np.sin(ang).astype(np.float32)
    return np.concatenate([cos, cos], axis=1), np.concatenate([-sin, sin], axis=1)


_XP_OFF = 8


def _conv_silu(win_ref, x_new, cw_ref, cb_ref, q):
    win_ref[_XP_OFF:_XP_OFF + q, :] = x_new
    conv = cb_ref[...]
    for k in range(CONV_K):
        conv = conv + win_ref[_XP_OFF - (CONV_K - 1) + k:_XP_OFF - (CONV_K - 1) + k + q, :] * cw_ref[k:k + 1, :]
    tail = win_ref[_XP_OFF + q - (CONV_K - 1):_XP_OFF + q, :]
    win_ref[_XP_OFF - (CONV_K - 1):_XP_OFF, :] = tail
    return _silu(conv), tail


def _in_proj_kernel(x_ref, g_ref, w_ref, qn_ref, kvn_ref, wuq_ref, wuk_ref, cosq_ref, sinq_ref, cosk_ref, sink_ref,
                    q_ref, kv_ref, c_ref, kr_ref, z_ref, xbc_ref, dt_ref, *, tb, tl):
    tm = tb * tl

    def rows(tab_ref):
        t = tab_ref[...]
        if tb == 1:
            return t
        return jnp.broadcast_to(t[None], (tb, tl, t.shape[-1])).reshape(tm, t.shape[-1])

    h = _rms(x_ref[...], g_ref[...]).astype(_BF)
    proj = _dot(h, w_ref[...])
    z_ref[...] = proj[:, _C_Z:_C_XBC]
    xbc_ref[...] = proj[:, _C_XBC:_C_DT]
    dt_ref[...] = proj[:, _C_DT:_N_IN_PAD]

    c = _rms(proj[:, _C_KV:_C_KPE], kvn_ref[...])
    kr = proj[:, _C_KPE:_C_KPES] * rows(cosk_ref) + proj[:, _C_KPES:_C_Z] * rows(sink_ref)
    c_ref[...] = c
    kr_ref[...] = kr[:, :QK_ROPE]
    kv_ref[:, :KV_LORA] = c.astype(_BF)
    kv_ref[:, KV_LORA:] = kr.astype(_BF)

    qn = _rms(proj[:, _C_QLAT:_C_KV], qn_ref[...]).astype(_BF)
    qq = _dot(qn, wuq_ref[...])
    q_pe = qq[:, _Q_ROPE:_Q_ROPES] * rows(cosq_ref) + qq[:, _Q_ROPES:_N_UQ_PAD] * rows(sinq_ref)
    lane = lax.broadcasted_iota(jnp.int32, (tm, LANES), 1)
    heads_per_group = LANES // QK_ROPE
    for hd in range(MLA_HEADS):
        q_abs = _dot(qq[:, hd * LANES:(hd + 1) * LANES].astype(_BF), wuk_ref[hd]) * SCALE_LOG2
        grp = q_pe[:, (hd // heads_per_group) * LANES:(hd // heads_per_group + 1) * LANES]
        shift = (LANES - QK_ROPE * (hd % heads_per_group)) % LANES
        if shift:
            grp = pltpu.roll(grp, shift, 1)
        pe = jnp.where(lane < QK_ROPE, grp * SCALE_LOG2, 0.0)
        q_ref[:, hd, :, :KV_LORA] = q_abs.astype(_BF).reshape(tb, tl, KV_LORA)
        q_ref[:, hd, :, KV_LORA:] = pe.astype(_BF).reshape(tb, tl, LANES)


def _in_proj(x2d, b, l, past_len, p, tb, tl):
    t = b * l
    tm = tb * tl
    n_l = l // tl
    cos32, sin32 = _rope_tables(l, past_len)
    padk = np.zeros((l, LANES - QK_ROPE), np.float32)
    cosk = np.concatenate([cos32, padk], axis=1)
    sink = np.concatenate([sin32, padk], axis=1)
    cosq = np.tile(cos32, (1, MLA_HEADS))
    sinq = np.tile(sin32, (1, MLA_HEADS))

    row_spec = lambda w: pl.BlockSpec((tm, w), lambda i: (i, 0))
    tab_spec = lambda w: pl.BlockSpec((tl, w), lambda i: (i % n_l, 0))
    out_shape = (
        jax.ShapeDtypeStruct((b, MLA_HEADS, l, QK_PAD), _BF),
        jax.ShapeDtypeStruct((t, QK_PAD), _BF),
        jax.ShapeDtypeStruct((t, KV_LORA), _F32),
        jax.ShapeDtypeStruct((t, QK_ROPE), _F32),
        jax.ShapeDtypeStruct((t, D_SSM), _F32),
        jax.ShapeDtypeStruct((t, CONV_DIM), _F32),
        jax.ShapeDtypeStruct((t, LANES), _F32),
    )
    out_specs = (
        pl.BlockSpec((tb, MLA_HEADS, tl, QK_PAD), lambda i: (i // n_l, 0, i % n_l, 0)),
        row_spec(QK_PAD), row_spec(KV_LORA), row_spec(QK_ROPE), row_spec(D_SSM), row_spec(CONV_DIM), row_spec(LANES),
    )
    in_specs = [
        row_spec(D_MODEL), _const_spec((1, D_MODEL)), _const_spec((D_MODEL, _N_IN_PAD)),
        _const_spec((1, Q_LORA)), _const_spec((1, KV_LORA)), _const_spec((Q_LORA, _N_UQ_PAD)),
        _const_spec((MLA_HEADS, LANES, KV_LORA)),
        tab_spec(MLA_HEADS * QK_ROPE), tab_spec(MLA_HEADS * QK_ROPE), tab_spec(LANES), tab_spec(LANES),
    ]
    return pl.pallas_call(
        functools.partial(_in_proj_kernel, tb=tb, tl=tl),
        grid=(t // tm,), in_specs=in_specs, out_specs=out_specs, out_shape=out_shape,
        compiler_params=pltpu.CompilerParams(dimension_semantics=("arbitrary",), vmem_limit_bytes=VMEM_LIMIT),
        name="in_proj",
    )(x2d, p["g_attn"], p["w_in"], p["q_norm"], p["kv_norm"], p["w_uq"], p["w_uk"], cosq, sinq, cosk, sink)


def _in_proj_prompt_kernel(x_ref, g_ref, w_ref, qn_ref, kvn_ref, wuqt_ref, wk_ref, wuvt_ref, cost_ref, sint_ref,
                           cosk_ref, sink_ref, qt_ref, kh_ref, vt_ref, c_ref, kr_ref, z_ref, xbc_ref, dt_ref):
    h = _rms(x_ref[...], g_ref[...]).astype(_BF)
    proj = _dot(h, w_ref[...])
    z_ref[...] = proj[:, _C_Z:_C_XBC]
    xbc_ref[...] = proj[:, _C_XBC:_C_DT]
    dt_ref[...] = proj[:, _C_DT:_N_IN_PAD]

    c = _rms(proj[:, _C_KV:_C_KPE], kvn_ref[...])
    kr = proj[:, _C_KPE:_C_KPES] * cosk_ref[...] + proj[:, _C_KPES:_C_Z] * sink_ref[...]
    c_ref[...] = c
    kr_ref[...] = kr[:, QK_NOPE:QK_NOPE + QK_ROPE]
    c_bf = c.astype(_BF)
    kn = _dot(c_bf, wk_ref[...])
    for hd in range(MLA_HEADS):
        kh_ref[:, hd * LANES:(hd + 1) * LANES] = (kn[:, hd * LANES:(hd + 1) * LANES] + kr).astype(_BF)
    vt_ref[0] = _dot_nt(wuvt_ref[...], c_bf).astype(_BF)

    qn = _rms(proj[:, _C_QLAT:_C_KV], qn_ref[...]).astype(_BF)
    qqt = _dot_nt(wuqt_ref[...], qn)
    half = MLA_HEADS * LANES
    for hd in range(MLA_HEADS):
        qt = (qqt[hd * LANES:(hd + 1) * LANES] * cost_ref[...]
              + qqt[half + hd * LANES:half + (hd + 1) * LANES] * sint_ref[...])
        qt_ref[0, hd] = qt.astype(_BF)


def _in_proj_prompt(x2d, b, l, p, tl):
    t = b * l
    n_l = l // tl
    cos32, sin32 = _rope_tables(l, 0)
    lead = np.zeros((l, QK_NOPE), np.float32)
    trail = np.zeros((l, LANES - QK_NOPE - QK_ROPE), np.float32)
    cosk = np.concatenate([lead, cos32, trail], axis=1)
    sink = np.concatenate([lead, sin32, trail], axis=1)
    cost = np.ascontiguousarray((np.concatenate([lead + 1.0, cos32, trail], axis=1) * np.float32(SCALE_LOG2)).T)
    sint = np.ascontiguousarray((sink * np.float32(SCALE_LOG2)).T)

    row_spec = lambda w: pl.BlockSpec((tl, w), lambda i: (i, 0))
    out_shape = (
        jax.ShapeDtypeStruct((b, MLA_HEADS, LANES, l), _BF),
        jax.ShapeDtypeStruct((t, MLA_HEADS * LANES), _BF),
        jax.ShapeDtypeStruct((b, D_ATTN, l), _BF),
        jax.ShapeDtypeStruct((t, KV_LORA), _F32),
        jax.ShapeDtypeStruct((t, QK_ROPE), _F32),
        jax.ShapeDtypeStruct((t, D_SSM), _F32),
        jax.ShapeDtypeStruct((t, CONV_DIM), _F32),
        jax.ShapeDtypeStruct((t, LANES), _F32),
    )
    out_specs = (
        pl.BlockSpec((1, MLA_HEADS, LANES, tl), lambda i: (i // n_l, 0, 0, i % n_l)),
        row_spec(MLA_HEADS * LANES),
        pl.BlockSpec((1, D_ATTN, tl), lambda i: (i // n_l, 0, i % n_l)),
        row_spec(KV_LORA), row_spec(QK_ROPE), row_spec(D_SSM), row_spec(CONV_DIM), row_spec(LANES),
    )
    in_specs = [
        row_spec(D_MODEL), _const_spec((1, D_MODEL)), _const_spec((D_MODEL, _N_IN_PAD)),
        _const_spec((1, Q_LORA)), _const_spec((1, KV_LORA)), _const_spec((2 * MLA_HEADS * LANES, Q_LORA)),
        _const_spec((KV_LORA, MLA_HEADS * LANES)), _const_spec((D_ATTN, KV_LORA)),
        pl.BlockSpec((LANES, tl), lambda i: (0, i % n_l)), pl.BlockSpec((LANES, tl), lambda i: (0, i % n_l)),
        pl.BlockSpec((tl, LANES), lambda i: (i % n_l, 0)), pl.BlockSpec((tl, LANES), lambda i: (i % n_l, 0)),
    ]
    return pl.pallas_call(
        _in_proj_prompt_kernel,
        grid=(t // tl,), in_specs=in_specs, out_specs=out_specs, out_shape=out_shape,
        compiler_params=pltpu.CompilerParams(dimension_semantics=("arbitrary",), vmem_limit_bytes=VMEM_LIMIT),
        name="in_proj_prompt",
    )(x2d, p["g_attn"], p["w_in_prompt"], p["q_norm"], p["kv_norm"], p["w_uq_t"], p["w_k"], p["w_uv_t"],
      cost, sint, cosk, sink)


def _attn_prompt_kernel(it_ref, jt_ref, qt_ref, kh_ref, vt_ref, o_ref, m_ref, l_ref, acc_ref, *, tq):
    s_idx = pl.program_id(1)
    i = it_ref[s_idx]
    j = jt_ref[s_idx]

    @pl.when(j == 0)
    def _():
        m_ref[...] = jnp.full(m_ref.shape, -jnp.inf, _F32)
        l_ref[...] = jnp.zeros(l_ref.shape, _F32)
        acc_ref[...] = jnp.zeros(acc_ref.shape, _F32)

    def step(masked):
        if masked:
            krow = lax.broadcasted_iota(jnp.int32, (tq, tq), 0)
            qcol = lax.broadcasted_iota(jnp.int32, (tq, tq), 1)
            keep = krow <= qcol

        def scores(hd):
            st = _dot(kh_ref[:, hd * LANES:(hd + 1) * LANES], qt_ref[0, hd])
            if masked:
                st = jnp.where(keep, st, -jnp.inf)
            return st.reshape(tq // SUBLANES, SUBLANES, tq)

        ahead = {hd: scores(hd) for hd in range(QK_AHEAD)}
        for hd in range(MLA_HEADS):
            if hd + QK_AHEAD < MLA_HEADS:
                ahead[hd + QK_AHEAD] = scores(hd + QK_AHEAD)
            st = ahead.pop(hd)
            m_prev = m_ref[hd]
            m_new = jnp.maximum(m_prev, _sublane_all(jnp.maximum, jnp.max(st, axis=0)))
            alpha = jnp.exp2(m_prev - m_new)
            pr = jnp.exp2(st - m_new[None])
            l_ref[hd] = alpha * l_ref[hd] + jnp.sum(pr, axis=0)
            rows = slice(hd * V_HEAD, (hd + 1) * V_HEAD)
            acc = acc_ref[rows, :].reshape(V_HEAD // SUBLANES, SUBLANES, tq) * alpha[None]
            acc_ref[rows, :] = acc.reshape(V_HEAD, tq) + _dot(vt_ref[0, rows, :], pr.reshape(tq, tq).astype(_BF))
            m_ref[hd] = m_new

    @pl.when(j < i)
    def _():
        step(False)

    @pl.when(j == i)
    def _():
        step(True)
        for hd in range(MLA_HEADS):
            rows = slice(hd * V_HEAD, (hd + 1) * V_HEAD)
            den = _sublane_all(jnp.add, l_ref[hd])
            acc = acc_ref[rows, :].reshape(V_HEAD // SUBLANES, SUBLANES, tq) / den[None]
            acc_ref[rows, :] = acc.reshape(V_HEAD, tq)
        o_ref[...] = acc_ref[...].T.astype(o_ref.dtype)


def _attn_prompt(qt, kh, vt, b, l, tq):
    nq = l // tq
    pairs = [(i, j) for i in range(nq) for j in range(i + 1)]
    it = jnp.asarray([pr[0] for pr in pairs], jnp.int32)
    jt = jnp.asarray([pr[1] for pr in pairs], jnp.int32)
    grid_spec = pltpu.PrefetchScalarGridSpec(
        num_scalar_prefetch=2,
        grid=(b, len(pairs)),
        in_specs=[
            pl.BlockSpec((1, MLA_HEADS, LANES, tq), lambda bi, s, it, jt: (bi, 0, 0, it[s])),
            pl.BlockSpec((tq, MLA_HEADS * LANES), lambda bi, s, it, jt: (bi * nq + jt[s], 0)),
            pl.BlockSpec((1, D_ATTN, tq), lambda bi, s, it, jt: (bi, 0, jt[s])),
        ],
        out_specs=pl.BlockSpec((tq, D_ATTN), lambda bi, s, it, jt: (bi * nq + it[s], 0)),
        scratch_shapes=[pltpu.VMEM((MLA_HEADS, SUBLANES, tq), _F32), pltpu.VMEM((MLA_HEADS, SUBLANES, tq), _F32),
                        pltpu.VMEM((D_ATTN, tq), _F32)],
    )
    return pl.pallas_call(
        functools.partial(_attn_prompt_kernel, tq=tq),
        grid_spec=grid_spec,
        out_shape=jax.ShapeDtypeStruct((b * l, D_ATTN), _BF),
        compiler_params=pltpu.CompilerParams(dimension_semantics=("arbitrary", "arbitrary"),
                                             vmem_limit_bytes=VMEM_LIMIT),
        name="attn_prompt",
    )(it, jt, qt, kh, vt)


def _page_copies(pt_ref, cache_c_ref, cache_krt_ref, cbuf_ref, krbuf_ref, sem_ref, item, slot, pg, page):
    pid = pt_ref[item, pg]
    dst = pl.ds(pl.multiple_of(pg * page, page), page)
    return (pltpu.make_async_copy(cache_c_ref.at[pid], cbuf_ref.at[slot, dst, :], sem_ref.at[slot, 0]),
            pltpu.make_async_copy(cache_krt_ref.at[pid], krbuf_ref.at[slot, :, dst], sem_ref.at[slot, 1]))


def _wait_item(cbuf_ref, krbuf_ref, sem_ref, slot):
    pltpu.make_async_copy(cbuf_ref.at[slot], cbuf_ref.at[slot], sem_ref.at[slot, 0]).wait()
    pltpu.make_async_copy(krbuf_ref.at[slot], krbuf_ref.at[slot], sem_ref.at[slot, 1]).wait()


def _attend_item(q, kvn, slot, cbuf_ref, krbuf_ref, kc_ref, *, n_pages, page, l_new, n_split):
    m_rows = MLA_HEADS * l_new
    q_abs = q[:, :KV_LORA]
    q_pe = q[:, KV_LORA:KV_LORA + QK_ROPE]
    ks = (n_pages // n_split) * page

    def scores(sp):
        rows = pl.ds(sp * ks, ks)
        kc_ref[rows, :] = cbuf_ref[slot, rows, :].astype(_BF)
        return _dot_nt(q_abs, kc_ref[rows, :]) + _dot(q_pe, krbuf_ref[slot, :, rows].astype(_BF))

    def partial(s, values):
        m = jnp.max(s, axis=1, keepdims=True)
        pr = jnp.exp2(s - m)
        return m, jnp.sum(pr, axis=1, keepdims=True), _dot(pr.astype(_BF), values)

    def merge(run, new):
        m = jnp.maximum(run[0], new[0])
        a, c = jnp.exp2(run[0] - m), jnp.exp2(new[0] - m)
        return m, a * run[1] + c * new[1], a * run[2] + c * new[2]

    ahead = {sp: scores(sp) for sp in range(min(SPLITS_AHEAD, n_split))}
    sn = _dot_nt(q, kvn)
    qpos = lax.broadcasted_iota(jnp.int32, (MLA_HEADS, l_new, l_new), 1).reshape(m_rows, l_new)
    kpos = lax.broadcasted_iota(jnp.int32, (m_rows, l_new), 1)
    run = partial(jnp.where(kpos <= qpos, sn, -jnp.inf), kvn[:, :KV_LORA])
    for sp in range(n_split):
        if sp + SPLITS_AHEAD < n_split:
            ahead[sp + SPLITS_AHEAD] = scores(sp + SPLITS_AHEAD)
        run = merge(run, partial(ahead.pop(sp), kc_ref[pl.ds(sp * ks, ks), :]))
    return run[2] / run[1]


def _uv_proj_kernel(o_ref, wuv_ref, out_ref, *, tb, l_new):
    for hd in range(MLA_HEADS):
        o = o_ref[:, hd].reshape(tb * l_new, KV_LORA)
        out_ref[:, hd * V_HEAD:(hd + 1) * V_HEAD] = _dot(o, wuv_ref[hd]).astype(out_ref.dtype)


def _uv_proj(o_lat, wuv, tb):
    b, _, l_new, _ = o_lat.shape
    return pl.pallas_call(
        functools.partial(_uv_proj_kernel, tb=tb, l_new=l_new),
        grid=(b // tb,),
        in_specs=[pl.BlockSpec((tb, MLA_HEADS, l_new, KV_LORA), lambda i: (i, 0, 0, 0)),
                  _const_spec((MLA_HEADS, KV_LORA, V_HEAD))],
        out_specs=pl.BlockSpec((tb * l_new, D_ATTN), lambda i: (i, 0)),
        out_shape=jax.ShapeDtypeStruct((b * l_new, D_ATTN), _BF),
        compiler_params=pltpu.CompilerParams(dimension_semantics=("arbitrary",), vmem_limit_bytes=VMEM_LIMIT),
        name="uv_proj",
    )(o_lat, wuv)


def _ssd_kernel(xbc_ref, z_ref, dtr_ref, cbuf_ref, h0_ref, cw_ref, cb_ref, dtb_ref, alog_ref, dskip_ref, nrm_ref,
                sel_ref, ex_ref, exw_ref, o_ref, h_ref, cnew_ref, xp_ref, y_ref, *, n_seq, n_chunk, q):
    assert n_seq == 1 or n_chunk == 1
    g_items = n_seq * n_chunk
    c_idx = pl.program_id(1)
    last = pl.num_programs(1) - 1
    heads_per_group = SSM_HEADS // SSM_GROUPS
    rows = g_items * q
    row = lax.broadcasted_iota(jnp.int32, (q, q), 0)
    col = lax.broadcasted_iota(jnp.int32, (q, q), 1)
    causal = row >= col
    n_bc = SSM_GROUPS * D_STATE

    def conv_item(g):
        sq, ck = divmod(g, n_chunk)
        rs = slice(g * q, (g + 1) * q)

        if ck == 0:
            @pl.when(c_idx == 0)
            def _():
                h_ref[sq] = h0_ref[sq]
                xp_ref[sq, _XP_OFF - (CONV_K - 1):_XP_OFF, :] = cbuf_ref[sq]

        act, tail = _conv_silu(xp_ref.at[sq], xbc_ref[rs, :], cw_ref, cb_ref, q)

        if ck == n_chunk - 1:
            @pl.when(c_idx == last)
            def _():
                cnew_ref[sq] = tail

        return act[:, :D_SSM], act[:, D_SSM:D_SSM + n_bc].astype(_BF), act[:, D_SSM + n_bc:].astype(_BF)

    convs = {g: conv_item(g) for g in range(g_items)} if n_seq == 1 else {}

    if g_items == 1:
        tril = jnp.where(causal, 1.0, 0.0).astype(_BF)
    else:
        r_all = lax.broadcasted_iota(jnp.int32, (rows, rows), 0)
        c_all = lax.broadcasted_iota(jnp.int32, (rows, rows), 1)
        same_item = (r_all // q) == (c_all // q)
        tril = jnp.where(r_all >= c_all, jnp.where(same_item, 1.0, 0.0), 0.0).astype(_BF)
    a_neg = -jnp.exp(alog_ref[...])
    dtv = dtr_ref[...] + dtb_ref[...]
    dt = jnp.maximum(dtv, 0.0) + jnp.log1p(jnp.exp(-jnp.abs(dtv)))
    cs = _dot(tril, _split3(dt * a_neg))
    acs = cs[:, :LANES] + cs[:, LANES:2 * LANES] + cs[:, 2 * LANES:]
    acs3 = _split3(acs)
    acs_t = _dot_nt(sel_ref[...], acs3)
    dt_x = _dot(_split3(dt), ex_ref[...])
    acs_x = _dot(acs3, ex_ref[...])
    acs_w = _dot(acs3, exw_ref[...])
    e_acs = jnp.exp(acs_x)

    for g in range(g_items):
        sq = g // n_chunk
        rs = slice(g * q, (g + 1) * q)
        xs, bmat, cmat = convs[g] if g in convs else conv_item(g)
        acs_last_x = acs_x[(g + 1) * q - 1:(g + 1) * q, :]
        x_dt = xs * dt_x[rs]
        xw = (x_dt * jnp.exp(acs_last_x - acs_x[rs])).astype(_BF)
        x_dt = x_dt.astype(_BF)
        for grp in range(SSM_GROUPS):
            b_g = bmat[:, grp * D_STATE:(grp + 1) * D_STATE]
            c_g = cmat[:, grp * D_STATE:(grp + 1) * D_STATE]
            cb = _dot_nt(c_g, b_g)
            for hd in range(grp * heads_per_group, (grp + 1) * heads_per_group):
                cols = slice(hd * SSM_HEADDIM, (hd + 1) * SSM_HEADDIM)
                acs_h = acs_w[rs, hd * LANES:(hd + 1) * LANES]
                acs_h = acs_h[:, :q] if q <= LANES else jnp.concatenate([acs_h] * (q // LANES), axis=1)
                lmat = jnp.exp(jnp.where(causal, acs_h - acs_t[hd:hd + 1, rs], -jnp.inf))
                h_prev = h_ref[sq, hd]
                y = _dot((cb * lmat).astype(_BF), x_dt[:, cols])
                y = y + _dot_nt(c_g, h_prev.astype(_BF)) * e_acs[rs, cols]
                decay = jnp.exp(acs[(g + 1) * q - 1:(g + 1) * q, hd:hd + 1])
                h_ref[sq, hd] = decay * h_prev + _dot_tn(xw[:, cols], b_g)
                y_ref[:, cols] = y
        yv = y_ref[...] + dskip_ref[...] * xs
        gated = yv * _silu(z_ref[g * q:(g + 1) * q, :])
        gw = D_SSM // SSM_GROUPS
        parts = []
        for grp in range(SSM_GROUPS):
            gg = gated[:, grp * gw:(grp + 1) * gw]
            parts.append(gg * lax.rsqrt(jnp.mean(gg * gg, axis=-1, keepdims=True) + EPS))
        o_ref[g * q:(g + 1) * q, :] = (jnp.concatenate(parts, axis=1) * nrm_ref[...]).astype(o_ref.dtype)


def _ssd(xbc, z, dtr, conv_buf, h0, p, b, l, q, n_seq, n_chunk):
    nc = l // (q * n_chunk)
    rows = n_seq * n_chunk * q
    row_spec = lambda w: pl.BlockSpec((rows, w), lambda bi, c: (bi * nc + c, 0))
    state_spec = pl.BlockSpec((n_seq, SSM_HEADS, SSM_HEADDIM, D_STATE), lambda bi, c: (bi, 0, 0, 0))
    hist_spec = pl.BlockSpec((n_seq, CONV_K - 1, CONV_DIM), lambda bi, c: (bi, 0, 0))
    consts = [p["dt_bias"], p["a_log"], p["d_skip"], p["ssm_norm"], p["sel"], p["head_to_cols"], p["head_to_lanes"]]
    const_specs = [_const_spec((1, LANES)), _const_spec((1, LANES)), _const_spec((1, D_SSM)), _const_spec((1, D_SSM)),
                   _const_spec((SSM_HEADS, 3 * LANES)), _const_spec((3 * LANES, D_SSM)),
                   _const_spec((3 * LANES, SSM_HEADS * LANES))]
    in_specs = [row_spec(CONV_DIM), row_spec(D_SSM), row_spec(LANES), hist_spec, state_spec,
                _const_spec((CONV_K, CONV_DIM)), _const_spec((1, CONV_DIM)), *const_specs]
    out_shape = (jax.ShapeDtypeStruct((b * l, D_SSM), _BF),
                 jax.ShapeDtypeStruct((b, SSM_HEADS, SSM_HEADDIM, D_STATE), _F32),
                 jax.ShapeDtypeStruct((b, CONV_K - 1, CONV_DIM), _F32))
    return pl.pallas_call(
        functools.partial(_ssd_kernel, n_seq=n_seq, n_chunk=n_chunk, q=q),
        grid=(b // n_seq, nc), in_specs=in_specs, out_specs=(row_spec(D_SSM), state_spec, hist_spec),
        out_shape=out_shape,
        scratch_shapes=[pltpu.VMEM((n_seq, _XP_OFF + q, CONV_DIM), _F32), pltpu.VMEM((q, D_SSM), _F32)],
        compiler_params=pltpu.CompilerParams(dimension_semantics=("arbitrary", "arbitrary"),
                                             vmem_limit_bytes=VMEM_LIMIT),
        name="ssd",
    )(xbc, z, dtr, conv_buf, h0, p["conv_w"], p["conv_b"], *consts)


def _mlp_chunk(h2, acc, wu_ref, wd_ref, f0, f_chunk):
    u = jnp.maximum(_dot(h2, wu_ref[:, f0:f0 + f_chunk]), 0.0)
    return acc + _dot((u * u).astype(_BF), wd_ref[f0:f0 + f_chunk, :])


def _out_mlp_kernel(x_ref, oa_ref, os_ref, wo_ref, gm_ref, wu_ref, wd_ref, gf_ref, y_ref, *, f_chunk):
    mix = jnp.concatenate([oa_ref[...], os_ref[...]], axis=1)
    x1 = x_ref[...] + _dot(mix, wo_ref[...])
    h2 = _rms(x1, gm_ref[...]).astype(_BF)
    acc = x1
    for f0 in range(0, D_FF, f_chunk):
        acc = _mlp_chunk(h2, acc, wu_ref, wd_ref, f0, f_chunk)
    y_ref[...] = _rms(acc, gf_ref[...])


def _mlp_attn_kernel(pt_ref, x_ref, oa_ref, os_ref, wo_ref, gm_ref, wu_ref, wd_ref, gf_ref, q_ref, kvn_ref,
                     cache_c_ref, cache_krt_ref, y_ref, o_ref, cbuf_ref, krbuf_ref, kc_ref, sem_ref, *,
                     f_chunk, ipb, n_items, n_pages, page, l_new, n_split):
    i = pl.program_id(0)
    copies = functools.partial(_page_copies, pt_ref, cache_c_ref, cache_krt_ref, cbuf_ref, krbuf_ref, sem_ref)

    @pl.when(i == 0)
    def _():
        def body(pg, carry):
            for cp in copies(0, 0, pg, page):
                cp.start()
            return carry
        lax.fori_loop(0, n_pages, body, 0)

    mix = jnp.concatenate([oa_ref[...], os_ref[...]], axis=1)
    x1 = x_ref[...] + _dot(mix, wo_ref[...])
    h2 = _rms(x1, gm_ref[...]).astype(_BF)
    acc = x1
    f_starts = list(range(0, D_FF, f_chunk))
    per_item = -(-len(f_starts) // ipb)
    for k in range(ipb):
        item = i * ipb + k
        slot = k % 2 if ipb % 2 == 0 else item % 2
        _wait_item(cbuf_ref, krbuf_ref, sem_ref, slot)
        nxt = jnp.minimum(item + 1, n_items - 1)
        for pg in range(n_pages):
            for cp in copies(nxt, 1 - slot, pg, page):
                cp.start()
        o = _attend_item(q_ref[k], kvn_ref[k], slot, cbuf_ref, krbuf_ref, kc_ref, n_pages=n_pages, page=page,
                         l_new=l_new, n_split=n_split)
        o_ref[k] = o.astype(o_ref.dtype)
        for f0 in f_starts[k * per_item:(k + 1) * per_item]:
            acc = _mlp_chunk(h2, acc, wu_ref, wd_ref, f0, f_chunk)
    for f0 in f_starts[ipb * per_item:]:
        acc = _mlp_chunk(h2, acc, wu_ref, wd_ref, f0, f_chunk)
    y_ref[...] = _rms(acc, gf_ref[...])

    @pl.when(i == pl.num_programs(0) - 1)
    def _():
        _wait_item(cbuf_ref, krbuf_ref, sem_ref, n_items % 2)


def _mlp_attn(x2d, oa, osm, q, kvn, cache_c, cache_krt, page_table, p, tm):
    t = x2d.shape[0]
    n_steps = t // tm
    b, m_rows, _ = q.shape
    assert b % n_steps == 0, "sample batch must split evenly over the MLP token tiles"
    ipb = b // n_steps
    l_new = m_rows // MLA_HEADS
    n_pages = page_table.shape[1]
    page = cache_c.shape[1]
    keys = n_pages * page
    row_spec = lambda w: pl.BlockSpec((tm, w), lambda i, pt: (i, 0))
    once = lambda shape: pl.BlockSpec(shape, lambda i, pt: (0,) * len(shape), pipeline_mode=pl.Buffered(1))
    item_spec = lambda r, w: pl.BlockSpec((ipb, r, w), lambda i, pt: (i, 0, 0))
    grid_spec = pltpu.PrefetchScalarGridSpec(
        num_scalar_prefetch=1,
        grid=(n_steps,),
        in_specs=[row_spec(D_MODEL), row_spec(D_ATTN), row_spec(D_SSM), once((D_MODEL, D_MODEL)),
                  once((1, D_MODEL)), once((D_MODEL, D_FF)), once((D_FF, D_MODEL)), once((1, D_MODEL)),
                  item_spec(m_rows, QK_PAD), item_spec(l_new, QK_PAD),
                  pl.BlockSpec(memory_space=pl.ANY), pl.BlockSpec(memory_space=pl.ANY)],
        out_specs=(row_spec(D_MODEL), item_spec(m_rows, KV_LORA)),
        scratch_shapes=[pltpu.VMEM((2, keys, KV_LORA), _F32), pltpu.VMEM((2, QK_ROPE, keys), _F32),
                        pltpu.VMEM((keys, KV_LORA), _BF), pltpu.SemaphoreType.DMA((2, 2))],
    )
    return pl.pallas_call(
        functools.partial(_mlp_attn_kernel, f_chunk=1024, ipb=ipb, n_items=b, n_pages=n_pages, page=page,
                          l_new=l_new, n_split=_largest_divisor(n_pages, 8)),
        grid_spec=grid_spec,
        out_shape=(jax.ShapeDtypeStruct((t, D_MODEL), _F32), jax.ShapeDtypeStruct((b, m_rows, KV_LORA), _BF)),
        compiler_params=pltpu.CompilerParams(dimension_semantics=("arbitrary",), vmem_limit_bytes=VMEM_LIMIT_FUSED),
        name="mlp_attn",
    )(page_table, x2d, oa, osm, p["w_out"], p["g_mlp"], p["w_up"], p["w_down"], p["g_final"], q, kvn, cache_c,
      cache_krt)


def _out_mlp(x2d, oa, osm, p, tm):
    t = x2d.shape[0]
    row_spec = lambda w: pl.BlockSpec((tm, w), lambda i: (i, 0))
    once = lambda shape: pl.BlockSpec(shape, lambda i: (0,) * len(shape), pipeline_mode=pl.Buffered(1))
    return pl.pallas_call(
        functools.partial(_out_mlp_kernel, f_chunk=1024),
        grid=(t // tm,),
        in_specs=[row_spec(D_MODEL), row_spec(D_ATTN), row_spec(D_SSM), once((D_MODEL, D_MODEL)),
                  once((1, D_MODEL)), once((D_MODEL, D_FF)), once((D_FF, D_MODEL)), once((1, D_MODEL))],
        out_specs=row_spec(D_MODEL),
        out_shape=jax.ShapeDtypeStruct((t, D_MODEL), _F32),
        compiler_params=pltpu.CompilerParams(dimension_semantics=("arbitrary",), vmem_limit_bytes=VMEM_LIMIT),
        name="out_mlp",
    )(x2d, oa, osm, p["w_out"], p["g_mlp"], p["w_up"], p["w_down"], p["g_final"])


def _prep_params(norm_attn, w_in, q_norm, kv_norm, w_uq, w_uk, w_uv, conv_w, conv_b, dt_bias, a_log, d_skip,
                 ssm_norm, w_out, norm_mlp, w_up, w_down, norm_final):
    half = QK_ROPE // 2

    def swap(w):
        return jnp.concatenate([w[..., half:], w[..., :half]], axis=-1)

    def place(w, lead, n):
        return jnp.pad(w, ((0, 0), (lead, n - lead - w.shape[1])))

    o1 = Q_LORA
    o2 = o1 + KV_LORA
    o3 = o2 + QK_ROPE
    o5 = o3 + D_SSM + CONV_DIM
    w_kpe = w_in[:, o2:o3]

    def w_in_padded(lead):
        return jnp.concatenate([
            w_in[:, :o2], place(w_kpe, lead, LANES), place(swap(w_kpe), lead, LANES), w_in[:, o3:o5],
            place(w_in[:, o5:], 0, LANES)], axis=1).astype(_BF)

    uq = w_uq.reshape(Q_LORA, MLA_HEADS, QK_NOPE + QK_ROPE)
    uq_nope = uq[:, :, :QK_NOPE]
    uq_rope = uq[:, :, QK_NOPE:]
    w_uq_p = jnp.concatenate([
        jnp.pad(uq_nope, ((0, 0), (0, 0), (0, LANES - QK_NOPE))).reshape(Q_LORA, MLA_HEADS * LANES),
        uq_rope.reshape(Q_LORA, -1), swap(uq_rope).reshape(Q_LORA, -1)], axis=1).astype(_BF)
    w_uk_p = jnp.pad(w_uk, ((0, 0), (0, LANES - QK_NOPE), (0, 0))).astype(_BF)
    tail = LANES - QK_NOPE - QK_ROPE
    plain = jnp.pad(uq, ((0, 0), (0, 0), (0, tail)))
    swapped = jnp.pad(swap(uq_rope), ((0, 0), (0, 0), (QK_NOPE, tail)))
    w_uq_t = jnp.concatenate([plain.reshape(Q_LORA, -1), swapped.reshape(Q_LORA, -1)], axis=1).T.astype(_BF)
    w_k = jnp.pad(jnp.transpose(w_uk, (2, 0, 1)), ((0, 0), (0, 0), (0, LANES - QK_NOPE)))
    w_k = w_k.reshape(KV_LORA, MLA_HEADS * LANES).astype(_BF)
    w_uv_t = jnp.transpose(w_uv, (0, 2, 1)).reshape(D_ATTN, KV_LORA).astype(_BF)
    pad_heads = lambda v: jnp.pad(v.reshape(1, SSM_HEADS), ((0, 0), (0, LANES - SSM_HEADS)))
    return dict(
        g_attn=norm_attn.reshape(1, D_MODEL), w_in=w_in_padded(0), w_in_prompt=w_in_padded(QK_NOPE),
        q_norm=q_norm.reshape(1, Q_LORA), kv_norm=kv_norm.reshape(1, KV_LORA), w_uq=w_uq_p, w_uk=w_uk_p,
        w_uq_t=w_uq_t, w_k=w_k, w_uv_t=w_uv_t, w_uv=w_uv.astype(_BF),
        conv_w=conv_w, conv_b=conv_b.reshape(1, CONV_DIM), dt_bias=pad_heads(dt_bias), a_log=pad_heads(a_log),
        d_skip=jnp.repeat(d_skip, SSM_HEADDIM).reshape(1, D_SSM), ssm_norm=ssm_norm.reshape(1, D_SSM),
        sel=jnp.asarray(np.tile(np.eye(SSM_HEADS, LANES), (1, 3)), _BF),
        head_to_cols=jnp.asarray(np.tile(np.repeat(np.eye(LANES, SSM_HEADS), SSM_HEADDIM, axis=1), (3, 1)), _BF),
        head_to_lanes=jnp.asarray(np.tile(np.repeat(np.eye(LANES, SSM_HEADS), LANES, axis=1), (3, 1)), _BF),
        w_out=w_out.astype(_BF), g_mlp=norm_mlp.reshape(1, D_MODEL), w_up=w_up.astype(_BF),
        w_down=w_down.astype(_BF), g_final=norm_final.reshape(1, D_MODEL),
    )


def _largest_divisor(n, cap):
    d = min(n, cap)
    while n % d:
        d -= 1
    return d


def kernel(x_prompt, x_sample, cache_kv_latent, cache_k_rope, state_ssm, state_conv, page_table, norm_attn, w_in, q_norm, kv_norm, w_uq, w_uk, w_uv, conv_w, conv_b, dt_bias, a_log, d_skip, ssm_norm, w_out, norm_mlp, w_up, w_down, norm_final):
    assert norm_attn.shape[0] == 1, "single-layer model"
    p = _prep_params(norm_attn[0], w_in[0], q_norm[0], kv_norm[0], w_uq[0], w_uk[0], w_uv[0], conv_w[0], conv_b[0],
                     dt_bias[0], a_log[0], d_skip[0], ssm_norm[0], w_out[0], norm_mlp[0], w_up[0], w_down[0],
                     norm_final)
    bp, lp, _ = x_prompt.shape
    bs, ls, _ = x_sample.shape
    n_pages = page_table.shape[1]
    page = cache_kv_latent.shape[2]
    past_len = n_pages * page

    xp2 = x_prompt.reshape(bp * lp, D_MODEL)
    xs2 = x_sample.reshape(bs * ls, D_MODEL)
    qt, kh, vt, c_p, kr_p, z_p, xbc_p, dtr_p = _in_proj_prompt(xp2, bp, lp, p, _largest_divisor(lp, 512))
    tb = _largest_divisor(bs, 512 // ls)
    q_s, kv_s, c_s, kr_s, z_s, xbc_s, dtr_s = _in_proj(xs2, bs, ls, past_len, p, tb, ls)

    o_attn_p = _attn_prompt(qt, kh, vt, bp, lp, _largest_divisor(lp, 512))
    chunk = _largest_divisor(lp, CHUNK)
    o_ssm_p, h_p, conv_p = _ssd(xbc_p, z_p, dtr_p, jnp.zeros((bp, CONV_K - 1, CONV_DIM), _F32),
                                jnp.zeros((bp, SSM_HEADS, SSM_HEADDIM, D_STATE), _F32), p, bp, lp,
                                chunk, 1, _largest_divisor(lp // chunk, 2))

    cache_krt = jnp.swapaxes(cache_k_rope[0], 1, 2)
    y_p, o_lat = _mlp_attn(xp2, o_attn_p, o_ssm_p, q_s.reshape(bs, MLA_HEADS * ls, QK_PAD),
                           kv_s.reshape(bs, ls, QK_PAD), cache_kv_latent[0], cache_krt, page_table, p,
                           _largest_divisor(bp * lp, 512))
    outs_p = (y_p.reshape(bp, lp, D_MODEL), c_p.reshape(1, bp, lp, KV_LORA), kr_p.reshape(1, bp, lp, QK_ROPE),
              h_p[None], conv_p[None])

    o_attn_s = _uv_proj(o_lat.reshape(bs, MLA_HEADS, ls, KV_LORA), p["w_uv"], tb)
    o_ssm_s, h_s, conv_s = _ssd(xbc_s, z_s, dtr_s, state_conv[0], state_ssm[0], p, bs, ls, ls,
                                _largest_divisor(bs, 16), 1)
    y_s = _out_mlp(xs2, o_attn_s, o_ssm_s, p, _largest_divisor(bs * ls, 512))
    outs_s = (y_s.reshape(bs, ls, D_MODEL), c_s.reshape(1, bs, ls, KV_LORA), kr_s.reshape(1, bs, ls, QK_ROPE),
              h_s[None], conv_s[None])

    return (outs_p[0], outs_s[0], outs_p[1], outs_p[2], outs_p[3], outs_p[4],
            outs_s[1], outs_s[2], outs_s[3], outs_s[4])
```

```python
import functools

import jax
import jax.numpy as jnp
import numpy as np
from jax import lax
from jax.experimental import pallas as pl
from jax.experimental.pallas import tpu as pltpu

D_MODEL = 1024
D_ATTN = 512
D_SSM = 512
V_HEAD = 64
MLA_HEADS = 8
QK_NOPE = 64
QK_ROPE = 32
KV_LORA = 256
Q_LORA = 384
ROPE_THETA = 10000.0
SSM_HEADDIM = 64
SSM_HEADS = 8
SSM_GROUPS = 2
D_STATE = 128
CONV_K = 4
CHUNK = 256
CONV_DIM = D_SSM + 2 * SSM_GROUPS * D_STATE
D_FF = 4096
EPS = 1e-6

LANES = 128
SUBLANES = 8
QK_AHEAD = 2
GATHER_SLOTS = 3
SPLITS_AHEAD = 4
QK_PAD = KV_LORA + LANES
VMEM_LIMIT = 56 * 1024 * 1024
VMEM_LIMIT_FUSED = 60 * 1024 * 1024
SCALE_LOG2 = (QK_NOPE + QK_ROPE) ** -0.5 * 1.4426950408889634

_C_QLAT = 0
_C_KV = _C_QLAT + Q_LORA
_C_KPE = _C_KV + KV_LORA
_C_KPES = _C_KPE + LANES
_C_Z = _C_KPES + LANES
_C_XBC = _C_Z + D_SSM
_C_DT = _C_XBC + CONV_DIM
_N_IN_PAD = _C_DT + LANES
_Q_NOPE = 0
_Q_ROPE = MLA_HEADS * LANES
_Q_ROPES = _Q_ROPE + MLA_HEADS * QK_ROPE
_N_UQ_PAD = _Q_ROPES + MLA_HEADS * QK_ROPE

_BF = jnp.bfloat16
_F32 = jnp.float32


def _dot(a, b):
    return jnp.dot(a, b, preferred_element_type=_F32)


def _dot_nt(a, b):
    return lax.dot_general(a, b, (((1,), (1,)), ((), ())), preferred_element_type=_F32)


def _dot_tn(a, b):
    return lax.dot_general(a, b, (((0,), (0,)), ((), ())), preferred_element_type=_F32)


def _split3(a):
    hi = a.astype(_BF)
    r1 = a - hi.astype(_F32)
    mid = r1.astype(_BF)
    lo = (r1 - mid.astype(_F32)).astype(_BF)
    return jnp.concatenate([hi, mid, lo], axis=1)


def _rms(x, g):
    return x * lax.rsqrt(jnp.mean(x * x, axis=-1, keepdims=True) + EPS) * g


def _silu(x):
    return x * (1.0 / (1.0 + jnp.exp(-x)))


def _sublane_all(op, x):
    for shift in (4, 2, 1):
        x = op(x, pltpu.roll(x, shift, 0))
    return x


def _const_spec(shape):
    nd = len(shape)
    return pl.BlockSpec(shape, lambda *_: (0,) * nd)


def _rope_tables(l, past_len):
    pos = past_len + np.arange(l, dtype=np.float64)
    inv = ROPE_THETA ** (-(np.arange(0, QK_ROPE, 2, dtype=np.float64) / QK_ROPE))
    ang = pos[:, None] * inv[None, :]
    cos, sin = np.cos(ang).astype(np.float32), np.sin(ang).astype(np.float32)
    return np.concatenate([cos, cos], axis=1), np.concatenate([-sin, sin], axis=1)


_XP_OFF = 8


def _conv_silu(win_ref, x_new, cw_ref, cb_ref, q):
    win_ref[_XP_OFF:_XP_OFF + q, :] = x_new
    conv = cb_ref[...]
    for k in range(CONV_K):
        conv = conv + win_ref[_XP_OFF - (CONV_K - 1) + k:_XP_OFF - (CONV_K - 1) + k + q, :] * cw_ref[k:k + 1, :]
    tail = win_ref[_XP_OFF + q - (CONV_K - 1):_XP_OFF + q, :]
    win_ref[_XP_OFF - (CONV_K - 1):_XP_OFF, :] = tail
    return _silu(conv), tail


def _in_proj_kernel(x_ref, g_ref, w_ref, qn_ref, kvn_ref, wuq_ref, wuk_ref, cosq_ref, sinq_ref, cosk_ref, sink_ref,
                    q_ref, kv_ref, c_ref, kr_ref, z_ref, xbc_ref, dt_ref, *, tb, tl):
    tm = tb * tl

    def rows(tab_ref):
        t = tab_ref[...]
        if tb == 1:
            return t
        return jnp.broadcast_to(t[None], (tb, tl, t.shape[-1])).reshape(tm, t.shape[-1])

    h = _rms(x_ref[...], g_ref[...]).astype(_BF)
    proj = _dot(h, w_ref[...])
    z_ref[...] = proj[:, _C_Z:_C_XBC]
    xbc_ref[...] = proj[:, _C_XBC:_C_DT]
    dt_ref[...] = proj[:, _C_DT:_N_IN_PAD]

    c = _rms(proj[:, _C_KV:_C_KPE], kvn_ref[...])
    kr = proj[:, _C_KPE:_C_KPES] * rows(cosk_ref) + proj[:, _C_KPES:_C_Z] * rows(sink_ref)
    c_ref[...] = c
    kr_ref[...] = kr[:, :QK_ROPE]
    kv_ref[:, :KV_LORA] = c.astype(_BF)
    kv_ref[:, KV_LORA:] = kr.astype(_BF)

    qn = _rms(proj[:, _C_QLAT:_C_KV], qn_ref[...]).astype(_BF)
    qq = _dot(qn, wuq_ref[...])
    q_pe = qq[:, _Q_ROPE:_Q_ROPES] * rows(cosq_ref) + qq[:, _Q_ROPES:_N_UQ_PAD] * rows(sinq_ref)
    lane = lax.broadcasted_iota(jnp.int32, (tm, LANES), 1)
    heads_per_group = LANES // QK_ROPE
    for hd in range(MLA_HEADS):
        q_abs = _dot(qq[:, hd * LANES:(hd + 1) * LANES].astype(_BF), wuk_ref[hd]) * SCALE_LOG2
        grp = q_pe[:, (hd // heads_per_group) * LANES:(hd // heads_per_group + 1) * LANES]
        shift = (LANES - QK_ROPE * (hd % heads_per_group)) % LANES
        if shift:
            grp = pltpu.roll(grp, shift, 1)
        pe = jnp.where(lane < QK_ROPE, grp * SCALE_LOG2, 0.0)
        q_ref[:, hd, :, :KV_LORA] = q_abs.astype(_BF).reshape(tb, tl, KV_LORA)
        q_ref[:, hd, :, KV_LORA:] = pe.astype(_BF).reshape(tb, tl, LANES)


def _in_proj(x2d, b, l, past_len, p, tb, tl):
    t = b * l
    tm = tb * tl
    n_l = l // tl
    cos32, sin32 = _rope_tables(l, past_len)
    padk = np.zeros((l, LANES - QK_ROPE), np.float32)
    cosk = np.concatenate([cos32, padk], axis=1)
    sink = np.concatenate([sin32, padk], axis=1)
    cosq = np.tile(cos32, (1, MLA_HEADS))
    sinq = np.tile(sin32, (1, MLA_HEADS))

    row_spec = lambda w: pl.BlockSpec((tm, w), lambda i: (i, 0))
    tab_spec = lambda w: pl.BlockSpec((tl, w), lambda i: (i % n_l, 0))
    out_shape = (
        jax.ShapeDtypeStruct((b, MLA_HEADS, l, QK_PAD), _BF),
        jax.ShapeDtypeStruct((t, QK_PAD), _BF),
        jax.ShapeDtypeStruct((t, KV_LORA), _F32),
        jax.ShapeDtypeStruct((t, QK_ROPE), _F32),
        jax.ShapeDtypeStruct((t, D_SSM), _F32),
        jax.ShapeDtypeStruct((t, CONV_DIM), _F32),
        jax.ShapeDtypeStruct((t, LANES), _F32),
    )
    out_specs = (
        pl.BlockSpec((tb, MLA_HEADS, tl, QK_PAD), lambda i: (i // n_l, 0, i % n_l, 0)),
        row_spec(QK_PAD), row_spec(KV_LORA), row_spec(QK_ROPE), row_spec(D_SSM), row_spec(CONV_DIM), row_spec(LANES),
    )
    in_specs = [
        row_spec(D_MODEL), _const_spec((1, D_MODEL)), _const_spec((D_MODEL, _N_IN_PAD)),
        _const_spec((1, Q_LORA)), _const_spec((1, KV_LORA)), _const_spec((Q_LORA, _N_UQ_PAD)),
        _const_spec((MLA_HEADS, LANES, KV_LORA)),
        tab_spec(MLA_HEADS * QK_ROPE), tab_spec(MLA_HEADS * QK_ROPE), tab_spec(LANES), tab_spec(LANES),
    ]
    return pl.pallas_call(
        functools.partial(_in_proj_kernel, tb=tb, tl=tl),
        grid=(t // tm,), in_specs=in_specs, out_specs=out_specs, out_shape=out_shape,
        compiler_params=pltpu.CompilerParams(dimension_semantics=("arbitrary",), vmem_limit_bytes=VMEM_LIMIT),
        name="in_proj",
    )(x2d, p["g_attn"], p["w_in"], p["q_norm"], p["kv_norm"], p["w_uq"], p["w_uk"], cosq, sinq, cosk, sink)


def _in_proj_prompt_kernel(x_ref, g_ref, w_ref, qn_ref, kvn_ref, wuqt_ref, wk_ref, wuvt_ref, cost_ref, sint_ref,
                           cosk_ref, sink_ref, qt_ref, kh_ref, vt_ref, c_ref, kr_ref, z_ref, xbc_ref, dt_ref):
    h = _rms(x_ref[...], g_ref[...]).astype(_BF)
    proj = _dot(h, w_ref[...])
    z_ref[...] = proj[:, _C_Z:_C_XBC]
    xbc_ref[...] = proj[:, _C_XBC:_C_DT]
    dt_ref[...] = proj[:, _C_DT:_N_IN_PAD]

    c = _rms(proj[:, _C_KV:_C_KPE], kvn_ref[...])
    kr = proj[:, _C_KPE:_C_KPES] * cosk_ref[...] + proj[:, _C_KPES:_C_Z] * sink_ref[...]
    c_ref[...] = c
    kr_ref[...] = kr[:, QK_NOPE:QK_NOPE + QK_ROPE]
    c_bf = c.astype(_BF)
    kn = _dot(c_bf, wk_ref[...])
    for hd in range(MLA_HEADS):
        kh_ref[:, hd * LANES:(hd + 1) * LANES] = (kn[:, hd * LANES:(hd + 1) * LANES] + kr).astype(_BF)
    vt_ref[0] = _dot_nt(wuvt_ref[...], c_bf).astype(_BF)

    qn = _rms(proj[:, _C_QLAT:_C_KV], qn_ref[...]).astype(_BF)
    qqt = _dot_nt(wuqt_ref[...], qn)
    half = MLA_HEADS * LANES
    for hd in range(MLA_HEADS):
        qt = (qqt[hd * LANES:(hd + 1) * LANES] * cost_ref[...]
              + qqt[half + hd * LANES:half + (hd + 1) * LANES] * sint_ref[...])
        qt_ref[0, hd] = qt.astype(_BF)


def _in_proj_prompt(x2d, b, l, p, tl):
    t = b * l
    n_l = l // tl
    cos32, sin32 = _rope_tables(l, 0)
    lead = np.zeros((l, QK_NOPE), np.float32)
    trail = np.zeros((l, LANES - QK_NOPE - QK_ROPE), np.float32)
    cosk = np.concatenate([lead, cos32, trail], axis=1)
    sink = np.concatenate([lead, sin32, trail], axis=1)
    cost = np.ascontiguousarray((np.concatenate([lead + 1.0, cos32, trail], axis=1) * np.float32(SCALE_LOG2)).T)
    sint = np.ascontiguousarray((sink * np.float32(SCALE_LOG2)).T)

    row_spec = lambda w: pl.BlockSpec((tl, w), lambda i: (i, 0))
    out_shape = (
        jax.ShapeDtypeStruct((b, MLA_HEADS, LANES, l), _BF),
        jax.ShapeDtypeStruct((t, MLA_HEADS * LANES), _BF),
        jax.ShapeDtypeStruct((b, D_ATTN, l), _BF),
        jax.ShapeDtypeStruct((t, KV_LORA), _F32),
        jax.ShapeDtypeStruct((t, QK_ROPE), _F32),
        jax.ShapeDtypeStruct((t, D_SSM), _F32),
        jax.ShapeDtypeStruct((t, CONV_DIM), _F32),
        jax.ShapeDtypeStruct((t, LANES), _F32),
    )
    out_specs = (
        pl.BlockSpec((1, MLA_HEADS, LANES, tl), lambda i: (i // n_l, 0, 0, i % n_l)),
        row_spec(MLA_HEADS * LANES),
        pl.BlockSpec((1, D_ATTN, tl), lambda i: (i // n_l, 0, i % n_l)),
        row_spec(KV_LORA), row_spec(QK_ROPE), row_spec(D_SSM), row_spec(CONV_DIM), row_spec(LANES),
    )
    in_specs = [
        row_spec(D_MODEL), _const_spec((1, D_MODEL)), _const_spec((D_MODEL, _N_IN_PAD)),
        _const_spec((1, Q_LORA)), _const_spec((1, KV_LORA)), _const_spec((2 * MLA_HEADS * LANES, Q_LORA)),
        _const_spec((KV_LORA, MLA_HEADS * LANES)), _const_spec((D_ATTN, KV_LORA)),
        pl.BlockSpec((LANES, tl), lambda i: (0, i % n_l)), pl.BlockSpec((LANES, tl), lambda i: (0, i % n_l)),
        pl.BlockSpec((tl, LANES), lambda i: (i % n_l, 0)), pl.BlockSpec((tl, LANES), lambda i: (i % n_l, 0)),
    ]
    return pl.pallas_call(
        _in_proj_prompt_kernel,
        grid=(t // tl,), in_specs=in_specs, out_specs=out_specs, out_shape=out_shape,
        compiler_params=pltpu.CompilerParams(dimension_semantics=("arbitrary",), vmem_limit_bytes=VMEM_LIMIT),
        name="in_proj_prompt",
    )(x2d, p["g_attn"], p["w_in_prompt"], p["q_norm"], p["kv_norm"], p["w_uq_t"], p["w_k"], p["w_uv_t"],
      cost, sint, cosk, sink)


def _attn_prompt_kernel(it_ref, jt_ref, qt_ref, kh_ref, vt_ref, o_ref, m_ref, l_ref, acc_ref, *, tq):
    s_idx = pl.program_id(1)
    i = it_ref[s_idx]
    j = jt_ref[s_idx]

    @pl.when(j == 0)
    def _():
        m_ref[...] = jnp.full(m_ref.shape, -jnp.inf, _F32)
        l_ref[...] = jnp.zeros(l_ref.shape, _F32)
        acc_ref[...] = jnp.zeros(acc_ref.shape, _F32)

    def step(masked):
        if masked:
            krow = lax.broadcasted_iota(jnp.int32, (tq, tq), 0)
            qcol = lax.broadcasted_iota(jnp.int32, (tq, tq), 1)
            keep = krow <= qcol

        def scores(hd):
            st = _dot(kh_ref[:, hd * LANES:(hd + 1) * LANES], qt_ref[0, hd])
            if masked:
                st = jnp.where(keep, st, -jnp.inf)
            return st.reshape(tq // SUBLANES, SUBLANES, tq)

        ahead = {hd: scores(hd) for hd in range(QK_AHEAD)}
        for hd in range(MLA_HEADS):
            if hd + QK_AHEAD < MLA_HEADS:
                ahead[hd + QK_AHEAD] = scores(hd + QK_AHEAD)
            st = ahead.pop(hd)
            m_prev = m_ref[hd]
            m_new = jnp.maximum(m_prev, _sublane_all(jnp.maximum, jnp.max(st, axis=0)))
            alpha = jnp.exp2(m_prev - m_new)
            pr = jnp.exp2(st - m_new[None])
            l_ref[hd] = alpha * l_ref[hd] + jnp.sum(pr, axis=0)
            rows = slice(hd * V_HEAD, (hd + 1) * V_HEAD)
            acc = acc_ref[rows, :].reshape(V_HEAD // SUBLANES, SUBLANES, tq) * alpha[None]
            acc_ref[rows, :] = acc.reshape(V_HEAD, tq) + _dot(vt_ref[0, rows, :], pr.reshape(tq, tq).astype(_BF))
            m_ref[hd] = m_new

    @pl.when(j < i)
    def _():
        step(False)

    @pl.when(j == i)
    def _():
        step(True)
        for hd in range(MLA_HEADS):
            rows = slice(hd * V_HEAD, (hd + 1) * V_HEAD)
            den = _sublane_all(jnp.add, l_ref[hd])
            acc = acc_ref[rows, :].reshape(V_HEAD // SUBLANES, SUBLANES, tq) / den[None]
            acc_ref[rows, :] = acc.reshape(V_HEAD, tq)
        o_ref[...] = acc_ref[...].T.astype(o_ref.dtype)


def _attn_prompt(qt, kh, vt, b, l, tq):
    nq = l // tq
    pairs = [(i, j) for i in range(nq) for j in range(i + 1)]
    it = jnp.asarray([pr[0] for pr in pairs], jnp.int32)
    jt = jnp.asarray([pr[1] for pr in pairs], jnp.int32)
    grid_spec = pltpu.PrefetchScalarGridSpec(
        num_scalar_prefetch=2,
        grid=(b, len(pairs)),
        in_specs=[
            pl.BlockSpec((1, MLA_HEADS, LANES, tq), lambda bi, s, it, jt: (bi, 0, 0, it[s])),
            pl.BlockSpec((tq, MLA_HEADS * LANES), lambda bi, s, it, jt: (bi * nq + jt[s], 0)),
            pl.BlockSpec((1, D_ATTN, tq), lambda bi, s, it, jt: (bi, 0, jt[s])),
        ],
        out_specs=pl.BlockSpec((tq, D_ATTN), lambda bi, s, it, jt: (bi * nq + it[s], 0)),
        scratch_shapes=[pltpu.VMEM((MLA_HEADS, SUBLANES, tq), _F32), pltpu.VMEM((MLA_HEADS, SUBLANES, tq), _F32),
                        pltpu.VMEM((D_ATTN, tq), _F32)],
    )
    return pl.pallas_call(
        functools.partial(_attn_prompt_kernel, tq=tq),
        grid_spec=grid_spec,
        out_shape=jax.ShapeDtypeStruct((b * l, D_ATTN), _BF),
        compiler_params=pltpu.CompilerParams(dimension_semantics=("arbitrary", "arbitrary"),
                                             vmem_limit_bytes=VMEM_LIMIT),
        name="attn_prompt",
    )(it, jt, qt, kh, vt)


def _page_copies(pt_ref, cache_c_ref, cache_krt_ref, cbuf_ref, krbuf_ref, sem_ref, item, slot, pg, page):
    pid = pt_ref[item, pg]
    dst = pl.ds(pl.multiple_of(pg * page, page), page)
    return (pltpu.make_async_copy(cache_c_ref.at[pid], cbuf_ref.at[slot, dst, :], sem_ref.at[slot, 0]),
            pltpu.make_async_copy(cache_krt_ref.at[pid], krbuf_ref.at[slot, :, dst], sem_ref.at[slot, 1]))


def _wait_item(cbuf_ref, krbuf_ref, sem_ref, slot):
    pltpu.make_async_copy(cbuf_ref.at[slot], cbuf_ref.at[slot], sem_ref.at[slot, 0]).wait()
    pltpu.make_async_copy(krbuf_ref.at[slot], krbuf_ref.at[slot], sem_ref.at[slot, 1]).wait()


def _attend_item(q, kvn, slot, cbuf_ref, krbuf_ref, kc_ref, *, n_pages, page, l_new, n_split):
    m_rows = MLA_HEADS * l_new
    q_abs = q[:, :KV_LORA]
    q_pe = q[:, KV_LORA:KV_LORA + QK_ROPE]
    ks = (n_pages // n_split) * page

    def scores(sp):
        rows = pl.ds(sp * ks, ks)
        kc_ref[rows, :] = cbuf_ref[slot, rows, :].astype(_BF)
        return _dot_nt(q_abs, kc_ref[rows, :]) + _dot(q_pe, krbuf_ref[slot, :, rows].astype(_BF))

    def partial(s, values):
        m = jnp.max(s, axis=1, keepdims=True)
        pr = jnp.exp2(s - m)
        return m, jnp.sum(pr, axis=1, keepdims=True), _dot(pr.astype(_BF), values)

    def merge(run, new):
        m = jnp.maximum(run[0], new[0])
        a, c = jnp.exp2(run[0] - m), jnp.exp2(new[0] - m)
        return m, a * run[1] + c * new[1], a * run[2] + c * new[2]

    ahead = {sp: scores(sp) for sp in range(min(SPLITS_AHEAD, n_split))}
    sn = _dot_nt(q, kvn)
    qpos = lax.broadcasted_iota(jnp.int32, (MLA_HEADS, l_new, l_new), 1).reshape(m_rows, l_new)
    kpos = lax.broadcasted_iota(jnp.int32, (m_rows, l_new), 1)
    run = partial(jnp.where(kpos <= qpos, sn, -jnp.inf), kvn[:, :KV_LORA])
    for sp in range(n_split):
        if sp + SPLITS_AHEAD < n_split:
            ahead[sp + SPLITS_AHEAD] = scores(sp + SPLITS_AHEAD)
        run = merge(run, partial(ahead.pop(sp), kc_ref[pl.ds(sp * ks, ks), :]))
    return run[2] / run[1]


def _uv_proj_kernel(o_ref, wuv_ref, out_ref, *, tb, l_new):
    for hd in range(MLA_HEADS):
        o = o_ref[:, hd].reshape(tb * l_new, KV_LORA)
        out_ref[:, hd * V_HEAD:(hd + 1) * V_HEAD] = _dot(o, wuv_ref[hd]).astype(out_ref.dtype)


def _uv_proj(o_lat, wuv, tb):
    b, _, l_new, _ = o_lat.shape
    return pl.pallas_call(
        functools.partial(_uv_proj_kernel, tb=tb, l_new=l_new),
        grid=(b // tb,),
        in_specs=[pl.BlockSpec((tb, MLA_HEADS, l_new, KV_LORA), lambda i: (i, 0, 0, 0)),
                  _const_spec((MLA_HEADS, KV_LORA, V_HEAD))],
        out_specs=pl.BlockSpec((tb * l_new, D_ATTN), lambda i: (i, 0)),
        out_shape=jax.ShapeDtypeStruct((b * l_new, D_ATTN), _BF),
        compiler_params=pltpu.CompilerParams(dimension_semantics=("arbitrary",), vmem_limit_bytes=VMEM_LIMIT),
        name="uv_proj",
    )(o_lat, wuv)


def _ssd_kernel(xbc_ref, z_ref, dtr_ref, cbuf_ref, h0_ref, cw_ref, cb_ref, dtb_ref, alog_ref, dskip_ref, nrm_ref,
                sel_ref, ex_ref, exw_ref, o_ref, h_ref, cnew_ref, xp_ref, y_ref, *, n_seq, n_chunk, q):
    assert n_seq == 1 or n_chunk == 1
    g_items = n_seq * n_chunk
    c_idx = pl.program_id(1)
    last = pl.num_programs(1) - 1
    heads_per_group = SSM_HEADS // SSM_GROUPS
    rows = g_items * q
    row = lax.broadcasted_iota(jnp.int32, (q, q), 0)
    col = lax.broadcasted_iota(jnp.int32, (q, q), 1)
    causal = row >= col
    n_bc = SSM_GROUPS * D_STATE

    def conv_item(g):
        sq, ck = divmod(g, n_chunk)
        rs = slice(g * q, (g + 1) * q)

        if ck == 0:
            @pl.when(c_idx == 0)
            def _():
                h_ref[sq] = h0_ref[sq]
                xp_ref[sq, _XP_OFF - (CONV_K - 1):_XP_OFF, :] = cbuf_ref[sq]

        act, tail = _conv_silu(xp_ref.at[sq], xbc_ref[rs, :], cw_ref, cb_ref, q)

        if ck == n_chunk - 1:
            @pl.when(c_idx == last)
            def _():
                cnew_ref[sq] = tail

        return act[:, :D_SSM], act[:, D_SSM:D_SSM + n_bc].astype(_BF), act[:, D_SSM + n_bc:].astype(_BF)

    convs = {g: conv_item(g) for g in range(g_items)} if n_seq == 1 else {}

    if g_items == 1:
        tril = jnp.where(causal, 1.0, 0.0).astype(_BF)
    else:
        r_all = lax.broadcasted_iota(jnp.int32, (rows, rows), 0)
        c_all = lax.broadcasted_iota(jnp.int32, (rows, rows), 1)
        same_item = (r_all // q) == (c_all // q)
        tril = jnp.where(r_all >= c_all, jnp.where(same_item, 1.0, 0.0), 0.0).astype(_BF)
    a_neg = -jnp.exp(alog_ref[...])
    dtv = dtr_ref[...] + dtb_ref[...]
    dt = jnp.maximum(dtv, 0.0) + jnp.log1p(jnp.exp(-jnp.abs(dtv)))
    cs = _dot(tril, _split3(dt * a_neg))
    acs = cs[:, :LANES] + cs[:, LANES:2 * LANES] + cs[:, 2 * LANES:]
    acs3 = _split3(acs)
    acs_t = _dot_nt(sel_ref[...], acs3)
    dt_x = _dot(_split3(dt), ex_ref[...])
    acs_x = _dot(acs3, ex_ref[...])
    acs_w = _dot(acs3, exw_ref[...])
    e_acs = jnp.exp(acs_x)

    for g in range(g_items):
        sq = g // n_chunk
        rs = slice(g * q, (g + 1) * q)
        xs, bmat, cmat = convs[g] if g in convs else conv_item(g)
        acs_last_x = acs_x[(g + 1) * q - 1:(g + 1) * q, :]
        x_dt = xs * dt_x[rs]
        xw = (x_dt * jnp.exp(acs_last_x - acs_x[rs])).astype(_BF)
        x_dt = x_dt.astype(_BF)
        for grp in range(SSM_GROUPS):
            b_g = bmat[:, grp * D_STATE:(grp + 1) * D_STATE]
            c_g = cmat[:, grp * D_STATE:(grp + 1) * D_STATE]
            cb = _dot_nt(c_g, b_g)
            for hd in range(grp * heads_per_group, (grp + 1) * heads_per_group):
                cols = slice(hd * SSM_HEADDIM, (hd + 1) * SSM_HEADDIM)
                acs_h = acs_w[rs, hd * LANES:(hd + 1) * LANES]
                acs_h = acs_h[:, :q] if q <= LANES else jnp.concatenate([acs_h] * (q // LANES), axis=1)
                lmat = jnp.exp(jnp.where(causal, acs_h - acs_t[hd:hd + 1, rs], -jnp.inf))
                h_prev = h_ref[sq, hd]
                y = _dot((cb * lmat).astype(_BF), x_dt[:, cols])
                y = y + _dot_nt(c_g, h_prev.astype(_BF)) * e_acs[rs, cols]
                decay = jnp.exp(acs[(g + 1) * q - 1:(g + 1) * q, hd:hd + 1])
                h_ref[sq, hd] = decay * h_prev + _dot_tn(xw[:, cols], b_g)
                y_ref[:, cols] = y
        yv = y_ref[...] + dskip_ref[...] * xs
        gated = yv * _silu(z_ref[g * q:(g + 1) * q, :])
        gw = D_SSM // SSM_GROUPS
        parts = []
        for grp in range(SSM_GROUPS):
            gg = gated[:, grp * gw:(grp + 1) * gw]
            parts.append(gg * lax.rsqrt(jnp.mean(gg * gg, axis=-1, keepdims=True) + EPS))
        o_ref[g * q:(g + 1) * q, :] = (jnp.concatenate(parts, axis=1) * nrm_ref[...]).astype(o_ref.dtype)


def _ssd(xbc, z, dtr, conv_buf, h0, p, b, l, q, n_seq, n_chunk):
    nc = l // (q * n_chunk)
    rows = n_seq * n_chunk * q
    row_spec = lambda w: pl.BlockSpec((rows, w), lambda bi, c: (bi * nc + c, 0))
    state_spec = pl.BlockSpec((n_seq, SSM_HEADS, SSM_HEADDIM, D_STATE), lambda bi, c: (bi, 0, 0, 0))
    hist_spec = pl.BlockSpec((n_seq, CONV_K - 1, CONV_DIM), lambda bi, c: (bi, 0, 0))
    consts = [p["dt_bias"], p["a_log"], p["d_skip"], p["ssm_norm"], p["sel"], p["head_to_cols"], p["head_to_lanes"]]
    const_specs = [_const_spec((1, LANES)), _const_spec((1, LANES)), _const_spec((1, D_SSM)), _const_spec((1, D_SSM)),
                   _const_spec((SSM_HEADS, 3 * LANES)), _const_spec((3 * LANES, D_SSM)),
                   _const_spec((3 * LANES, SSM_HEADS * LANES))]
    in_specs = [row_spec(CONV_DIM), row_spec(D_SSM), row_spec(LANES), hist_spec, state_spec,
                _const_spec((CONV_K, CONV_DIM)), _const_spec((1, CONV_DIM)), *const_specs]
    out_shape = (jax.ShapeDtypeStruct((b * l, D_SSM), _BF),
                 jax.ShapeDtypeStruct((b, SSM_HEADS, SSM_HEADDIM, D_STATE), _F32),
                 jax.ShapeDtypeStruct((b, CONV_K - 1, CONV_DIM), _F32))
    return pl.pallas_call(
        functools.partial(_ssd_kernel, n_seq=n_seq, n_chunk=n_chunk, q=q),
        grid=(b // n_seq, nc), in_specs=in_specs, out_specs=(row_spec(D_SSM), state_spec, hist_spec),
        out_shape=out_shape,
        scratch_shapes=[pltpu.VMEM((n_seq, _XP_OFF + q, CONV_DIM), _F32), pltpu.VMEM((q, D_SSM), _F32)],
        compiler_params=pltpu.CompilerParams(dimension_semantics=("arbitrary", "arbitrary"),
                                             vmem_limit_bytes=VMEM_LIMIT),
        name="ssd",
    )(xbc, z, dtr, conv_buf, h0, p["conv_w"], p["conv_b"], *consts)


def _mlp_chunk(h2, acc, wu_ref, wd_ref, f0, f_chunk):
    u = jnp.maximum(_dot(h2, wu_ref[:, f0:f0 + f_chunk]), 0.0)
    return acc + _dot((u * u).astype(_BF), wd_ref[f0:f0 + f_chunk, :])


def _out_mlp_kernel(x_ref, oa_ref, os_ref, wo_ref, gm_ref, wu_ref, wd_ref, gf_ref, y_ref, *, f_chunk):
    mix = jnp.concatenate([oa_ref[...], os_ref[...]], axis=1)
    x1 = x_ref[...] + _dot(mix, wo_ref[...])
    h2 = _rms(x1, gm_ref[...]).astype(_BF)
    acc = x1
    for f0 in range(0, D_FF, f_chunk):
        acc = _mlp_chunk(h2, acc, wu_ref, wd_ref, f0, f_chunk)
    y_ref[...] = _rms(acc, gf_ref[...])


def _mlp_attn_kernel(pt_ref, x_ref, oa_ref, os_ref, wo_ref, gm_ref, wu_ref, wd_ref, gf_ref, q_ref, kvn_ref,
                     cache_c_ref, cache_krt_ref, y_ref, o_ref, cbuf_ref, krbuf_ref, kc_ref, sem_ref, *,
                     f_chunk, ipb, n_items, n_pages, page, l_new, n_split):
    i = pl.program_id(0)
    copies = functools.partial(_page_copies, pt_ref, cache_c_ref, cache_krt_ref, cbuf_ref, krbuf_ref, sem_ref)

    ahead = GATHER_SLOTS - 1

    @pl.when(i == 0)
    def _():
        for first in range(ahead):
            def body(pg, carry):
                for cp in copies(min(first, n_items - 1), first, pg, page):
                    cp.start()
                return carry
            lax.fori_loop(0, n_pages, body, 0)

    mix = jnp.concatenate([oa_ref[...], os_ref[...]], axis=1)
    x1 = x_ref[...] + _dot(mix, wo_ref[...])
    h2 = _rms(x1, gm_ref[...]).astype(_BF)
    acc = x1
    f_starts = list(range(0, D_FF, f_chunk))
    per_item = -(-len(f_starts) // ipb)
    for k in range(ipb):
        item = i * ipb + k
        slot = item % GATHER_SLOTS
        _wait_item(cbuf_ref, krbuf_ref, sem_ref, slot)
        nxt = jnp.minimum(item + ahead, n_items - 1)
        nxt_slot = (item + ahead) % GATHER_SLOTS
        for pg in range(n_pages):
            for cp in copies(nxt, nxt_slot, pg, page):
                cp.start()
        o = _attend_item(q_ref[k], kvn_ref[k], slot, cbuf_ref, krbuf_ref, kc_ref, n_pages=n_pages, page=page,
                         l_new=l_new, n_split=n_split)
        o_ref[k] = o.astype(o_ref.dtype)
        for f0 in f_starts[k * per_item:(k + 1) * per_item]:
            acc = _mlp_chunk(h2, acc, wu_ref, wd_ref, f0, f_chunk)
    for f0 in f_starts[ipb * per_item:]:
        acc = _mlp_chunk(h2, acc, wu_ref, wd_ref, f0, f_chunk)
    y_ref[...] = _rms(acc, gf_ref[...])

    @pl.when(i == pl.num_programs(0) - 1)
    def _():
        for extra in range(ahead):
            _wait_item(cbuf_ref, krbuf_ref, sem_ref, (n_items + extra) % GATHER_SLOTS)


def _mlp_attn(x2d, oa, osm, q, kvn, cache_c, cache_krt, page_table, p, tm):
    t = x2d.shape[0]
    n_steps = t // tm
    b, m_rows, _ = q.shape
    assert b % n_steps == 0, "sample batch must split evenly over the MLP token tiles"
    ipb = b // n_steps
    l_new = m_rows // MLA_HEADS
    n_pages = page_table.shape[1]
    page = cache_c.shape[1]
    keys = n_pages * page
    row_spec = lambda w: pl.BlockSpec((tm, w), lambda i, pt: (i, 0))
    once = lambda shape: pl.BlockSpec(shape, lambda i, pt: (0,) * len(shape), pipeline_mode=pl.Buffered(1))
    item_spec = lambda r, w: pl.BlockSpec((ipb, r, w), lambda i, pt: (i, 0, 0))
    grid_spec = pltpu.PrefetchScalarGridSpec(
        num_scalar_prefetch=1,
        grid=(n_steps,),
        in_specs=[row_spec(D_MODEL), row_spec(D_ATTN), row_spec(D_SSM), once((D_MODEL, D_MODEL)),
                  once((1, D_MODEL)), once((D_MODEL, D_FF)), once((D_FF, D_MODEL)), once((1, D_MODEL)),
                  item_spec(m_rows, QK_PAD), item_spec(l_new, QK_PAD),
                  pl.BlockSpec(memory_space=pl.ANY), pl.BlockSpec(memory_space=pl.ANY)],
        out_specs=(row_spec(D_MODEL), item_spec(m_rows, KV_LORA)),
        scratch_shapes=[pltpu.VMEM((GATHER_SLOTS, keys, KV_LORA), _F32), pltpu.VMEM((GATHER_SLOTS, QK_ROPE, keys), _F32),
                        pltpu.VMEM((keys, KV_LORA), _BF), pltpu.SemaphoreType.DMA((GATHER_SLOTS, 2))],
    )
    return pl.pallas_call(
        functools.partial(_mlp_attn_kernel, f_chunk=1024, ipb=ipb, n_items=b, n_pages=n_pages, page=page,
                          l_new=l_new, n_split=_largest_divisor(n_pages, 8)),
        grid_spec=grid_spec,
        out_shape=(jax.ShapeDtypeStruct((t, D_MODEL), _F32), jax.ShapeDtypeStruct((b, m_rows, KV_LORA), _BF)),
        compiler_params=pltpu.CompilerParams(dimension_semantics=("arbitrary",), vmem_limit_bytes=VMEM_LIMIT_FUSED),
        name="mlp_attn",
    )(page_table, x2d, oa, osm, p["w_out"], p["g_mlp"], p["w_up"], p["w_down"], p["g_final"], q, kvn, cache_c,
      cache_krt)


def _out_mlp(x2d, oa, osm, p, tm):
    t = x2d.shape[0]
    row_spec = lambda w: pl.BlockSpec((tm, w), lambda i: (i, 0))
    once = lambda shape: pl.BlockSpec(shape, lambda i: (0,) * len(shape), pipeline_mode=pl.Buffered(1))
    return pl.pallas_call(
        functools.partial(_out_mlp_kernel, f_chunk=1024),
        grid=(t // tm,),
        in_specs=[row_spec(D_MODEL), row_spec(D_ATTN), row_spec(D_SSM), once((D_MODEL, D_MODEL)),
                  once((1, D_MODEL)), once((D_MODEL, D_FF)), once((D_FF, D_MODEL)), once((1, D_MODEL))],
        out_specs=row_spec(D_MODEL),
        out_shape=jax.ShapeDtypeStruct((t, D_MODEL), _F32),
        compiler_params=pltpu.CompilerParams(dimension_semantics=("arbitrary",), vmem_limit_bytes=VMEM_LIMIT),
        name="out_mlp",
    )(x2d, oa, osm, p["w_out"], p["g_mlp"], p["w_up"], p["w_down"], p["g_final"])


def _prep_params(norm_attn, w_in, q_norm, kv_norm, w_uq, w_uk, w_uv, conv_w, conv_b, dt_bias, a_log, d_skip,
                 ssm_norm, w_out, norm_mlp, w_up, w_down, norm_final):
    half = QK_ROPE // 2

    def swap(w):
        return jnp.concatenate([w[..., half:], w[..., :half]], axis=-1)

    def place(w, lead, n):
        return jnp.pad(w, ((0, 0), (lead, n - lead - w.shape[1])))

    o1 = Q_LORA
    o2 = o1 + KV_LORA
    o3 = o2 + QK_ROPE
    o5 = o3 + D_SSM + CONV_DIM
    w_kpe = w_in[:, o2:o3]

    def w_in_padded(lead):
        return jnp.concatenate([
            w_in[:, :o2], place(w_kpe, lead, LANES), place(swap(w_kpe), lead, LANES), w_in[:, o3:o5],
            place(w_in[:, o5:], 0, LANES)], axis=1).astype(_BF)

    uq = w_uq.reshape(Q_LORA, MLA_HEADS, QK_NOPE + QK_ROPE)
    uq_nope = uq[:, :, :QK_NOPE]
    uq_rope = uq[:, :, QK_NOPE:]
    w_uq_p = jnp.concatenate([
        jnp.pad(uq_nope, ((0, 0), (0, 0), (0, LANES - QK_NOPE))).reshape(Q_LORA, MLA_HEADS * LANES),
        uq_rope.reshape(Q_LORA, -1), swap(uq_rope).reshape(Q_LORA, -1)], axis=1).astype(_BF)
    w_uk_p = jnp.pad(w_uk, ((0, 0), (0, LANES - QK_NOPE), (0, 0))).astype(_BF)
    tail = LANES - QK_NOPE - QK_ROPE
    plain = jnp.pad(uq, ((0, 0), (0, 0), (0, tail)))
    swapped = jnp.pad(swap(uq_rope), ((0, 0), (0, 0), (QK_NOPE, tail)))
    w_uq_t = jnp.concatenate([plain.reshape(Q_LORA, -1), swapped.reshape(Q_LORA, -1)], axis=1).T.astype(_BF)
    w_k = jnp.pad(jnp.transpose(w_uk, (2, 0, 1)), ((0, 0), (0, 0), (0, LANES - QK_NOPE)))
    w_k = w_k.reshape(KV_LORA, MLA_HEADS * LANES).astype(_BF)
    w_uv_t = jnp.transpose(w_uv, (0, 2, 1)).reshape(D_ATTN, KV_LORA).astype(_BF)
    pad_heads = lambda v: jnp.pad(v.reshape(1, SSM_HEADS), ((0, 0), (0, LANES - SSM_HEADS)))
    return dict(
        g_attn=norm_attn.reshape(1, D_MODEL), w_in=w_in_padded(0), w_in_prompt=w_in_padded(QK_NOPE),
        q_norm=q_norm.reshape(1, Q_LORA), kv_norm=kv_norm.reshape(1, KV_LORA), w_uq=w_uq_p, w_uk=w_uk_p,
        w_uq_t=w_uq_t, w_k=w_k, w_uv_t=w_uv_t, w_uv=w_uv.astype(_BF),
        conv_w=conv_w, conv_b=conv_b.reshape(1, CONV_DIM), dt_bias=pad_heads(dt_bias), a_log=pad_heads(a_log),
        d_skip=jnp.repeat(d_skip, SSM_HEADDIM).reshape(1, D_SSM), ssm_norm=ssm_norm.reshape(1, D_SSM),
        sel=jnp.asarray(np.tile(np.eye(SSM_HEADS, LANES), (1, 3)), _BF),
        head_to_cols=jnp.asarray(np.tile(np.repeat(np.eye(LANES, SSM_HEADS), SSM_HEADDIM, axis=1), (3, 1)), _BF),
        head_to_lanes=jnp.asarray(np.tile(np.repeat(np.eye(LANES, SSM_HEADS), LANES, axis=1), (3, 1)), _BF),
        w_out=w_out.astype(_BF), g_mlp=norm_mlp.reshape(1, D_MODEL), w_up=w_up.astype(_BF),
        w_down=w_down.astype(_BF), g_final=norm_final.reshape(1, D_MODEL),
    )


def _largest_divisor(n, cap):
    d = min(n, cap)
    while n % d:
        d -= 1
    return d


def kernel(x_prompt, x_sample, cache_kv_latent, cache_k_rope, state_ssm, state_conv, page_table, norm_attn, w_in, q_norm, kv_norm, w_uq, w_uk, w_uv, conv_w, conv_b, dt_bias, a_log, d_skip, ssm_norm, w_out, norm_mlp, w_up, w_down, norm_final):
    assert norm_attn.shape[0] == 1, "single-layer model"
    p = _prep_params(norm_attn[0], w_in[0], q_norm[0], kv_norm[0], w_uq[0], w_uk[0], w_uv[0], conv_w[0], conv_b[0],
                     dt_bias[0], a_log[0], d_skip[0], ssm_norm[0], w_out[0], norm_mlp[0], w_up[0], w_down[0],
                     norm_final)
    bp, lp, _ = x_prompt.shape
    bs, ls, _ = x_sample.shape
    n_pages = page_table.shape[1]
    page = cache_kv_latent.shape[2]
    past_len = n_pages * page

    xp2 = x_prompt.reshape(bp * lp, D_MODEL)
    xs2 = x_sample.reshape(bs * ls, D_MODEL)
    qt, kh, vt, c_p, kr_p, z_p, xbc_p, dtr_p = _in_proj_prompt(xp2, bp, lp, p, _largest_divisor(lp, 512))
    tb = _largest_divisor(bs, 512 // ls)
    q_s, kv_s, c_s, kr_s, z_s, xbc_s, dtr_s = _in_proj(xs2, bs, ls, past_len, p, tb, ls)

    o_attn_p = _attn_prompt(qt, kh, vt, bp, lp, _largest_divisor(lp, 512))
    chunk = _largest_divisor(lp, CHUNK)
    o_ssm_p, h_p, conv_p = _ssd(xbc_p, z_p, dtr_p, jnp.zeros((bp, CONV_K - 1, CONV_DIM), _F32),
                                jnp.zeros((bp, SSM_HEADS, SSM_HEADDIM, D_STATE), _F32), p, bp, lp,
                                chunk, 1, _largest_divisor(lp // chunk, 2))

    cache_krt = jnp.swapaxes(cache_k_rope[0], 1, 2)
    y_p, o_lat = _mlp_attn(xp2, o_attn_p, o_ssm_p, q_s.reshape(bs, MLA_HEADS * ls, QK_PAD),
                           kv_s.reshape(bs, ls, QK_PAD), cache_kv_latent[0], cache_krt, page_table, p,
                           _largest_divisor(bp * lp, 256))
    outs_p = (y_p.reshape(bp, lp, D_MODEL), c_p.reshape(1, bp, lp, KV_LORA), kr_p.reshape(1, bp, lp, QK_ROPE),
              h_p[None], conv_p[None])

    o_attn_s = _uv_proj(o_lat.reshape(bs, MLA_HEADS, ls, KV_LORA), p["w_uv"], tb)
    o_ssm_s, h_s, conv_s = _ssd(xbc_s, z_s, dtr_s, state_conv[0], state_ssm[0], p, bs, ls, ls,
                                _largest_divisor(bs, 16), 1)
    y_s = _out_mlp(xs2, o_attn_s, o_ssm_s, p, _largest_divisor(bs * ls, 512))
    outs_s = (y_s.reshape(bs, ls, D_MODEL), c_s.reshape(1, bs, ls, KV_LORA), kr_s.reshape(1, bs, ls, QK_ROPE),
              h_s[None], conv_s[None])

    return (outs_p[0], outs_s[0], outs_p[1], outs_p[2], outs_p[3], outs_p[4],
            outs_s[1], outs_s[2], outs_s[3], outs_s[4])
```

```python
import functools

import jax
import jax.numpy as jnp
import numpy as np
from jax import lax
from jax.experimental import pallas as pl
from jax.experimental.pallas import tpu as pltpu

D_MODEL = 1024
D_ATTN = 512
D_SSM = 512
V_HEAD = 64
MLA_HEADS = 8
QK_NOPE = 64
QK_ROPE = 32
KV_LORA = 256
Q_LORA = 384
ROPE_THETA = 10000.0
SSM_HEADDIM = 64
SSM_HEADS = 8
SSM_GROUPS = 2
D_STATE = 128
CONV_K = 4
CHUNK = 256
CONV_DIM = D_SSM + 2 * SSM_GROUPS * D_STATE
D_FF = 4096
EPS = 1e-6

LANES = 128
SUBLANES = 8
QK_AHEAD = 2
SPLITS_AHEAD = 4
QK_PAD = KV_LORA + LANES
VMEM_LIMIT = 56 * 1024 * 1024
VMEM_LIMIT_FUSED = 60 * 1024 * 1024
SCALE_LOG2 = (QK_NOPE + QK_ROPE) ** -0.5 * 1.4426950408889634

_C_QLAT = 0
_C_KV = _C_QLAT + Q_LORA
_C_KPE = _C_KV + KV_LORA
_C_KPES = _C_KPE + LANES
_C_Z = _C_KPES + LANES
_C_XBC = _C_Z + D_SSM
_C_DT = _C_XBC + CONV_DIM
_N_IN_PAD = _C_DT + LANES
_Q_NOPE = 0
_Q_ROPE = MLA_HEADS * LANES
_Q_ROPES = _Q_ROPE + MLA_HEADS * QK_ROPE
_N_UQ_PAD = _Q_ROPES + MLA_HEADS * QK_ROPE

_BF = jnp.bfloat16
_F32 = jnp.float32


def _dot(a, b):
    return jnp.dot(a, b, preferred_element_type=_F32)


def _dot_nt(a, b):
    return lax.dot_general(a, b, (((1,), (1,)), ((), ())), preferred_element_type=_F32)


def _dot_tn(a, b):
    return lax.dot_general(a, b, (((0,), (0,)), ((), ())), preferred_element_type=_F32)


def _split3(a):
    hi = a.astype(_BF)
    r1 = a - hi.astype(_F32)
    mid = r1.astype(_BF)
    lo = (r1 - mid.astype(_F32)).astype(_BF)
    return jnp.concatenate([hi, mid, lo], axis=1)


def _rms(x, g):
    return x * lax.rsqrt(jnp.mean(x * x, axis=-1, keepdims=True) + EPS) * g


def _silu(x):
    return x * (1.0 / (1.0 + jnp.exp(-x)))


def _sublane_all(op, x):
    for shift in (4, 2, 1):
        x = op(x, pltpu.roll(x, shift, 0))
    return x


def _const_spec(shape):
    nd = len(shape)
    return pl.BlockSpec(shape, lambda *_: (0,) * nd)


def _rope_tables(l, past_len):
    pos = past_len + np.arange(l, dtype=np.float64)
    inv = ROPE_THETA ** (-(np.arange(0, QK_ROPE, 2, dtype=np.float64) / QK_ROPE))
    ang = pos[:, None] * inv[None, :]
    cos, sin = np.cos(ang).astype(np.float32), np.sin(ang).astype(np.float32)
    return np.concatenate([cos, cos], axis=1), np.concatenate([-sin, sin], axis=1)


_XP_OFF = 8


def _conv_silu(win_ref, x_new, cw_ref, cb_ref, q):
    win_ref[_XP_OFF:_XP_OFF + q, :] = x_new
    conv = cb_ref[...]
    for k in range(CONV_K):
        conv = conv + win_ref[_XP_OFF - (CONV_K - 1) + k:_XP_OFF - (CONV_K - 1) + k + q, :] * cw_ref[k:k + 1, :]
    tail = win_ref[_XP_OFF + q - (CONV_K - 1):_XP_OFF + q, :]
    win_ref[_XP_OFF - (CONV_K - 1):_XP_OFF, :] = tail
    return _silu(conv), tail


def _in_proj_kernel(x_ref, g_ref, w_ref, qn_ref, kvn_ref, wuq_ref, wuk_ref, cosq_ref, sinq_ref, cosk_ref, sink_ref,
                    q_ref, kv_ref, c_ref, kr_ref, z_ref, xbc_ref, dt_ref, *, tb, tl):
    tm = tb * tl

    def rows(tab_ref):
        t = tab_ref[...]
        if tb == 1:
            return t
        return jnp.broadcast_to(t[None], (tb, tl, t.shape[-1])).reshape(tm, t.shape[-1])

    h = _rms(x_ref[...], g_ref[...]).astype(_BF)
    proj = _dot(h, w_ref[...])
    z_ref[...] = proj[:, _C_Z:_C_XBC]
    xbc_ref[...] = proj[:, _C_XBC:_C_DT]
    dt_ref[...] = proj[:, _C_DT:_N_IN_PAD]

    c = _rms(proj[:, _C_KV:_C_KPE], kvn_ref[...])
    kr = proj[:, _C_KPE:_C_KPES] * rows(cosk_ref) + proj[:, _C_KPES:_C_Z] * rows(sink_ref)
    c_ref[...] = c
    kr_ref[...] = kr[:, :QK_ROPE]
    kv_ref[:, :KV_LORA] = c.astype(_BF)
    kv_ref[:, KV_LORA:] = kr.astype(_BF)

    qn = _rms(proj[:, _C_QLAT:_C_KV], qn_ref[...]).astype(_BF)
    qq = _dot(qn, wuq_ref[...])
    q_pe = qq[:, _Q_ROPE:_Q_ROPES] * rows(cosq_ref) + qq[:, _Q_ROPES:_N_UQ_PAD] * rows(sinq_ref)
    lane = lax.broadcasted_iota(jnp.int32, (tm, LANES), 1)
    heads_per_group = LANES // QK_ROPE
    for hd in range(MLA_HEADS):
        q_abs = _dot(qq[:, hd * LANES:(hd + 1) * LANES].astype(_BF), wuk_ref[hd]) * SCALE_LOG2
        grp = q_pe[:, (hd // heads_per_group) * LANES:(hd // heads_per_group + 1) * LANES]
        shift = (LANES - QK_ROPE * (hd % heads_per_group)) % LANES
        if shift:
            grp = pltpu.roll(grp, shift, 1)
        pe = jnp.where(lane < QK_ROPE, grp * SCALE_LOG2, 0.0)
        q_ref[:, hd, :, :KV_LORA] = q_abs.astype(_BF).reshape(tb, tl, KV_LORA)
        q_ref[:, hd, :, KV_LORA:] = pe.astype(_BF).reshape(tb, tl, LANES)


def _in_proj(x2d, b, l, past_len, p, tb, tl):
    t = b * l
    tm = tb * tl
    n_l = l // tl
    cos32, sin32 = _rope_tables(l, past_len)
    padk = np.zeros((l, LANES - QK_ROPE), np.float32)
    cosk = np.concatenate([cos32, padk], axis=1)
    sink = np.concatenate([sin32, padk], axis=1)
    cosq = np.tile(cos32, (1, MLA_HEADS))
    sinq = np.tile(sin32, (1, MLA_HEADS))

    row_spec = lambda w: pl.BlockSpec((tm, w), lambda i: (i, 0))
    tab_spec = lambda w: pl.BlockSpec((tl, w), lambda i: (i % n_l, 0))
    out_shape = (
        jax.ShapeDtypeStruct((b, MLA_HEADS, l, QK_PAD), _BF),
        jax.ShapeDtypeStruct((t, QK_PAD), _BF),
        jax.ShapeDtypeStruct((t, KV_LORA), _F32),
        jax.ShapeDtypeStruct((t, QK_ROPE), _F32),
        jax.ShapeDtypeStruct((t, D_SSM), _F32),
        jax.ShapeDtypeStruct((t, CONV_DIM), _F32),
        jax.ShapeDtypeStruct((t, LANES), _F32),
    )
    out_specs = (
        pl.BlockSpec((tb, MLA_HEADS, tl, QK_PAD), lambda i: (i // n_l, 0, i % n_l, 0)),
        row_spec(QK_PAD), row_spec(KV_LORA), row_spec(QK_ROPE), row_spec(D_SSM), row_spec(CONV_DIM), row_spec(LANES),
    )
    in_specs = [
        row_spec(D_MODEL), _const_spec((1, D_MODEL)), _const_spec((D_MODEL, _N_IN_PAD)),
        _const_spec((1, Q_LORA)), _const_spec((1, KV_LORA)), _const_spec((Q_LORA, _N_UQ_PAD)),
        _const_spec((MLA_HEADS, LANES, KV_LORA)),
        tab_spec(MLA_HEADS * QK_ROPE), tab_spec(MLA_HEADS * QK_ROPE), tab_spec(LANES), tab_spec(LANES),
    ]
    return pl.pallas_call(
        functools.partial(_in_proj_kernel, tb=tb, tl=tl),
        grid=(t // tm,), in_specs=in_specs, out_specs=out_specs, out_shape=out_shape,
        compiler_params=pltpu.CompilerParams(dimension_semantics=("arbitrary",), vmem_limit_bytes=VMEM_LIMIT),
        name="in_proj",
    )(x2d, p["g_attn"], p["w_in"], p["q_norm"], p["kv_norm"], p["w_uq"], p["w_uk"], cosq, sinq, cosk, sink)


def _in_proj_prompt_kernel(x_ref, g_ref, w_ref, qn_ref, kvn_ref, wuqt_ref, wk_ref, wuvt_ref, cost_ref, sint_ref,
                           cosk_ref, sink_ref, qt_ref, kh_ref, vt_ref, c_ref, kr_ref, z_ref, xbc_ref, dt_ref):
    h = _rms(x_ref[...], g_ref[...]).astype(_BF)
    proj = _dot(h, w_ref[...])
    z_ref[...] = proj[:, _C_Z:_C_XBC]
    xbc_ref[...] = proj[:, _C_XBC:_C_DT]
    dt_ref[...] = proj[:, _C_DT:_N_IN_PAD]

    c = _rms(proj[:, _C_KV:_C_KPE], kvn_ref[...])
    kr = proj[:, _C_KPE:_C_KPES] * cosk_ref[...] + proj[:, _C_KPES:_C_Z] * sink_ref[...]
    c_ref[...] = c
    kr_ref[...] = kr[:, QK_NOPE:QK_NOPE + QK_ROPE]
    c_bf = c.astype(_BF)
    kn = _dot(c_bf, wk_ref[...])
    for hd in range(MLA_HEADS):
        kh_ref[:, hd * LANES:(hd + 1) * LANES] = (kn[:, hd * LANES:(hd + 1) * LANES] + kr).astype(_BF)
    vt_ref[0] = _dot_nt(wuvt_ref[...], c_bf).astype(_BF)

    qn = _rms(proj[:, _C_QLAT:_C_KV], qn_ref[...]).astype(_BF)
    qqt = _dot_nt(wuqt_ref[...], qn)
    half = MLA_HEADS * LANES
    for hd in range(MLA_HEADS):
        qt = (qqt[hd * LANES:(hd + 1) * LANES] * cost_ref[...]
              + qqt[half + hd * LANES:half + (hd + 1) * LANES] * sint_ref[...])
        qt_ref[0, hd] = qt.astype(_BF)


def _in_proj_prompt(x2d, b, l, p, tl):
    t = b * l
    n_l = l // tl
    cos32, sin32 = _rope_tables(l, 0)
    lead = np.zeros((l, QK_NOPE), np.float32)
    trail = np.zeros((l, LANES - QK_NOPE - QK_ROPE), np.float32)
    cosk = np.concatenate([lead, cos32, trail], axis=1)
    sink = np.concatenate([lead, sin32, trail], axis=1)
    cost = np.ascontiguousarray((np.concatenate([lead + 1.0, cos32, trail], axis=1) * np.float32(SCALE_LOG2)).T)
    sint = np.ascontiguousarray((sink * np.float32(SCALE_LOG2)).T)

    row_spec = lambda w: pl.BlockSpec((tl, w), lambda i: (i, 0))
    out_shape = (
        jax.ShapeDtypeStruct((t // tl, MLA_HEADS, LANES, tl), _BF),
        jax.ShapeDtypeStruct((t, MLA_HEADS * LANES), _BF),
        jax.ShapeDtypeStruct((t // tl, D_ATTN, tl), _BF),
        jax.ShapeDtypeStruct((t, KV_LORA), _F32),
        jax.ShapeDtypeStruct((t, QK_ROPE), _F32),
        jax.ShapeDtypeStruct((t, D_SSM), _F32),
        jax.ShapeDtypeStruct((t, CONV_DIM), _F32),
        jax.ShapeDtypeStruct((t, LANES), _F32),
    )
    out_specs = (
        pl.BlockSpec((1, MLA_HEADS, LANES, tl), lambda i: (i, 0, 0, 0)),
        row_spec(MLA_HEADS * LANES),
        pl.BlockSpec((1, D_ATTN, tl), lambda i: (i, 0, 0)),
        row_spec(KV_LORA), row_spec(QK_ROPE), row_spec(D_SSM), row_spec(CONV_DIM), row_spec(LANES),
    )
    in_specs = [
        row_spec(D_MODEL), _const_spec((1, D_MODEL)), _const_spec((D_MODEL, _N_IN_PAD)),
        _const_spec((1, Q_LORA)), _const_spec((1, KV_LORA)), _const_spec((2 * MLA_HEADS * LANES, Q_LORA)),
        _const_spec((KV_LORA, MLA_HEADS * LANES)), _const_spec((D_ATTN, KV_LORA)),
        pl.BlockSpec((LANES, tl), lambda i: (0, i % n_l)), pl.BlockSpec((LANES, tl), lambda i: (0, i % n_l)),
        pl.BlockSpec((tl, LANES), lambda i: (i % n_l, 0)), pl.BlockSpec((tl, LANES), lambda i: (i % n_l, 0)),
    ]
    return pl.pallas_call(
        _in_proj_prompt_kernel,
        grid=(t // tl,), in_specs=in_specs, out_specs=out_specs, out_shape=out_shape,
        compiler_params=pltpu.CompilerParams(dimension_semantics=("arbitrary",), vmem_limit_bytes=VMEM_LIMIT),
        name="in_proj_prompt",
    )(x2d, p["g_attn"], p["w_in_prompt"], p["q_norm"], p["kv_norm"], p["w_uq_t"], p["w_k"], p["w_uv_t"],
      cost, sint, cosk, sink)


def _attn_prompt_kernel(it_ref, jt_ref, qt_ref, kh_ref, vt_ref, o_ref, m_ref, l_ref, acc_ref, *, tq):
    s_idx = pl.program_id(1)
    i = it_ref[s_idx]
    j = jt_ref[s_idx]

    @pl.when(j == 0)
    def _():
        m_ref[...] = jnp.full(m_ref.shape, -jnp.inf, _F32)
        l_ref[...] = jnp.zeros(l_ref.shape, _F32)
        acc_ref[...] = jnp.zeros(acc_ref.shape, _F32)

    def step(masked):
        if masked:
            krow = lax.broadcasted_iota(jnp.int32, (tq, tq), 0)
            qcol = lax.broadcasted_iota(jnp.int32, (tq, tq), 1)
            keep = krow <= qcol

        def scores(hd):
            st = _dot(kh_ref[:, hd * LANES:(hd + 1) * LANES], qt_ref[0, hd])
            if masked:
                st = jnp.where(keep, st, -jnp.inf)
            return st.reshape(tq // SUBLANES, SUBLANES, tq)

        ahead = {hd: scores(hd) for hd in range(QK_AHEAD)}
        for hd in range(MLA_HEADS):
            if hd + QK_AHEAD < MLA_HEADS:
                ahead[hd + QK_AHEAD] = scores(hd + QK_AHEAD)
            st = ahead.pop(hd)
            m_prev = m_ref[hd]
            m_new = jnp.maximum(m_prev, _sublane_all(jnp.maximum, jnp.max(st, axis=0)))
            alpha = jnp.exp2(m_prev - m_new)
            pr = jnp.exp2(st - m_new[None])
            l_ref[hd] = alpha * l_ref[hd] + jnp.sum(pr, axis=0)
            rows = slice(hd * V_HEAD, (hd + 1) * V_HEAD)
            acc = acc_ref[rows, :].reshape(V_HEAD // SUBLANES, SUBLANES, tq) * alpha[None]
            acc_ref[rows, :] = acc.reshape(V_HEAD, tq) + _dot(vt_ref[0, rows, :], pr.reshape(tq, tq).astype(_BF))
            m_ref[hd] = m_new

    @pl.when(j < i)
    def _():
        step(False)

    @pl.when(j == i)
    def _():
        step(True)
        for hd in range(MLA_HEADS):
            rows = slice(hd * V_HEAD, (hd + 1) * V_HEAD)
            den = _sublane_all(jnp.add, l_ref[hd])
            acc = acc_ref[rows, :].reshape(V_HEAD // SUBLANES, SUBLANES, tq) / den[None]
            acc_ref[rows, :] = acc.reshape(V_HEAD, tq)
        o_ref[...] = acc_ref[...].T.astype(o_ref.dtype)


def _attn_prompt(qt, kh, vt, b, l, tq):
    nq = l // tq
    assert qt.shape == (b * nq, MLA_HEADS, LANES, tq) and vt.shape == (b * nq, D_ATTN, tq)
    pairs = [(i, j) for i in range(nq) for j in range(i + 1)]
    it = jnp.asarray([pr[0] for pr in pairs], jnp.int32)
    jt = jnp.asarray([pr[1] for pr in pairs], jnp.int32)
    grid_spec = pltpu.PrefetchScalarGridSpec(
        num_scalar_prefetch=2,
        grid=(b, len(pairs)),
        in_specs=[
            pl.BlockSpec((1, MLA_HEADS, LANES, tq), lambda bi, s, it, jt: (bi * nq + it[s], 0, 0, 0)),
            pl.BlockSpec((tq, MLA_HEADS * LANES), lambda bi, s, it, jt: (bi * nq + jt[s], 0)),
            pl.BlockSpec((1, D_ATTN, tq), lambda bi, s, it, jt: (bi * nq + jt[s], 0, 0)),
        ],
        out_specs=pl.BlockSpec((tq, D_ATTN), lambda bi, s, it, jt: (bi * nq + it[s], 0)),
        scratch_shapes=[pltpu.VMEM((MLA_HEADS, SUBLANES, tq), _F32), pltpu.VMEM((MLA_HEADS, SUBLANES, tq), _F32),
                        pltpu.VMEM((D_ATTN, tq), _F32)],
    )
    return pl.pallas_call(
        functools.partial(_attn_prompt_kernel, tq=tq),
        grid_spec=grid_spec,
        out_shape=jax.ShapeDtypeStruct((b * l, D_ATTN), _BF),
        compiler_params=pltpu.CompilerParams(dimension_semantics=("arbitrary", "arbitrary"),
                                             vmem_limit_bytes=VMEM_LIMIT),
        name="attn_prompt",
    )(it, jt, qt, kh, vt)


def _page_copies(pt_ref, cache_c_ref, cache_krt_ref, cbuf_ref, krbuf_ref, sem_ref, item, slot, pg, page):
    pid = pt_ref[item, pg]
    dst = pl.ds(pl.multiple_of(pg * page, page), page)
    return (pltpu.make_async_copy(cache_c_ref.at[pid], cbuf_ref.at[slot, dst, :], sem_ref.at[slot, 0]),
            pltpu.make_async_copy(cache_krt_ref.at[pid], krbuf_ref.at[slot, :, dst], sem_ref.at[slot, 1]))


def _wait_item(cbuf_ref, krbuf_ref, sem_ref, slot):
    pltpu.make_async_copy(cbuf_ref.at[slot], cbuf_ref.at[slot], sem_ref.at[slot, 0]).wait()
    pltpu.make_async_copy(krbuf_ref.at[slot], krbuf_ref.at[slot], sem_ref.at[slot, 1]).wait()


def _attend_item(q, kvn, slot, cbuf_ref, krbuf_ref, kc_ref, *, n_pages, page, l_new, n_split):
    m_rows = MLA_HEADS * l_new
    q_abs = q[:, :KV_LORA]
    q_pe = q[:, KV_LORA:KV_LORA + QK_ROPE]
    ks = (n_pages // n_split) * page

    def scores(sp):
        rows = pl.ds(sp * ks, ks)
        kc_ref[rows, :] = cbuf_ref[slot, rows, :].astype(_BF)
        return _dot_nt(q_abs, kc_ref[rows, :]) + _dot(q_pe, krbuf_ref[slot, :, rows].astype(_BF))

    def partial(s, values):
        m = jnp.max(s, axis=1, keepdims=True)
        pr = jnp.exp2(s - m)
        return m, jnp.sum(pr, axis=1, keepdims=True), _dot(pr.astype(_BF), values)

    def merge(run, new):
        m = jnp.maximum(run[0], new[0])
        a, c = jnp.exp2(run[0] - m), jnp.exp2(new[0] - m)
        return m, a * run[1] + c * new[1], a * run[2] + c * new[2]

    ahead = {sp: scores(sp) for sp in range(min(SPLITS_AHEAD, n_split))}
    sn = _dot_nt(q, kvn)
    qpos = lax.broadcasted_iota(jnp.int32, (MLA_HEADS, l_new, l_new), 1).reshape(m_rows, l_new)
    kpos = lax.broadcasted_iota(jnp.int32, (m_rows, l_new), 1)
    run = partial(jnp.where(kpos <= qpos, sn, -jnp.inf), kvn[:, :KV_LORA])
    for sp in range(n_split):
        if sp + SPLITS_AHEAD < n_split:
            ahead[sp + SPLITS_AHEAD] = scores(sp + SPLITS_AHEAD)
        run = merge(run, partial(ahead.pop(sp), kc_ref[pl.ds(sp * ks, ks), :]))
    return run[2] / run[1]


def _uv_proj_kernel(o_ref, wuv_ref, out_ref, *, tb, l_new):
    for hd in range(MLA_HEADS):
        o = o_ref[:, hd].reshape(tb * l_new, KV_LORA)
        out_ref[:, hd * V_HEAD:(hd + 1) * V_HEAD] = _dot(o, wuv_ref[hd]).astype(out_ref.dtype)


def _uv_proj(o_lat, wuv, tb):
    b, _, l_new, _ = o_lat.shape
    return pl.pallas_call(
        functools.partial(_uv_proj_kernel, tb=tb, l_new=l_new),
        grid=(b // tb,),
        in_specs=[pl.BlockSpec((tb, MLA_HEADS, l_new, KV_LORA), lambda i: (i, 0, 0, 0)),
                  _const_spec((MLA_HEADS, KV_LORA, V_HEAD))],
        out_specs=pl.BlockSpec((tb * l_new, D_ATTN), lambda i: (i, 0)),
        out_shape=jax.ShapeDtypeStruct((b * l_new, D_ATTN), _BF),
        compiler_params=pltpu.CompilerParams(dimension_semantics=("arbitrary",), vmem_limit_bytes=VMEM_LIMIT),
        name="uv_proj",
    )(o_lat, wuv)


def _ssd_kernel(xbc_ref, z_ref, dtr_ref, cbuf_ref, h0_ref, cw_ref, cb_ref, dtb_ref, alog_ref, dskip_ref, nrm_ref,
                sel_ref, ex_ref, exw_ref, o_ref, h_ref, cnew_ref, xp_ref, y_ref, *, n_seq, n_chunk, q):
    assert n_seq == 1 or n_chunk == 1
    g_items = n_seq * n_chunk
    c_idx = pl.program_id(1)
    last = pl.num_programs(1) - 1
    heads_per_group = SSM_HEADS // SSM_GROUPS
    rows = g_items * q
    row = lax.broadcasted_iota(jnp.int32, (q, q), 0)
    col = lax.broadcasted_iota(jnp.int32, (q, q), 1)
    causal = row >= col
    n_bc = SSM_GROUPS * D_STATE

    def conv_item(g):
        sq, ck = divmod(g, n_chunk)
        rs = slice(g * q, (g + 1) * q)

        if ck == 0:
            @pl.when(c_idx == 0)
            def _():
                h_ref[sq] = h0_ref[sq]
                xp_ref[sq, _XP_OFF - (CONV_K - 1):_XP_OFF, :] = cbuf_ref[sq]

        act, tail = _conv_silu(xp_ref.at[sq], xbc_ref[rs, :], cw_ref, cb_ref, q)

        if ck == n_chunk - 1:
            @pl.when(c_idx == last)
            def _():
                cnew_ref[sq] = tail

        return act[:, :D_SSM], act[:, D_SSM:D_SSM + n_bc].astype(_BF), act[:, D_SSM + n_bc:].astype(_BF)

    convs = {g: conv_item(g) for g in range(g_items)} if n_seq == 1 else {}

    if g_items == 1:
        tril = jnp.where(causal, 1.0, 0.0).astype(_BF)
    else:
        r_all = lax.broadcasted_iota(jnp.int32, (rows, rows), 0)
        c_all = lax.broadcasted_iota(jnp.int32, (rows, rows), 1)
        same_item = (r_all // q) == (c_all // q)
        tril = jnp.where(r_all >= c_all, jnp.where(same_item, 1.0, 0.0), 0.0).astype(_BF)
    a_neg = -jnp.exp(alog_ref[...])
    dtv = dtr_ref[...] + dtb_ref[...]
    dt = jnp.maximum(dtv, 0.0) + jnp.log1p(jnp.exp(-jnp.abs(dtv)))
    cs = _dot(tril, _split3(dt * a_neg))
    acs = cs[:, :LANES] + cs[:, LANES:2 * LANES] + cs[:, 2 * LANES:]
    acs3 = _split3(acs)
    acs_t = _dot_nt(sel_ref[...], acs3)
    dt_x = _dot(_split3(dt), ex_ref[...])
    acs_x = _dot(acs3, ex_ref[...])
    acs_w = _dot(acs3, exw_ref[...])
    e_acs = jnp.exp(acs_x)

    for g in range(g_items):
        sq = g // n_chunk
        rs = slice(g * q, (g + 1) * q)
        xs, bmat, cmat = convs[g] if g in convs else conv_item(g)
        acs_last_x = acs_x[(g + 1) * q - 1:(g + 1) * q, :]
        x_dt = xs * dt_x[rs]
        xw = (x_dt * jnp.exp(acs_last_x - acs_x[rs])).astype(_BF)
        x_dt = x_dt.astype(_BF)
        for grp in range(SSM_GROUPS):
            b_g = bmat[:, grp * D_STATE:(grp + 1) * D_STATE]
            c_g = cmat[:, grp * D_STATE:(grp + 1) * D_STATE]
            cb = _dot_nt(c_g, b_g)
            for hd in range(grp * heads_per_group, (grp + 1) * heads_per_group):
                cols = slice(hd * SSM_HEADDIM, (hd + 1) * SSM_HEADDIM)
                acs_h = acs_w[rs, hd * LANES:(hd + 1) * LANES]
                acs_h = acs_h[:, :q] if q <= LANES else jnp.concatenate([acs_h] * (q // LANES), axis=1)
                lmat = jnp.exp(jnp.where(causal, acs_h - acs_t[hd:hd + 1, rs], -jnp.inf))
                h_prev = h_ref[sq, hd]
                y = _dot((cb * lmat).astype(_BF), x_dt[:, cols])
                y = y + _dot_nt(c_g, h_prev.astype(_BF)) * e_acs[rs, cols]
                decay = jnp.exp(acs[(g + 1) * q - 1:(g + 1) * q, hd:hd + 1])
                h_ref[sq, hd] = decay * h_prev + _dot_tn(xw[:, cols], b_g)
                y_ref[:, cols] = y
        yv = y_ref[...] + dskip_ref[...] * xs
        gated = yv * _silu(z_ref[g * q:(g + 1) * q, :])
        gw = D_SSM // SSM_GROUPS
        parts = []
        for grp in range(SSM_GROUPS):
            gg = gated[:, grp * gw:(grp + 1) * gw]
            parts.append(gg * lax.rsqrt(jnp.mean(gg * gg, axis=-1, keepdims=True) + EPS))
        o_ref[g * q:(g + 1) * q, :] = (jnp.concatenate(parts, axis=1) * nrm_ref[...]).astype(o_ref.dtype)


def _ssd(xbc, z, dtr, conv_buf, h0, p, b, l, q, n_seq, n_chunk):
    nc = l // (q * n_chunk)
    rows = n_seq * n_chunk * q
    row_spec = lambda w: pl.BlockSpec((rows, w), lambda bi, c: (bi * nc + c, 0))
    state_spec = pl.BlockSpec((n_seq, SSM_HEADS, SSM_HEADDIM, D_STATE), lambda bi, c: (bi, 0, 0, 0))
    hist_spec = pl.BlockSpec((n_seq, CONV_K - 1, CONV_DIM), lambda bi, c: (bi, 0, 0))
    consts = [p["dt_bias"], p["a_log"], p["d_skip"], p["ssm_norm"], p["sel"], p["head_to_cols"], p["head_to_lanes"]]
    const_specs = [_const_spec((1, LANES)), _const_spec((1, LANES)), _const_spec((1, D_SSM)), _const_spec((1, D_SSM)),
                   _const_spec((SSM_HEADS, 3 * LANES)), _const_spec((3 * LANES, D_SSM)),
                   _const_spec((3 * LANES, SSM_HEADS * LANES))]
    in_specs = [row_spec(CONV_DIM), row_spec(D_SSM), row_spec(LANES), hist_spec, state_spec,
                _const_spec((CONV_K, CONV_DIM)), _const_spec((1, CONV_DIM)), *const_specs]
    out_shape = (jax.ShapeDtypeStruct((b * l, D_SSM), _BF),
                 jax.ShapeDtypeStruct((b, SSM_HEADS, SSM_HEADDIM, D_STATE), _F32),
                 jax.ShapeDtypeStruct((b, CONV_K - 1, CONV_DIM), _F32))
    return pl.pallas_call(
        functools.partial(_ssd_kernel, n_seq=n_seq, n_chunk=n_chunk, q=q),
        grid=(b // n_seq, nc), in_specs=in_specs, out_specs=(row_spec(D_SSM), state_spec, hist_spec),
        out_shape=out_shape,
        scratch_shapes=[pltpu.VMEM((n_seq, _XP_OFF + q, CONV_DIM), _F32), pltpu.VMEM((q, D_SSM), _F32)],
        compiler_params=pltpu.CompilerParams(dimension_semantics=("arbitrary", "arbitrary"),
                                             vmem_limit_bytes=VMEM_LIMIT),
        name="ssd",
    )(xbc, z, dtr, conv_buf, h0, p["conv_w"], p["conv_b"], *consts)


def _mlp_chunk(h2, acc, wu_ref, wd_ref, f0, f_chunk):
    u = jnp.maximum(_dot(h2, wu_ref[:, f0:f0 + f_chunk]), 0.0)
    return acc + _dot((u * u).astype(_BF), wd_ref[f0:f0 + f_chunk, :])


def _out_mlp_kernel(x_ref, oa_ref, os_ref, wo_ref, gm_ref, wu_ref, wd_ref, gf_ref, y_ref, *, f_chunk):
    mix = jnp.concatenate([oa_ref[...], os_ref[...]], axis=1)
    x1 = x_ref[...] + _dot(mix, wo_ref[...])
    h2 = _rms(x1, gm_ref[...]).astype(_BF)
    acc = x1
    for f0 in range(0, D_FF, f_chunk):
        acc = _mlp_chunk(h2, acc, wu_ref, wd_ref, f0, f_chunk)
    y_ref[...] = _rms(acc, gf_ref[...])


def _mlp_attn_kernel(pt_ref, x_ref, oa_ref, os_ref, wo_ref, gm_ref, wu_ref, wd_ref, gf_ref, q_ref, kvn_ref,
                     cache_c_ref, cache_krt_ref, y_ref, o_ref, cbuf_ref, krbuf_ref, kc_ref, sem_ref, *,
                     f_chunk, ipb, n_items, n_pages, page, l_new, n_split):
    i = pl.program_id(0)
    copies = functools.partial(_page_copies, pt_ref, cache_c_ref, cache_krt_ref, cbuf_ref, krbuf_ref, sem_ref)

    @pl.when(i == 0)
    def _():
        def body(pg, carry):
            for cp in copies(0, 0, pg, page):
                cp.start()
            return carry
        lax.fori_loop(0, n_pages, body, 0)

    mix = jnp.concatenate([oa_ref[...], os_ref[...]], axis=1)
    x1 = x_ref[...] + _dot(mix, wo_ref[...])
    h2 = _rms(x1, gm_ref[...]).astype(_BF)
    acc = x1
    f_starts = list(range(0, D_FF, f_chunk))
    per_item = -(-len(f_starts) // ipb)
    for k in range(ipb):
        item = i * ipb + k
        slot = k % 2 if ipb % 2 == 0 else item % 2
        _wait_item(cbuf_ref, krbuf_ref, sem_ref, slot)
        nxt = jnp.minimum(item + 1, n_items - 1)
        for pg in range(n_pages):
            for cp in copies(nxt, 1 - slot, pg, page):
                cp.start()
        o = _attend_item(q_ref[k], kvn_ref[k], slot, cbuf_ref, krbuf_ref, kc_ref, n_pages=n_pages, page=page,
                         l_new=l_new, n_split=n_split)
        o_ref[k] = o.astype(o_ref.dtype)
        for f0 in f_starts[k * per_item:(k + 1) * per_item]:
            acc = _mlp_chunk(h2, acc, wu_ref, wd_ref, f0, f_chunk)
    for f0 in f_starts[ipb * per_item:]:
        acc = _mlp_chunk(h2, acc, wu_ref, wd_ref, f0, f_chunk)
    y_ref[...] = _rms(acc, gf_ref[...])

    @pl.when(i == pl.num_programs(0) - 1)
    def _():
        _wait_item(cbuf_ref, krbuf_ref, sem_ref, n_items % 2)


def _mlp_attn(x2d, oa, osm, q, kvn, cache_c, cache_krt, page_table, p, tm):
    t = x2d.shape[0]
    n_steps = t // tm
    b, m_rows, _ = q.shape
    assert b % n_steps == 0, "sample batch must split evenly over the MLP token tiles"
    ipb = b // n_steps
    l_new = m_rows // MLA_HEADS
    n_pages = page_table.shape[1]
    page = cache_c.shape[1]
    keys = n_pages * page
    row_spec = lambda w: pl.BlockSpec((tm, w), lambda i, pt: (i, 0))
    once = lambda shape: pl.BlockSpec(shape, lambda i, pt: (0,) * len(shape), pipeline_mode=pl.Buffered(1))
    item_spec = lambda r, w: pl.BlockSpec((ipb, r, w), lambda i, pt: (i, 0, 0))
    grid_spec = pltpu.PrefetchScalarGridSpec(
        num_scalar_prefetch=1,
        grid=(n_steps,),
        in_specs=[row_spec(D_MODEL), row_spec(D_ATTN), row_spec(D_SSM), once((D_MODEL, D_MODEL)),
                  once((1, D_MODEL)), once((D_MODEL, D_FF)), once((D_FF, D_MODEL)), once((1, D_MODEL)),
                  item_spec(m_rows, QK_PAD), item_spec(l_new, QK_PAD),
                  pl.BlockSpec(memory_space=pl.ANY), pl.BlockSpec(memory_space=pl.ANY)],
        out_specs=(row_spec(D_MODEL), item_spec(m_rows, KV_LORA)),
        scratch_shapes=[pltpu.VMEM((2, keys, KV_LORA), _F32), pltpu.VMEM((2, QK_ROPE, keys), _F32),
                        pltpu.VMEM((keys, KV_LORA), _BF), pltpu.SemaphoreType.DMA((2, 2))],
    )
    return pl.pallas_call(
        functools.partial(_mlp_attn_kernel, f_chunk=1024, ipb=ipb, n_items=b, n_pages=n_pages, page=page,
                          l_new=l_new, n_split=_largest_divisor(n_pages, 8)),
        grid_spec=grid_spec,
        out_shape=(jax.ShapeDtypeStruct((t, D_MODEL), _F32), jax.ShapeDtypeStruct((b, m_rows, KV_LORA), _BF)),
        compiler_params=pltpu.CompilerParams(dimension_semantics=("arbitrary",), vmem_limit_bytes=VMEM_LIMIT_FUSED),
        name="mlp_attn",
    )(page_table, x2d, oa, osm, p["w_out"], p["g_mlp"], p["w_up"], p["w_down"], p["g_final"], q, kvn, cache_c,
      cache_krt)


def _out_mlp(x2d, oa, osm, p, tm):
    t = x2d.shape[0]
    row_spec = lambda w: pl.BlockSpec((tm, w), lambda i: (i, 0))
    once = lambda shape: pl.BlockSpec(shape, lambda i: (0,) * len(shape), pipeline_mode=pl.Buffered(1))
    return pl.pallas_call(
        functools.partial(_out_mlp_kernel, f_chunk=1024),
        grid=(t // tm,),
        in_specs=[row_spec(D_MODEL), row_spec(D_ATTN), row_spec(D_SSM), once((D_MODEL, D_MODEL)),
                  once((1, D_MODEL)), once((D_MODEL, D_FF)), once((D_FF, D_MODEL)), once((1, D_MODEL))],
        out_specs=row_spec(D_MODEL),
        out_shape=jax.ShapeDtypeStruct((t, D_MODEL), _F32),
        compiler_params=pltpu.CompilerParams(dimension_semantics=("arbitrary",), vmem_limit_bytes=VMEM_LIMIT),
        name="out_mlp",
    )(x2d, oa, osm, p["w_out"], p["g_mlp"], p["w_up"], p["w_down"], p["g_final"])


def _prep_params(norm_attn, w_in, q_norm, kv_norm, w_uq, w_uk, w_uv, conv_w, conv_b, dt_bias, a_log, d_skip,
                 ssm_norm, w_out, norm_mlp, w_up, w_down, norm_final):
    half = QK_ROPE // 2

    def swap(w):
        return jnp.concatenate([w[..., half:], w[..., :half]], axis=-1)

    def place(w, lead, n):
        return jnp.pad(w, ((0, 0), (lead, n - lead - w.shape[1])))

    o1 = Q_LORA
    o2 = o1 + KV_LORA
    o3 = o2 + QK_ROPE
    o5 = o3 + D_SSM + CONV_DIM
    w_kpe = w_in[:, o2:o3]

    def w_in_padded(lead):
        return jnp.concatenate([
            w_in[:, :o2], place(w_kpe, lead, LANES), place(swap(w_kpe), lead, LANES), w_in[:, o3:o5],
            place(w_in[:, o5:], 0, LANES)], axis=1).astype(_BF)

    uq = w_uq.reshape(Q_LORA, MLA_HEADS, QK_NOPE + QK_ROPE)
    uq_nope = uq[:, :, :QK_NOPE]
    uq_rope = uq[:, :, QK_NOPE:]
    w_uq_p = jnp.concatenate([
        jnp.pad(uq_nope, ((0, 0), (0, 0), (0, LANES - QK_NOPE))).reshape(Q_LORA, MLA_HEADS * LANES),
        uq_rope.reshape(Q_LORA, -1), swap(uq_rope).reshape(Q_LORA, -1)], axis=1).astype(_BF)
    w_uk_p = jnp.pad(w_uk, ((0, 0), (0, LANES - QK_NOPE), (0, 0))).astype(_BF)
    tail = LANES - QK_NOPE - QK_ROPE
    plain = jnp.pad(uq, ((0, 0), (0, 0), (0, tail)))
    swapped = jnp.pad(swap(uq_rope), ((0, 0), (0, 0), (QK_NOPE, tail)))
    w_uq_t = jnp.concatenate([plain.reshape(Q_LORA, -1), swapped.reshape(Q_LORA, -1)], axis=1).T.astype(_BF)
    w_k = jnp.pad(jnp.transpose(w_uk, (2, 0, 1)), ((0, 0), (0, 0), (0, LANES - QK_NOPE)))
    w_k = w_k.reshape(KV_LORA, MLA_HEADS * LANES).astype(_BF)
    w_uv_t = jnp.transpose(w_uv, (0, 2, 1)).reshape(D_ATTN, KV_LORA).astype(_BF)
    pad_heads = lambda v: jnp.pad(v.reshape(1, SSM_HEADS), ((0, 0), (0, LANES - SSM_HEADS)))
    return dict(
        g_attn=norm_attn.reshape(1, D_MODEL), w_in=w_in_padded(0), w_in_prompt=w_in_padded(QK_NOPE),
        q_norm=q_norm.reshape(1, Q_LORA), kv_norm=kv_norm.reshape(1, KV_LORA), w_uq=w_uq_p, w_uk=w_uk_p,
        w_uq_t=w_uq_t, w_k=w_k, w_uv_t=w_uv_t, w_uv=w_uv.astype(_BF),
        conv_w=conv_w, conv_b=conv_b.reshape(1, CONV_DIM), dt_bias=pad_heads(dt_bias), a_log=pad_heads(a_log),
        d_skip=jnp.repeat(d_skip, SSM_HEADDIM).reshape(1, D_SSM), ssm_norm=ssm_norm.reshape(1, D_SSM),
        sel=jnp.asarray(np.tile(np.eye(SSM_HEADS, LANES), (1, 3)), _BF),
        head_to_cols=jnp.asarray(np.tile(np.repeat(np.eye(LANES, SSM_HEADS), SSM_HEADDIM, axis=1), (3, 1)), _BF),
        head_to_lanes=jnp.asarray(np.tile(np.repeat(np.eye(LANES, SSM_HEADS), LANES, axis=1), (3, 1)), _BF),
        w_out=w_out.astype(_BF), g_mlp=norm_mlp.reshape(1, D_MODEL), w_up=w_up.astype(_BF),
        w_down=w_down.astype(_BF), g_final=norm_final.reshape(1, D_MODEL),
    )


def _largest_divisor(n, cap):
    d = min(n, cap)
    while n % d:
        d -= 1
    return d


def kernel(x_prompt, x_sample, cache_kv_latent, cache_k_rope, state_ssm, state_conv, page_table, norm_attn, w_in, q_norm, kv_norm, w_uq, w_uk, w_uv, conv_w, conv_b, dt_bias, a_log, d_skip, ssm_norm, w_out, norm_mlp, w_up, w_down, norm_final):
    assert norm_attn.shape[0] == 1, "single-layer model"
    p = _prep_params(norm_attn[0], w_in[0], q_norm[0], kv_norm[0], w_uq[0], w_uk[0], w_uv[0], conv_w[0], conv_b[0],
                     dt_bias[0], a_log[0], d_skip[0], ssm_norm[0], w_out[0], norm_mlp[0], w_up[0], w_down[0],
                     norm_final)
    bp, lp, _ = x_prompt.shape
    bs, ls, _ = x_sample.shape
    n_pages = page_table.shape[1]
    page = cache_kv_latent.shape[2]
    past_len = n_pages * page

    xp2 = x_prompt.reshape(bp * lp, D_MODEL)
    xs2 = x_sample.reshape(bs * ls, D_MODEL)
    qt, kh, vt, c_p, kr_p, z_p, xbc_p, dtr_p = _in_proj_prompt(xp2, bp, lp, p, _largest_divisor(lp, 512))
    tb = _largest_divisor(bs, 512 // ls)
    q_s, kv_s, c_s, kr_s, z_s, xbc_s, dtr_s = _in_proj(xs2, bs, ls, past_len, p, tb, ls)

    o_attn_p = _attn_prompt(qt, kh, vt, bp, lp, _largest_divisor(lp, 512))
    chunk = _largest_divisor(lp, CHUNK)
    o_ssm_p, h_p, conv_p = _ssd(xbc_p, z_p, dtr_p, jnp.zeros((bp, CONV_K - 1, CONV_DIM), _F32),
                                jnp.zeros((bp, SSM_HEADS, SSM_HEADDIM, D_STATE), _F32), p, bp, lp,
                                chunk, 1, _largest_divisor(lp // chunk, 2))

    cache_krt = jnp.swapaxes(cache_k_rope[0], 1, 2)
    y_p, o_lat = _mlp_attn(xp2, o_attn_p, o_ssm_p, q_s.reshape(bs, MLA_HEADS * ls, QK_PAD),
                           kv_s.reshape(bs, ls, QK_PAD), cache_kv_latent[0], cache_krt, page_table, p,
                           _largest_divisor(bp * lp, 512))
    outs_p = (y_p.reshape(bp, lp, D_MODEL), c_p.reshape(1, bp, lp, KV_LORA), kr_p.reshape(1, bp, lp, QK_ROPE),
              h_p[None], conv_p[None])

    o_attn_s = _uv_proj(o_lat.reshape(bs, MLA_HEADS, ls, KV_LORA), p["w_uv"], tb)
    o_ssm_s, h_s, conv_s = _ssd(xbc_s, z_s, dtr_s, state_conv[0], state_ssm[0], p, bs, ls, ls,
                                _largest_divisor(bs, 16), 1)
    y_s = _out_mlp(xs2, o_attn_s, o_ssm_s, p, _largest_divisor(bs * ls, 512))
    outs_s = (y_s.reshape(bs, ls, D_MODEL), c_s.reshape(1, bs, ls, KV_LORA), kr_s.reshape(1, bs, ls, QK_ROPE),
              h_s[None], conv_s[None])

    return (outs_p[0], outs_s[0], outs_p[1], outs_p[2], outs_p[3], outs_p[4],
            outs_s[1], outs_s[2], outs_s[3], outs_s[4])
```
